```python
import math
import jax, jax.numpy as jnp
from jax import lax
import numpy as np

D_MODEL = 1024
BATCH = 2
SEQ = 8192
DEPTH = 4
DEC_BATCH = 128
DEC_SEQ = 1
PAST_LEN = 8192
PAGE_SIZE = 128

N_A_LAYERS = DEPTH // 2
N_B_LAYERS = DEPTH - N_A_LAYERS
A_EXPAND = 128
A_HEADS = D_MODEL // A_EXPAND
A_DK = A_EXPAND
A_DV = D_MODEL // A_HEADS
A_CHUNK = 64
F_MIN = 1e-30
B_HEAD_DIM = 64
B_HEADS = D_MODEL // B_HEAD_DIM
B_KV_HEADS = 4
B_GROUPS = B_HEADS // B_KV_HEADS
WINDOW = 128
MASK_VALUE = -1e30
N_BUCKETS = 32
MAX_DISTANCE = 128
D_FF = -(-8 * D_MODEL // (3 * 256)) * 256
EPS = 1e-6

kernel_name = "yoco_hgrn2_swa_sink_adaln_decoder_step"


def rms_norm(x, g):
    x32 = x.astype(jnp.float32)
    y = x32 * lax.rsqrt(jnp.mean(x32 * x32, axis=-1, keepdims=True) + EPS)
    return y * g.astype(jnp.float32)


def modulate(x, g, shift, scale):
    y = rms_norm(x, g) * (1.0 + scale[:, None, :]) + shift[:, None, :]
    return y.astype(x.dtype)


def ada_params(c, w, b):
    return (jax.nn.silu(c) @ w + b).astype(jnp.float32)


def swiglu(x, w_in, w_out):
    gate, up = jnp.split(x @ w_in, 2, axis=-1)
    return (jax.nn.silu(gate) * up) @ w_out


def t5_bucket(dist):
    max_exact = N_BUCKETS // 2
    d = jnp.clip(dist, 0, WINDOW - 1)
    large = max_exact + (jnp.log(jnp.maximum(d, 1).astype(jnp.float32) / max_exact)
                         / math.log(MAX_DISTANCE / max_exact)
                         * (N_BUCKETS - max_exact)).astype(jnp.int32)
    large = jnp.clip(large, 0, N_BUCKETS - 1)
    return jnp.where(d < max_exact, d, large)


def rel_bias_for(dist, rel_bias):
    bias = rel_bias.astype(jnp.float32)[t5_bucket(dist)]
    q_len, k_len = dist.shape
    return bias.transpose(2, 0, 1).reshape(B_KV_HEADS, B_GROUPS, q_len, k_len)


def sink_softmax(logits, sinks):
    sink = sinks.astype(jnp.float32).reshape(B_KV_HEADS, B_GROUPS)[:, :, None, None]
    m = jnp.maximum(jnp.max(logits, axis=-1, keepdims=True), sink)
    p = jnp.exp(logits - m)
    denom = jnp.sum(p, axis=-1, keepdims=True) + jnp.exp(sink - m)
    return p / denom


def hgrn2_chunk_scan(q, k, v, log_f, s0, chunk):
    bsz, t, h, _ = q.shape
    dv = v.shape[-1]
    n = t // chunk

    def blocks(a):
        return a.reshape(bsz, n, chunk, h, a.shape[-1]).transpose(1, 0, 3, 2, 4)

    causal = jnp.tril(jnp.ones((chunk, chunk), dtype=bool))[None, None, :, :, None]

    def step(s, inp):
        qc, kc, vc, gc = inp
        b = jnp.cumsum(gc, axis=2)
        o_inter = jnp.einsum('bhtk,bhkv->bhtv', qc * jnp.exp(b), s)
        diff = b[:, :, :, None, :] - b[:, :, None, :, :]
        decay = jnp.where(causal, jnp.exp(jnp.minimum(diff, 0.0)), 0.0)
        att = jnp.einsum('bhtsk,bhtk,bhsk->bhts', decay, qc, kc)
        o = o_inter + jnp.einsum('bhts,bhsv->bhtv', att, vc)
        b_last = b[:, :, -1:, :]
        s = (jnp.exp(b_last[:, :, 0, :])[..., None] * s
             + jnp.einsum('bhsk,bhsv->bhkv', kc * jnp.exp(b_last - b), vc))
        return s, o

    s_final, o = lax.scan(step, s0, (blocks(q), blocks(k), blocks(v), blocks(log_f)))
    return o.transpose(1, 0, 3, 2, 4).reshape(bsz, t, h, dv), s_final


def hgrn2_mixer(xn, w_in, w_o, gnorm_w, lb, s0, chunk):
    bsz, t, _ = xn.shape
    q, f_raw, i, g = jnp.split(xn @ w_in, 4, axis=-1)
    heads = (bsz, t, A_HEADS, A_DK)
    q = jax.nn.silu(q.astype(jnp.float32)).reshape(heads)
    lb = lb.astype(jnp.float32)
    f = lb + (1.0 - lb) * jax.nn.sigmoid(f_raw.astype(jnp.float32))
    log_f = jnp.log(jnp.maximum(f, F_MIN)).reshape(heads)
    k = (1.0 - f).reshape(heads)
    v = i.astype(jnp.float32).reshape(bsz, t, A_HEADS, A_DV)
    o, s_new = hgrn2_chunk_scan(q, k, v, log_f, s0.astype(jnp.float32), chunk)
    gate = jax.nn.silu(g.astype(jnp.float32)).reshape(bsz, t, A_HEADS, A_DV)
    o = rms_norm(o, gnorm_w) * gate
    return o.reshape(bsz, t, D_MODEL).astype(xn.dtype) @ w_o, s_new


def swa_prompt(q, k, v, sinks, rel_bias):
    bsz, t = q.shape[:2]
    nb = t // WINDOW
    qb = q.reshape(bsz, nb, WINDOW, B_KV_HEADS, B_GROUPS, B_HEAD_DIM)

    def band(a):
        prev = jnp.pad(a, ((0, 0), (WINDOW, 0), (0, 0), (0, 0)))[:, :t]
        shp = (bsz, nb, WINDOW, B_KV_HEADS, B_HEAD_DIM)
        return jnp.concatenate([prev.reshape(shp), a.reshape(shp)], axis=2)

    kb, vb = band(k), band(v)
    scale = 1.0 / math.sqrt(B_HEAD_DIM)
    logits = jnp.einsum('bnqkgd,bnskd->bnkgqs', qb, kb).astype(jnp.float32) * scale
    t_loc = jnp.arange(WINDOW)[:, None]
    s_loc = jnp.arange(2 * WINDOW)[None, :]
    dist = t_loc + WINDOW - s_loc
    band_ok = (dist >= 0) & (dist < WINDOW)
    valid = band_ok[None] & ((jnp.arange(nb) > 0)[:, None, None] | (s_loc >= WINDOW)[None])
    logits = jnp.where(valid[None, :, None, None], logits + rel_bias_for(dist, rel_bias), MASK_VALUE)
    p = sink_softmax(logits, sinks)
    out = jnp.einsum('bnkgqs,bnskd->bnqkgd', p.astype(vb.dtype), vb)
    return out.reshape(bsz, t, B_HEADS * B_HEAD_DIM)


def swa_sample(q, k_buf, v_buf, k_new, v_new, sinks, rel_bias):
    bsz, s_len = q.shape[:2]
    w = k_buf.shape[1]
    kk = jnp.concatenate([k_buf, k_new.astype(k_buf.dtype)], axis=1)
    vv = jnp.concatenate([v_buf, v_new.astype(v_buf.dtype)], axis=1)
    qg = q.reshape(bsz, s_len, B_KV_HEADS, B_GROUPS, B_HEAD_DIM)
    scale = 1.0 / math.sqrt(B_HEAD_DIM)
    logits = jnp.einsum('bqkgd,bskd->bkgqs', qg, kk.astype(qg.dtype)).astype(jnp.float32) * scale
    dist = (w + jnp.arange(s_len))[:, None] - jnp.arange(w + s_len)[None, :]
    valid = (dist >= 0) & (dist < WINDOW)
    logits = jnp.where(valid, logits + rel_bias_for(dist, rel_bias), MASK_VALUE)
    p = sink_softmax(logits, sinks)
    out = jnp.einsum('bkgqs,bskd->bqkgd', p.astype(vv.dtype), vv)
    return out.reshape(bsz, s_len, B_HEADS * B_HEAD_DIM)


def trunk(x, c, hgrn_state0, k_buf, v_buf, w_in_a, w_o_a, gnorm_a, lb_a, w_kv, w_ada_kv, b_ada_kv,
          kv_norm_w, w_q_b, w_o_b, sinks_b, rel_bias, norm_w, w_ada, b_ada, w_ffn_in, w_ffn_out,
          final_norm_w):
    prompt = k_buf is None
    bsz, t, _ = x.shape
    lb_sm = jax.nn.softmax(lb_a.astype(jnp.float32), axis=0)
    lbs = jnp.cumsum(lb_sm, axis=0) - lb_sm[0:1]
    h = x
    new_states = []
    k_sh = v_sh = k_state = v_state = None
    for l in range(DEPTH):
        sh1, sc1, g1, sh2, sc2, g2 = jnp.split(ada_params(c, w_ada[l], b_ada[l]), 6, axis=-1)
        xn = modulate(h, norm_w[l, 0], sh1, sc1)
        if l < N_A_LAYERS:
            if prompt:
                s0 = jnp.zeros((bsz, A_HEADS, A_DK, A_DV), jnp.float32)
                chunk = A_CHUNK
            else:
                s0 = hgrn_state0[:, l]
                chunk = t
            mix, s_new = hgrn2_mixer(xn, w_in_a[l], w_o_a[l], gnorm_a[l], lbs[l], s0, chunk)
            new_states.append(s_new)
        else:
            j = l - N_A_LAYERS
            q = (xn @ w_q_b[j]).reshape(bsz, t, B_HEADS, B_HEAD_DIM)
            if prompt:
                att = swa_prompt(q, k_sh, v_sh, sinks_b[j], rel_bias)
            else:
                att = swa_sample(q, k_buf, v_buf, k_sh, v_sh, sinks_b[j], rel_bias)
            mix = att.astype(x.dtype) @ w_o_b[j]
        h = h + (g1[:, None, :] * mix.astype(jnp.float32)).astype(h.dtype)
        xn = modulate(h, norm_w[l, 1], sh2, sc2)
        h = h + (g2[:, None, :] * swiglu(xn, w_ffn_in[l], w_ffn_out[l]).astype(jnp.float32)).astype(h.dtype)
        if l == N_A_LAYERS - 1:
            sh_kv, sc_kv = jnp.split(ada_params(c, w_ada_kv, b_ada_kv), 2, axis=-1)
            kvn = modulate(h, kv_norm_w, sh_kv, sc_kv)
            k_sh, v_sh = jnp.split(kvn @ w_kv, 2, axis=-1)
            k_sh = k_sh.reshape(bsz, t, B_KV_HEADS, B_HEAD_DIM)
            v_sh = v_sh.reshape(bsz, t, B_KV_HEADS, B_HEAD_DIM)
            if prompt:
                k_state, v_state = k_sh[:, -WINDOW:], v_sh[:, -WINDOW:]
            else:
                k_state = jnp.concatenate([k_buf, k_sh.astype(k_buf.dtype)], axis=1)[:, -WINDOW:]
                v_state = jnp.concatenate([v_buf, v_sh.astype(v_buf.dtype)], axis=1)[:, -WINDOW:]
    y = rms_norm(h, final_norm_w).astype(x.dtype)
    return y, jnp.stack(new_states, axis=1).astype(x.dtype), k_state, v_state


def setup_inputs(seed: int = 0) -> dict:
    key = jax.random.key(seed)
    ks = jax.random.split(key, 25)
    D = D_MODEL
    f32 = jnp.float32

    def nrm(k, shape, scale):
        return jax.random.normal(k, shape, f32) * scale

    return {
        "x_prompt": nrm(ks[0], (BATCH, SEQ, D), 1.0),
        "x_sample": nrm(ks[1], (DEC_BATCH, DEC_SEQ, D), 1.0),
        "state_hgrn": nrm(ks[2], (DEC_BATCH, N_A_LAYERS, A_HEADS, A_DK, A_DV), 0.5),
        "cache_swa_k": nrm(ks[3], (DEC_BATCH, WINDOW, B_KV_HEADS, B_HEAD_DIM), 1.0),
        "cache_swa_v": nrm(ks[4], (DEC_BATCH, WINDOW, B_KV_HEADS, B_HEAD_DIM), 1.0),
        "c_prompt": nrm(ks[5], (BATCH, D), 1.0),
        "c_sample": nrm(ks[6], (DEC_BATCH, D), 1.0),
        "w_in_a": nrm(ks[7], (N_A_LAYERS, D, 4 * D), D ** -0.5),
        "w_o_a": nrm(ks[8], (N_A_LAYERS, D, D), D ** -0.5),
        "gnorm_a": 1.0 + nrm(ks[9], (N_A_LAYERS, A_DV), 0.02),
        "lb_a": nrm(ks[10], (N_A_LAYERS, D), 0.5),
        "w_kv": nrm(ks[11], (D, 2 * B_KV_HEADS * B_HEAD_DIM), D ** -0.5),
        "w_ada_kv": nrm(ks[12], (D, 2 * D), 0.5 * D ** -0.5),
        "b_ada_kv": nrm(ks[13], (2 * D,), 0.01),
        "kv_norm_w": 1.0 + nrm(ks[14], (D,), 0.02),
        "w_q_b": nrm(ks[15], (N_B_LAYERS, D, B_HEADS * B_HEAD_DIM), D ** -0.5),
        "w_o_b": nrm(ks[16], (N_B_LAYERS, B_HEADS * B_HEAD_DIM, D), (B_HEADS * B_HEAD_DIM) ** -0.5),
        "sinks_b": nrm(ks[17], (N_B_LAYERS, B_HEADS), 0.5),
        "rel_bias": nrm(ks[18], (N_BUCKETS, B_HEADS), 0.5),
        "norm_w": 1.0 + nrm(ks[19], (DEPTH, 2, D), 0.02),
        "w_ada": nrm(ks[20], (DEPTH, D, 6 * D), 0.5 * D ** -0.5),
        "b_ada": nrm(ks[21], (DEPTH, 6 * D), 0.01),
        "w_ffn_in": nrm(ks[22], (DEPTH, D, 2 * D_FF), D ** -0.5),
        "w_ffn_out": nrm(ks[23], (DEPTH, D_FF, D), D_FF ** -0.5),
        "final_norm_w": 1.0 + nrm(ks[24], (D,), 0.02),
    }


def reference(x_prompt, x_sample, state_hgrn, cache_swa_k, cache_swa_v, c_prompt, c_sample,
              w_in_a, w_o_a, gnorm_a, lb_a, w_kv, w_ada_kv, b_ada_kv, kv_norm_w, w_q_b, w_o_b,
              sinks_b, rel_bias, norm_w, w_ada, b_ada, w_ffn_in, w_ffn_out, final_norm_w):
    weights = (w_in_a, w_o_a, gnorm_a, lb_a, w_kv, w_ada_kv, b_ada_kv, kv_norm_w, w_q_b, w_o_b,
               sinks_b, rel_bias, norm_w, w_ada, b_ada, w_ffn_in, w_ffn_out, final_norm_w)
    y_prompt, state_hgrn_prompt, cache_swa_k_prompt, cache_swa_v_prompt = trunk(
        x_prompt, c_prompt, None, None, None, *weights)
    y_sample, state_hgrn_sample, cache_swa_k_sample, cache_swa_v_sample = trunk(
        x_sample, c_sample, state_hgrn, cache_swa_k, cache_swa_v, *weights)
    return (y_prompt, y_sample, state_hgrn_prompt, state_hgrn_sample,
            cache_swa_k_prompt, cache_swa_v_prompt, cache_swa_k_sample, cache_swa_v_sample)
```

```python
import functools
import math

import numpy as np
import jax
import jax.numpy as jnp
from jax import lax
from jax.experimental import pallas as pl
from jax.experimental.pallas import tpu as pltpu

F32 = jnp.float32
BF16 = jnp.bfloat16

D_MODEL = 1024
DEPTH = 4
N_A_LAYERS = 2
A_HEADS = 8
A_DK = 128
A_DV = 128
F_MIN = 1e-30
B_HEAD_DIM = 64
B_HEADS = 16
B_KV_HEADS = 4
B_GROUPS = 4
WINDOW = 128
MASK_VALUE = -1e30
N_BUCKETS = 32
MAX_DISTANCE = 128
D_FF = 2816
EPS = 1e-6

CHUNK = 128
N_LEVELS = 7
VMEM_LIMIT_BYTES = 48 * 1024 * 1024


def _params(sem):
    return pltpu.CompilerParams(dimension_semantics=sem, vmem_limit_bytes=VMEM_LIMIT_BYTES)


def _sigmoid(x):
    return 1.0 / (1.0 + jnp.exp(-x))


def _silu(x):
    return x * _sigmoid(x)


NORM_ROWS = 128


def _mm_body(*refs, tm, normmod, per_token, n_w, epilogue, out_scale):
    it = iter(refs)
    x_ref = next(it)
    a_ref = s_ref = h_ref = g_ref = xn_ref = None
    if normmod:
        a_ref, s_ref = next(it), next(it)
    w_refs = [next(it) for _ in range(n_w)]
    if epilogue == "residual":
        h_ref, g_ref = next(it), next(it)
    o_ref = next(it)
    if normmod:
        xn_ref = next(it)

        @pl.when(pl.program_id(1) == 0)
        def _():
            def step(r, carry):
                rows = pl.ds(pl.multiple_of(r * NORM_ROWS, NORM_ROWS), NORM_ROWS)
                x = x_ref[rows, :]
                y = x * lax.rsqrt(jnp.mean(x * x, axis=-1, keepdims=True) + EPS)
                if per_token:
                    y = y * a_ref[rows, :] + s_ref[rows, :]
                else:
                    y = y * a_ref[0] + s_ref[0]
                xn_ref[rows, :] = y.astype(BF16)
                return carry

            lax.fori_loop(0, tm // NORM_ROWS, step, 0)

        lhs = xn_ref[...]
    else:
        lhs = x_ref[...].astype(BF16)

    accs = [jnp.dot(lhs, w[...].astype(BF16), preferred_element_type=F32) for w in w_refs]
    if epilogue == "plain":
        out = accs[0]
        if out_scale is not None:
            out = out * out_scale
    elif epilogue == "swiglu":
        out = _silu(accs[0]) * accs[1]
    elif epilogue == "residual":
        gate = g_ref[...] if per_token else g_ref[0]
        out = h_ref[...] + gate * accs[0]
    else:
        raise ValueError(epilogue)
    o_ref[...] = out.astype(o_ref.dtype)


def fused_mm(x, w, layer, *, n_out, tn, tm, col_blocks=(0,), mod=None, res=None,
             per_token=False, rows_per_batch=None, epilogue="plain", out_dtype=F32,
             out_scale=None):
    m, k = x.shape
    normmod = mod is not None
    grid = (m // tm, n_out // tn)
    assert m % tm == 0 and n_out % tn == 0

    def batch_of(i):
        return (i * tm) // rows_per_batch

    in_specs = [pl.BlockSpec((tm, k), lambda i, j: (i, 0))]
    args = [x]
    if normmod:
        for arr in mod:
            if per_token:
                in_specs.append(pl.BlockSpec((tm, k), lambda i, j: (i, 0)))
            else:
                in_specs.append(pl.BlockSpec((1, 1, k), lambda i, j: (batch_of(i), 0, 0)))
            args.append(arr)
    for cb in col_blocks:
        in_specs.append(pl.BlockSpec((None, k, tn), lambda i, j, cb=cb: (layer, 0, cb + j)))
        args.append(w)
    if epilogue == "residual":
        h, gate = res
        in_specs.append(pl.BlockSpec((tm, tn), lambda i, j: (i, j)))
        if per_token:
            in_specs.append(pl.BlockSpec((tm, tn), lambda i, j: (i, j)))
        else:
            in_specs.append(pl.BlockSpec((1, 1, tn), lambda i, j: (batch_of(i), 0, j)))
        args += [h, gate]
    body = functools.partial(_mm_body, tm=tm, normmod=normmod, per_token=per_token,
                             n_w=len(col_blocks), epilogue=epilogue, out_scale=out_scale)
    return pl.pallas_call(
        body,
        grid=grid,
        in_specs=in_specs,
        out_specs=pl.BlockSpec((tm, tn), lambda i, j: (i, j)),
        out_shape=jax.ShapeDtypeStruct((m, n_out), out_dtype),
        scratch_shapes=[pltpu.VMEM((tm, k), BF16)] if normmod else [],
        compiler_params=_params(("parallel", "arbitrary")),
    )(*args)


def _ada_body(c_ref, w_ref, b_ref, p_ref, q_ref, o_ref):
    c = _silu(c_ref[...]).astype(BF16)
    acc = jnp.dot(c, w_ref[...].astype(BF16), preferred_element_type=F32)
    o_ref[...] = p_ref[...] + q_ref[...] * (acc + b_ref[...])


def ada_project(c_all, w, b, p, q):
    n_l, _, n_cols = w.shape
    n_c = n_cols // D_MODEL
    rows = c_all.shape[0]
    vec_spec = pl.BlockSpec((None, None, 1, D_MODEL), lambda l, j: (l, j, 0, 0))
    return pl.pallas_call(
        _ada_body,
        grid=(n_l, n_c),
        in_specs=[
            pl.BlockSpec((rows, D_MODEL), lambda l, j: (0, 0)),
            pl.BlockSpec((None, D_MODEL, D_MODEL), lambda l, j: (l, 0, j)),
            vec_spec, vec_spec, vec_spec,
        ],
        out_specs=pl.BlockSpec((None, rows, D_MODEL), lambda l, j: (l, 0, j)),
        out_shape=jax.ShapeDtypeStruct((n_l, rows, n_cols), F32),
        compiler_params=_params(("parallel", "parallel")),
    )(c_all, w, b, p, q)


def _scan_tables():
    c = CHUNK
    t = np.arange(c)[:, None]
    u = np.arange(c)[None, :]
    blocks = [(u <= t), (u > t)]
    for level in range(1, N_LEVELS + 1):
        p = N_LEVELS - level
        odd = ((t >> p) & 1) == 1
        start = (t >> p) << p
        end = (((t >> p) + 1) << p) - 1
        blocks.append(np.where(odd, (u >= start) & (u <= t), (u > t) & (u <= end)))
    sums = np.concatenate(blocks, axis=0).astype(np.float32)
    x = np.arange(c)[:, None] ^ np.arange(c)[None, :]
    msb = np.floor(np.log2(np.maximum(x, 1))).astype(np.int32)
    lvl = np.where(x == 0, 0, N_LEVELS - msb)
    lvl = np.where(np.arange(c)[None, :] > np.arange(c)[:, None], -1, lvl).astype(np.int32)
    return sums, lvl


_SCAN_SUMS, _SCAN_LEVELS = _scan_tables()

_TN = (((1,), (1,)), ((), ()))
_TM = (((0,), (0,)), ((), ()))


def _hgrn_scan_body(q_ref, f_ref, i_ref, g_ref, lb_ref, gn_ref, sums_ref, lvl_ref,
                    o_ref, st_ref, state_ref, ex_ref):
    n = pl.program_id(1)

    @pl.when(n == 0)
    def _():
        state_ref[...] = jnp.zeros_like(state_ref)

    lb = lb_ref[...]
    f = lb + (1.0 - lb) * _sigmoid(f_ref[...])
    kk = 1.0 - f
    log_f = jnp.log(jnp.maximum(f, F_MIN))
    hi = log_f.astype(BF16)
    rem = log_f - hi.astype(F32)
    mid = rem.astype(BF16)
    lo = (rem - mid.astype(F32)).astype(BF16)
    sums = sums_ref[...]
    ex_ref[...] = (jnp.dot(sums, hi, preferred_element_type=F32)
                   + jnp.dot(sums, mid, preferred_element_type=F32)
                   + jnp.dot(sums, lo, preferred_element_type=F32))

    lvl = lvl_ref[...]
    row = lax.broadcasted_iota(jnp.int32, (CHUNK, A_DK), 0)
    gn = gn_ref[...]
    for h in range(A_HEADS):
        sl = slice(h * A_DK, (h + 1) * A_DK)
        qh = _silu(q_ref[:, sl])
        kh = kk[:, sl]
        vh = i_ref[:, sl].astype(BF16)
        b_incl = ex_ref[0:CHUNK, sl]
        q_dec = (qh * jnp.exp(b_incl)).astype(BF16)
        k_dec = (kh * jnp.exp(ex_ref[CHUNK:2 * CHUNK, sl])).astype(BF16)
        att = lax.dot_general(qh.astype(BF16), kh.astype(BF16), _TN, preferred_element_type=F32)
        att = jnp.where(lvl == 0, att, 0.0)
        for level in range(1, N_LEVELS + 1):
            p = N_LEVELS - level
            e = jnp.exp(ex_ref[(level + 1) * CHUNK:(level + 2) * CHUNK, sl])
            odd = ((row >> p) & 1) == 1
            xl = (jnp.where(odd, qh, kh) * e).astype(BF16)
            a = lax.dot_general(xl, xl, _TN, preferred_element_type=F32)
            att = jnp.where(lvl == level, a, att)
        s_t = state_ref[h]
        o = (lax.dot_general(q_dec, s_t.astype(BF16), _TN, preferred_element_type=F32)
             + jnp.dot(att.astype(BF16), vh, preferred_element_type=F32))
        d_end = jnp.exp(b_incl[CHUNK - 1:CHUNK, :])
        s_new = d_end * s_t + lax.dot_general(vh, k_dec, _TM, preferred_element_type=F32)
        state_ref[h] = s_new
        y = o * lax.rsqrt(jnp.mean(o * o, axis=-1, keepdims=True) + EPS)
        y = y * gn * _silu(g_ref[:, sl])
        o_ref[:, sl] = y.astype(o_ref.dtype)

        @pl.when(n == pl.num_programs(1) - 1)
        def _():
            st_ref[0, h] = s_new.T


def hgrn_scan_prompt(proj, lb, gn, bsz, t):
    nc = t // CHUNK
    sec = lambda k: pl.BlockSpec((CHUNK, D_MODEL), lambda b, n, k=k: (b * nc + n, k))
    n_sum = _SCAN_SUMS.shape[0]
    return pl.pallas_call(
        _hgrn_scan_body,
        grid=(bsz, nc),
        in_specs=[sec(0), sec(1), sec(2), sec(3),
                  pl.BlockSpec((1, D_MODEL), lambda b, n: (0, 0)),
                  pl.BlockSpec((1, A_DV), lambda b, n: (0, 0)),
                  pl.BlockSpec((n_sum, CHUNK), lambda b, n: (0, 0)),
                  pl.BlockSpec((CHUNK, CHUNK), lambda b, n: (0, 0))],
        out_specs=[pl.BlockSpec((CHUNK, D_MODEL), lambda b, n: (b * nc + n, 0)),
                   pl.BlockSpec((1, A_HEADS, A_DK, A_DV), lambda b, n: (b, 0, 0, 0))],
        out_shape=[jax.ShapeDtypeStruct((bsz * t, D_MODEL), BF16),
                   jax.ShapeDtypeStruct((bsz, A_HEADS, A_DK, A_DV), F32)],
        scratch_shapes=[pltpu.VMEM((A_HEADS, A_DV, A_DK), F32),
                        pltpu.VMEM((n_sum, D_MODEL), F32)],
        compiler_params=_params(("parallel", "arbitrary")),
    )(proj, proj, proj, proj, lb, gn, jnp.asarray(_SCAN_SUMS, BF16), jnp.asarray(_SCAN_LEVELS))


def _hgrn_step_body(q_ref, f_ref, i_ref, g_ref, lb_ref, gn_ref, s_ref, o_ref, so_ref):
    lb = lb_ref[...]
    f = lb + (1.0 - lb) * _sigmoid(f_ref[0])
    decay = jnp.maximum(f, F_MIN)
    kk = 1.0 - f
    q = _silu(q_ref[0])
    gate = _silu(g_ref[0])
    v = i_ref[0]
    gn = gn_ref[...]

    def column(rowvec):
        return jnp.broadcast_to(rowvec, (A_DK, A_DK)).T

    for h in range(A_HEADS):
        sl = slice(h * A_DK, (h + 1) * A_DK)
        s_new = column(decay[:, sl]) * s_ref[0, h] + column(kk[:, sl]) * v[:, sl]
        so_ref[0, h] = s_new
        o = jnp.sum(column(q[:, sl]) * s_new, axis=0, keepdims=True)
        y = o * lax.rsqrt(jnp.mean(o * o, axis=-1, keepdims=True) + EPS)
        o_ref[0, :, sl] = y * gn * gate[:, sl]


def hgrn_step(proj, lb, gn, state, layer):
    bsz = proj.shape[0]
    proj3 = proj.reshape(bsz, 1, 4 * D_MODEL)
    sec = lambda k: pl.BlockSpec((1, 1, D_MODEL), lambda b, k=k: (b, 0, k))
    o, s_new = pl.pallas_call(
        _hgrn_step_body,
        grid=(bsz,),
        in_specs=[sec(0), sec(1), sec(2), sec(3),
                  pl.BlockSpec((1, D_MODEL), lambda b: (0, 0)),
                  pl.BlockSpec((1, A_DV), lambda b: (0, 0)),
                  pl.BlockSpec((1, None, A_HEADS, A_DK, A_DV), lambda b: (b, layer, 0, 0, 0))],
        out_specs=[pl.BlockSpec((1, 1, D_MODEL), lambda b: (b, 0, 0)),
                   pl.BlockSpec((1, A_HEADS, A_DK, A_DV), lambda b: (b, 0, 0, 0))],
        out_shape=[jax.ShapeDtypeStruct((bsz, 1, D_MODEL), F32),
                   jax.ShapeDtypeStruct((bsz, A_HEADS, A_DK, A_DV), F32)],
        compiler_params=_params(("parallel",)),
    )(proj3, proj3, proj3, proj3, lb, gn, state)
    return o.reshape(bsz, D_MODEL), s_new


GROUP_LANES = B_GROUPS * B_HEAD_DIM
HEAD_DIM_SHIFT = B_HEAD_DIM.bit_length() - 1
GROUPS_SHIFT = B_GROUPS.bit_length() - 1


def _t5_buckets():
    max_exact = N_BUCKETS // 2
    d = np.arange(WINDOW)
    large = max_exact + (np.log(np.maximum(d, 1).astype(np.float32) / max_exact)
                         / math.log(MAX_DISTANCE / max_exact)
                         * (N_BUCKETS - max_exact)).astype(np.int32)
    large = np.clip(large, 0, N_BUCKETS - 1)
    return np.where(d < max_exact, d, large).astype(np.int32)


_T5_BUCKETS = _t5_buckets()


def _swa_prompt_body(sink_ref, q_ref, kp_ref, kc_ref, vp_ref, vc_ref, bias_ref, o_ref):
    n = pl.program_id(1)
    t_loc = lax.broadcasted_iota(jnp.int32, (WINDOW, 2 * WINDOW), 0)
    s_loc = lax.broadcasted_iota(jnp.int32, (WINDOW, 2 * WINDOW), 1)
    dist = t_loc + WINDOW - s_loc
    valid = (dist >= 0) & (dist < WINDOW) & ((n > 0) | (s_loc >= WINDOW))
    lane_head = lax.broadcasted_iota(jnp.int32, (WINDOW, GROUP_LANES), 1) >> HEAD_DIM_SHIFT
    for g in range(B_KV_HEADS):
        cs = slice(g * GROUP_LANES, (g + 1) * GROUP_LANES)
        qg = q_ref[:, cs].astype(F32)
        kcat = jnp.concatenate([kp_ref[:, cs], kc_ref[:, cs]], axis=0)
        vcat = jnp.concatenate([vp_ref[:, cs], vc_ref[:, cs]], axis=0)
        qstack = jnp.concatenate(
            [jnp.where(lane_head == j, qg, 0.0).astype(BF16) for j in range(B_GROUPS)], axis=0)
        logits = lax.dot_general(qstack, kcat, _TN, preferred_element_type=F32)
        acc = jnp.zeros((WINDOW, GROUP_LANES), F32)
        for j in range(B_GROUPS):
            head = g * B_GROUPS + j
            lg = logits[j * WINDOW:(j + 1) * WINDOW] + bias_ref[head]
            lg = jnp.where(valid, lg, MASK_VALUE)
            sink = sink_ref[head]
            m = jnp.maximum(jnp.max(lg, axis=-1, keepdims=True), sink)
            p = jnp.exp(lg - m)
            denom = jnp.sum(p, axis=-1, keepdims=True) + jnp.exp(sink - m)
            pv = jnp.dot(p.astype(BF16), vcat, preferred_element_type=F32)
            acc = jnp.where(lane_head == j, pv / denom, acc)
        o_ref[:, cs] = acc.astype(o_ref.dtype)


def swa_prompt(q, kv_rep, sinks, bias_tab, bsz, t):
    nb = t // WINDOW
    cur = lambda c: pl.BlockSpec((WINDOW, D_MODEL), lambda b, n, c=c: (b * nb + n, c))
    prev = lambda c: pl.BlockSpec(
        (WINDOW, D_MODEL), lambda b, n, c=c: (b * nb + jnp.maximum(n - 1, 0), c))
    return pl.pallas_call(
        _swa_prompt_body,
        grid=(bsz, nb),
        in_specs=[pl.BlockSpec(memory_space=pltpu.SMEM),
                  cur(0), prev(0), cur(0), prev(1), cur(1),
                  pl.BlockSpec((B_HEADS, WINDOW, 2 * WINDOW), lambda b, n: (0, 0, 0))],
        out_specs=pl.BlockSpec((WINDOW, D_MODEL), lambda b, n: (b * nb + n, 0)),
        out_shape=jax.ShapeDtypeStruct((bsz * t, D_MODEL), BF16),
        compiler_params=_params(("parallel", "arbitrary")),
    )(sinks, q, kv_rep, kv_rep, kv_rep, kv_rep, bias_tab)


def _swa_step_body(sink_ref, qz_ref, k_ref, v_ref, bias_ref, fold_ref, o_ref):
    qz = qz_ref[0].astype(BF16)
    logits = lax.dot_general(qz, k_ref[0].astype(BF16), _TN, preferred_element_type=F32)
    logits = logits + bias_ref[...]
    sink = sink_ref[...]
    m = jnp.maximum(jnp.max(logits, axis=-1, keepdims=True), sink)
    p = jnp.exp(logits - m)
    denom = jnp.sum(p, axis=-1, keepdims=True) + jnp.exp(sink - m)
    pv = jnp.dot(p.astype(BF16), v_ref[0].astype(BF16), preferred_element_type=F32) / denom
    head_kv = lax.broadcasted_iota(jnp.int32, pv.shape, 0) >> GROUPS_SHIFT
    lane_kv = lax.broadcasted_iota(jnp.int32, pv.shape, 1) >> HEAD_DIM_SHIFT
    own = jnp.where(head_kv == lane_kv, pv, 0.0).astype(BF16)
    o_ref[0] = jnp.dot(own, fold_ref[...], preferred_element_type=F32)


def swa_step(qz, k_cache, v_cache, sinks, bias_row):
    bsz = qz.shape[0]
    kv_w = B_KV_HEADS * B_HEAD_DIM
    fold = np.zeros((kv_w, B_HEAD_DIM), np.float32)
    fold[np.arange(kv_w), np.arange(kv_w) % B_HEAD_DIM] = 1.0
    return pl.pallas_call(
        _swa_step_body,
        grid=(bsz,),
        in_specs=[pl.BlockSpec((B_HEADS, 1), lambda b: (0, 0)),
                  pl.BlockSpec((1, B_HEADS, kv_w), lambda b: (b, 0, 0)),
                  pl.BlockSpec((1, WINDOW, kv_w), lambda b: (b, 0, 0)),
                  pl.BlockSpec((1, WINDOW, kv_w), lambda b: (b, 0, 0)),
                  pl.BlockSpec((B_HEADS, WINDOW), lambda b: (0, 0)),
                  pl.BlockSpec((kv_w, B_HEAD_DIM), lambda b: (0, 0))],
        out_specs=pl.BlockSpec((1, B_HEADS, B_HEAD_DIM), lambda b: (b, 0, 0)),
        out_shape=jax.ShapeDtypeStruct((bsz, B_HEADS, B_HEAD_DIM), F32),
        compiler_params=_params(("parallel",)),
    )(sinks.reshape(B_HEADS, 1), qz, k_cache, v_cache, bias_row, jnp.asarray(fold, BF16))


def _rmsnorm_body(x_ref, w_ref, o_ref):
    x = x_ref[...]
    o_ref[...] = x * lax.rsqrt(jnp.mean(x * x, axis=-1, keepdims=True) + EPS) * w_ref[...]


def rmsnorm(x, w, tm):
    m, d = x.shape
    return pl.pallas_call(
        _rmsnorm_body,
        grid=(m // tm,),
        in_specs=[pl.BlockSpec((tm, d), lambda i: (i, 0)), pl.BlockSpec((1, d), lambda i: (0, 0))],
        out_specs=pl.BlockSpec((tm, d), lambda i: (i, 0)),
        out_shape=jax.ShapeDtypeStruct((m, d), F32),
        compiler_params=_params(("parallel",)),
    )(x, w.reshape(1, d))


def _trunk(x, mods, kv_mod, per_token, hgrn_state0, k_buf, v_buf, wts, lbs, w_kv3, w_kv_rep3,
           bias_tab, bias_row):
    (w_in_a, w_o_a, gnorm_a, w_q_b, w_o_b, sinks_b, w_ffn_in, w_ffn_out, final_norm_w) = wts
    prompt = k_buf is None
    bsz, t, _ = x.shape
    m = bsz * t
    tm = 1024 if prompt else m
    common = dict(tm=tm, per_token=per_token, rows_per_batch=t)
    h = x.reshape(m, D_MODEL)
    states = []
    kv = k_state = v_state = None
    for l in range(DEPTH):
        a1, s1, g1, a2, s2, g2 = mods[l]
        if l < N_A_LAYERS:
            proj = fused_mm(h, w_in_a, l, n_out=4 * D_MODEL, tn=512, mod=(a1, s1), **common)
            lb = lbs[l].reshape(1, D_MODEL)
            gn = gnorm_a[l].reshape(1, A_DV)
            if prompt:
                mix, s_new = hgrn_scan_prompt(proj, lb, gn, bsz, t)
            else:
                mix, s_new = hgrn_step(proj, lb, gn, hgrn_state0, l)
            states.append(s_new)
            h = fused_mm(mix, w_o_a, l, n_out=D_MODEL, tn=512, res=(h, g1),
                         epilogue="residual", **common)
        else:
            j = l - N_A_LAYERS
            scale = 1.0 / math.sqrt(B_HEAD_DIM)
            if prompt:
                q = fused_mm(h, w_q_b, j, n_out=D_MODEL, tn=512, mod=(a1, s1), out_dtype=BF16,
                             out_scale=scale, **common)
                mix = swa_prompt(q, kv, sinks_b[j], bias_tab, bsz, t)
            else:
                q = fused_mm(h, w_q_b, j, n_out=D_MODEL, tn=512, mod=(a1, s1),
                             out_scale=scale, **common)
                lane_kv = (np.arange(B_KV_HEADS * B_HEAD_DIM) // B_HEAD_DIM)[None, :]
                head_kv = (np.arange(B_HEADS) // B_GROUPS)[:, None]
                qz = jnp.where((lane_kv == head_kv)[None],
                               jnp.tile(q.reshape(m, B_HEADS, B_HEAD_DIM), (1, 1, B_KV_HEADS)), 0.0)
                mix = swa_step(qz, k_state.reshape(m, WINDOW, -1), v_state.reshape(m, WINDOW, -1),
                               sinks_b[j], bias_row).reshape(m, D_MODEL)
            h = fused_mm(mix, w_o_b, j, n_out=D_MODEL, tn=512, res=(h, g1),
                         epilogue="residual", **common)
        act = fused_mm(h, w_ffn_in, l, n_out=D_FF, tn=256, col_blocks=(0, D_FF // 256),
                       mod=(a2, s2), epilogue="swiglu", out_dtype=BF16, **common)
        h = fused_mm(act, w_ffn_out, l, n_out=D_MODEL, tn=512, res=(h, g2),
                     epilogue="residual", **common)
        if l == N_A_LAYERS - 1:
            kv_w = B_KV_HEADS * B_HEAD_DIM
            if prompt:
                kv = fused_mm(h, w_kv_rep3, 0, n_out=2 * D_MODEL, tn=512, mod=kv_mod,
                              out_dtype=BF16, **common)
                tail = h.reshape(bsz, t, D_MODEL)[:, -WINDOW:].reshape(bsz * WINDOW, D_MODEL)
                kv_tail = fused_mm(tail, w_kv3, 0, n_out=2 * kv_w, tn=2 * kv_w, mod=kv_mod,
                                   tm=WINDOW, per_token=False, rows_per_batch=WINDOW)
                kv_tail = kv_tail.reshape(bsz, WINDOW, 2, B_KV_HEADS, B_HEAD_DIM)
                k_state, v_state = kv_tail[:, :, 0], kv_tail[:, :, 1]
            else:
                kv_new = fused_mm(h, w_kv3, 0, n_out=2 * kv_w, tn=2 * kv_w, mod=kv_mod, **common)
                kv_new = kv_new.reshape(m, 1, 2, B_KV_HEADS, B_HEAD_DIM)
                k_state = jnp.concatenate([k_buf[:, 1:], kv_new[:, :, 0]], axis=1)
                v_state = jnp.concatenate([v_buf[:, 1:], kv_new[:, :, 1]], axis=1)
    y = rmsnorm(h, final_norm_w, tm=min(tm, 512)).reshape(bsz, t, D_MODEL)
    return y, jnp.stack(states, axis=1), k_state, v_state


def kernel(x_prompt, x_sample, state_hgrn, cache_swa_k, cache_swa_v, c_prompt, c_sample,
           w_in_a, w_o_a, gnorm_a, lb_a, w_kv, w_ada_kv, b_ada_kv, kv_norm_w, w_q_b, w_o_b,
           sinks_b, rel_bias, norm_w, w_ada, b_ada, w_ffn_in, w_ffn_out, final_norm_w):
    n_p, n_s = c_prompt.shape[0], c_sample.shape[0]
    rows = -(-(n_p + n_s) // 8) * 8
    c_all = jnp.concatenate(
        [c_prompt, c_sample, jnp.zeros((rows - n_p - n_s, D_MODEL), F32)], axis=0)

    zeros, ones = jnp.zeros((DEPTH, D_MODEL), F32), jnp.ones((DEPTH, D_MODEL), F32)
    p_ada = jnp.stack([zeros, norm_w[:, 0], zeros, zeros, norm_w[:, 1], zeros], axis=1)
    q_ada = jnp.stack([ones, norm_w[:, 0], ones, ones, norm_w[:, 1], ones], axis=1)
    ada = ada_project(c_all, w_ada, b_ada.reshape(DEPTH, 6, 1, D_MODEL),
                      p_ada.reshape(DEPTH, 6, 1, D_MODEL), q_ada.reshape(DEPTH, 6, 1, D_MODEL))
    p_kv = jnp.stack([zeros[0], kv_norm_w]).reshape(1, 2, 1, D_MODEL)
    q_kv = jnp.stack([ones[0], kv_norm_w]).reshape(1, 2, 1, D_MODEL)
    ada_kv = ada_project(c_all, w_ada_kv.reshape(1, D_MODEL, 2 * D_MODEL),
                         b_ada_kv.reshape(1, 2, 1, D_MODEL), p_kv, q_kv)

    def split_mods(arr, n_chunks, lo, hi, per_token):
        out = []
        for c in range(n_chunks):
            v = arr[lo:hi, c * D_MODEL:(c + 1) * D_MODEL]
            out.append(v if per_token else v.reshape(hi - lo, 1, D_MODEL))
        return out

    def mods_for(lo, hi, per_token):
        layers = []
        for l in range(DEPTH):
            sh1, a1, g1, sh2, a2, g2 = split_mods(ada[l], 6, lo, hi, per_token)
            layers.append((a1, sh1, g1, a2, sh2, g2))
        sh_kv, a_kv = split_mods(ada_kv[0], 2, lo, hi, per_token)
        return layers, (a_kv, sh_kv)

    lb_sm = jax.nn.softmax(lb_a.astype(F32), axis=0)
    lbs = jnp.cumsum(lb_sm, axis=0) - lb_sm[0:1]

    kv_w = B_KV_HEADS * B_HEAD_DIM
    w_k = w_kv[:, :kv_w].reshape(D_MODEL, B_KV_HEADS, 1, B_HEAD_DIM)
    w_v = w_kv[:, kv_w:].reshape(D_MODEL, B_KV_HEADS, 1, B_HEAD_DIM)
    rep = lambda w: jnp.tile(w, (1, 1, B_GROUPS, 1)).reshape(D_MODEL, D_MODEL)
    w_kv_rep3 = jnp.concatenate([rep(w_k), rep(w_v)], axis=1)[None]
    w_kv3 = w_kv[None]

    rb = rel_bias.astype(F32)[_T5_BUCKETS]
    dist = np.arange(WINDOW)[:, None] + WINDOW - np.arange(2 * WINDOW)[None, :]
    bias_tab = rb[np.clip(dist, 0, WINDOW - 1)].transpose(2, 0, 1)
    bias_row = rb[WINDOW - 1 - np.arange(WINDOW)].T

    wts = (w_in_a, w_o_a, gnorm_a, w_q_b, w_o_b, sinks_b, w_ffn_in, w_ffn_out, final_norm_w)
    mods_p, kv_mod_p = mods_for(0, n_p, False)
    mods_s, kv_mod_s = mods_for(n_p, n_p + n_s, True)
    y_p, st_p, k_p, v_p = _trunk(x_prompt, mods_p, kv_mod_p, False, None, None, None, wts, lbs,
                                 w_kv3, w_kv_rep3, bias_tab, bias_row)
    y_s, st_s, k_s, v_s = _trunk(x_sample, mods_s, kv_mod_s, True, state_hgrn, cache_swa_k,
                                 cache_swa_v, wts, lbs, w_kv3, w_kv_rep3, bias_tab, bias_row)
    return (y_p, y_s, st_p, st_s, k_p, v_p, k_s, v_s)
```

```python
import functools
import math

import numpy as np
import jax
import jax.numpy as jnp
from jax import lax
from jax.experimental import pallas as pl
from jax.experimental.pallas import tpu as pltpu

F32 = jnp.float32
BF16 = jnp.bfloat16

D_MODEL = 1024
DEPTH = 4
N_A_LAYERS = 2
A_HEADS = 8
A_DK = 128
A_DV = 128
F_MIN = 1e-30
B_HEAD_DIM = 64
B_HEADS = 16
B_KV_HEADS = 4
B_GROUPS = 4
WINDOW = 128
MASK_VALUE = -1e30
N_BUCKETS = 32
MAX_DISTANCE = 128
D_FF = 2816
EPS = 1e-6

CHUNK = 128
N_LEVELS = 7
VMEM_LIMIT_BYTES = 48 * 1024 * 1024


def _params(sem):
    return pltpu.CompilerParams(dimension_semantics=sem, vmem_limit_bytes=VMEM_LIMIT_BYTES)


def _sigmoid(x):
    return 1.0 / (1.0 + jnp.exp(-x))


def _silu(x):
    return x * _sigmoid(x)


NORM_ROWS = 128


def _mm_body(*refs, tm, normmod, per_token, n_w, epilogue, out_scale):
    it = iter(refs)
    x_ref = next(it)
    a_ref = s_ref = h_ref = g_ref = xn_ref = None
    if normmod:
        a_ref, s_ref = next(it), next(it)
    w_refs = [next(it) for _ in range(n_w)]
    if epilogue == "residual":
        h_ref, g_ref = next(it), next(it)
    o_ref = next(it)
    if normmod:
        xn_ref = next(it)

        @pl.when(pl.program_id(1) == 0)
        def _():
            def step(r, carry):
                rows = pl.ds(pl.multiple_of(r * NORM_ROWS, NORM_ROWS), NORM_ROWS)
                x = x_ref[rows, :]
                y = x * lax.rsqrt(jnp.mean(x * x, axis=-1, keepdims=True) + EPS)
                if per_token:
                    y = y * a_ref[rows, :] + s_ref[rows, :]
                else:
                    y = y * a_ref[0] + s_ref[0]
                xn_ref[rows, :] = y.astype(BF16)
                return carry

            lax.fori_loop(0, tm // NORM_ROWS, step, 0)

        lhs = xn_ref[...]
    else:
        lhs = x_ref[...].astype(BF16)

    accs = [jnp.dot(lhs, w[...].astype(BF16), preferred_element_type=F32) for w in w_refs]
    if epilogue == "plain":
        out = accs[0]
        if out_scale is not None:
            out = out * out_scale
    elif epilogue == "swiglu":
        out = _silu(accs[0]) * accs[1]
    elif epilogue == "residual":
        gate = g_ref[...] if per_token else g_ref[0]
        out = h_ref[...] + gate * accs[0]
    else:
        raise ValueError(epilogue)
    o_ref[...] = out.astype(o_ref.dtype)


def fused_mm(x, w, layer, *, n_out, tn, tm, col_blocks=(0,), mod=None, res=None,
             per_token=False, rows_per_batch=None, epilogue="plain", out_dtype=F32,
             out_scale=None):
    m, k = x.shape
    normmod = mod is not None
    grid = (m // tm, n_out // tn)
    assert m % tm == 0 and n_out % tn == 0

    def batch_of(i):
        return (i * tm) // rows_per_batch

    in_specs = [pl.BlockSpec((tm, k), lambda i, j: (i, 0))]
    args = [x]
    if normmod:
        for arr in mod:
            if per_token:
                in_specs.append(pl.BlockSpec((tm, k), lambda i, j: (i, 0)))
            else:
                in_specs.append(pl.BlockSpec((1, 1, k), lambda i, j: (batch_of(i), 0, 0)))
            args.append(arr)
    for cb in col_blocks:
        in_specs.append(pl.BlockSpec((None, k, tn), lambda i, j, cb=cb: (layer, 0, cb + j)))
        args.append(w)
    if epilogue == "residual":
        h, gate = res
        in_specs.append(pl.BlockSpec((tm, tn), lambda i, j: (i, j)))
        if per_token:
            in_specs.append(pl.BlockSpec((tm, tn), lambda i, j: (i, j)))
        else:
            in_specs.append(pl.BlockSpec((1, 1, tn), lambda i, j: (batch_of(i), 0, j)))
        args += [h, gate]
    body = functools.partial(_mm_body, tm=tm, normmod=normmod, per_token=per_token,
                             n_w=len(col_blocks), epilogue=epilogue, out_scale=out_scale)
    return pl.pallas_call(
        body,
        grid=grid,
        in_specs=in_specs,
        out_specs=pl.BlockSpec((tm, tn), lambda i, j: (i, j)),
        out_shape=jax.ShapeDtypeStruct((m, n_out), out_dtype),
        scratch_shapes=[pltpu.VMEM((tm, k), BF16)] if normmod else [],
        compiler_params=_params(("parallel", "arbitrary")),
        name=f"mm_{'norm_' if normmod else ''}{epilogue}_m{m}_k{k}_n{n_out}",
    )(*args)


def _ada_body(c_ref, w_ref, b_ref, p_ref, q_ref, o_ref):
    c = _silu(c_ref[...]).astype(BF16)
    acc = jnp.dot(c, w_ref[...].astype(BF16), preferred_element_type=F32)
    o_ref[...] = p_ref[...] + q_ref[...] * (acc + b_ref[...])


def ada_project(c_all, w, b, p, q):
    n_l, _, n_cols = w.shape
    n_c = n_cols // D_MODEL
    rows = c_all.shape[0]
    vec_spec = pl.BlockSpec((None, None, 1, D_MODEL), lambda l, j: (l, j, 0, 0))
    return pl.pallas_call(
        _ada_body,
        grid=(n_l, n_c),
        in_specs=[
            pl.BlockSpec((rows, D_MODEL), lambda l, j: (0, 0)),
            pl.BlockSpec((None, D_MODEL, D_MODEL), lambda l, j: (l, 0, j)),
            vec_spec, vec_spec, vec_spec,
        ],
        out_specs=pl.BlockSpec((None, rows, D_MODEL), lambda l, j: (l, 0, j)),
        out_shape=jax.ShapeDtypeStruct((n_l, rows, n_cols), F32),
        compiler_params=_params(("parallel", "parallel")),
        name="ada_project",
    )(c_all, w, b, p, q)


def _scan_tables():
    c = CHUNK
    t = np.arange(c)[:, None]
    u = np.arange(c)[None, :]
    blocks = [(u <= t), (u > t)]
    for level in range(1, N_LEVELS + 1):
        p = N_LEVELS - level
        odd = ((t >> p) & 1) == 1
        start = (t >> p) << p
        end = (((t >> p) + 1) << p) - 1
        blocks.append(np.where(odd, (u >= start) & (u <= t), (u > t) & (u <= end)))
    sums = np.concatenate(blocks, axis=0).astype(np.float32)
    x = np.arange(c)[:, None] ^ np.arange(c)[None, :]
    msb = np.floor(np.log2(np.maximum(x, 1))).astype(np.int32)
    lvl = np.where(x == 0, 0, N_LEVELS - msb)
    lvl = np.where(np.arange(c)[None, :] > np.arange(c)[:, None], -1, lvl).astype(np.int32)
    return sums, lvl


_SCAN_SUMS, _SCAN_LEVELS = _scan_tables()

_TN = (((1,), (1,)), ((), ()))
_TM = (((0,), (0,)), ((), ()))


def _level_mix(q, kk, row, level):
    m = CHUNK >> level
    if m >= 8:
        parts = [(q if (i & 1) else kk)[i * m:(i + 1) * m] for i in range(CHUNK // m)]
        return jnp.concatenate(parts, axis=0)
    return jnp.where(((row >> (N_LEVELS - level)) & 1) == 1, q, kk)


def _hgrn_scan_body(q_ref, f_ref, i_ref, g_ref, lb_ref, gn_ref, sums_ref, lvl_ref,
                    o_ref, st_ref, state_ref, ex_ref):
    n = pl.program_id(1)

    @pl.when(n == 0)
    def _():
        state_ref[...] = jnp.zeros_like(state_ref)

    lb = lb_ref[...]
    f = lb + (1.0 - lb) * _sigmoid(f_ref[...])
    kk = 1.0 - f
    log2_f = jnp.log2(jnp.maximum(f, F_MIN))
    hi = log2_f.astype(BF16)
    rem = log2_f - hi.astype(F32)
    mid = rem.astype(BF16)
    lo = (rem - mid.astype(F32)).astype(BF16)
    ex_ref[...] = jnp.dot(sums_ref[...], jnp.concatenate([hi, mid, lo], axis=0),
                          preferred_element_type=F32)

    q = _silu(q_ref[...])
    v = i_ref[...].astype(BF16)
    b_incl = ex_ref[0:CHUNK, :]
    q_dec = (q * jnp.exp2(b_incl)).astype(BF16)
    k_dec = (kk * jnp.exp2(ex_ref[CHUNK:2 * CHUNK, :])).astype(BF16)
    d_end = jnp.exp2(b_incl[CHUNK - 1:CHUNK, :])
    heads = [slice(h * A_DK, (h + 1) * A_DK) for h in range(A_HEADS)]

    lvl = lvl_ref[...]
    row = lax.broadcasted_iota(jnp.int32, (CHUNK, D_MODEL), 0)
    q_bf, k_bf = q.astype(BF16), kk.astype(BF16)
    atts = [jnp.where(lvl == 0, lax.dot_general(q_bf[:, sl], k_bf[:, sl], _TN,
                                                preferred_element_type=F32), 0.0)
            for sl in heads]
    for level in range(1, N_LEVELS + 1):
        e = jnp.exp2(ex_ref[(level + 1) * CHUNK:(level + 2) * CHUNK, :])
        mix = (_level_mix(q, kk, row, level) * e).astype(BF16)
        for h, sl in enumerate(heads):
            a = lax.dot_general(mix[:, sl], mix[:, sl], _TN, preferred_element_type=F32)
            atts[h] = jnp.where(lvl == level, a, atts[h])

    gn = gn_ref[...]
    gate = _silu(g_ref[...])
    for h, sl in enumerate(heads):
        s_t = state_ref[h]
        o = (lax.dot_general(q_dec[:, sl], s_t.astype(BF16), _TN, preferred_element_type=F32)
             + jnp.dot(atts[h].astype(BF16), v[:, sl], preferred_element_type=F32))
        state_ref[h] = d_end[:, sl] * s_t + lax.dot_general(
            v[:, sl], k_dec[:, sl], _TM, preferred_element_type=F32)
        y = o * lax.rsqrt(jnp.mean(o * o, axis=-1, keepdims=True) + EPS)
        o_ref[:, sl] = (y * gn * gate[:, sl]).astype(o_ref.dtype)

    @pl.when(n == pl.num_programs(1) - 1)
    def _():
        for h in range(A_HEADS):
            st_ref[0, h] = state_ref[h].T


def hgrn_scan_prompt(proj, lb, gn, bsz, t):
    nc = t // CHUNK
    sec = lambda k: pl.BlockSpec((CHUNK, D_MODEL), lambda b, n, k=k: (b * nc + n, k))
    n_sum = _SCAN_SUMS.shape[0]
    return pl.pallas_call(
        _hgrn_scan_body,
        grid=(bsz, nc),
        in_specs=[sec(0), sec(1), sec(2), sec(3),
                  pl.BlockSpec((1, D_MODEL), lambda b, n: (0, 0)),
                  pl.BlockSpec((1, A_DV), lambda b, n: (0, 0)),
                  pl.BlockSpec((n_sum, 3 * CHUNK), lambda b, n: (0, 0)),
                  pl.BlockSpec((CHUNK, CHUNK), lambda b, n: (0, 0))],
        out_specs=[pl.BlockSpec((CHUNK, D_MODEL), lambda b, n: (b * nc + n, 0)),
                   pl.BlockSpec((1, A_HEADS, A_DK, A_DV), lambda b, n: (b, 0, 0, 0))],
        out_shape=[jax.ShapeDtypeStruct((bsz * t, D_MODEL), BF16),
                   jax.ShapeDtypeStruct((bsz, A_HEADS, A_DK, A_DV), F32)],
        scratch_shapes=[pltpu.VMEM((A_HEADS, A_DV, A_DK), F32),
                        pltpu.VMEM((n_sum, D_MODEL), F32)],
        compiler_params=_params(("parallel", "arbitrary")),
        name="hgrn_scan",
    )(proj, proj, proj, proj, lb, gn, jnp.asarray(np.tile(_SCAN_SUMS, (1, 3)), BF16),
      jnp.asarray(_SCAN_LEVELS))


def _hgrn_step_body(q_ref, f_ref, i_ref, g_ref, lb_ref, gn_ref, s_ref, o_ref, so_ref):
    lb = lb_ref[...]
    f = lb + (1.0 - lb) * _sigmoid(f_ref[0])
    decay = jnp.maximum(f, F_MIN)
    kk = 1.0 - f
    q = _silu(q_ref[0])
    gate = _silu(g_ref[0])
    v = i_ref[0]
    gn = gn_ref[...]

    def column(rowvec):
        return jnp.broadcast_to(rowvec, (A_DK, A_DK)).T

    for h in range(A_HEADS):
        sl = slice(h * A_DK, (h + 1) * A_DK)
        s_new = column(decay[:, sl]) * s_ref[0, h] + column(kk[:, sl]) * v[:, sl]
        so_ref[0, h] = s_new
        o = jnp.sum(column(q[:, sl]) * s_new, axis=0, keepdims=True)
        y = o * lax.rsqrt(jnp.mean(o * o, axis=-1, keepdims=True) + EPS)
        o_ref[0, :, sl] = y * gn * gate[:, sl]


def hgrn_step(proj, lb, gn, state, layer):
    bsz = proj.shape[0]
    proj3 = proj.reshape(bsz, 1, 4 * D_MODEL)
    sec = lambda k: pl.BlockSpec((1, 1, D_MODEL), lambda b, k=k: (b, 0, k))
    o, s_new = pl.pallas_call(
        _hgrn_step_body,
        grid=(bsz,),
        in_specs=[sec(0), sec(1), sec(2), sec(3),
                  pl.BlockSpec((1, D_MODEL), lambda b: (0, 0)),
                  pl.BlockSpec((1, A_DV), lambda b: (0, 0)),
                  pl.BlockSpec((1, None, A_HEADS, A_DK, A_DV), lambda b: (b, layer, 0, 0, 0))],
        out_specs=[pl.BlockSpec((1, 1, D_MODEL), lambda b: (b, 0, 0)),
                   pl.BlockSpec((1, A_HEADS, A_DK, A_DV), lambda b: (b, 0, 0, 0))],
        out_shape=[jax.ShapeDtypeStruct((bsz, 1, D_MODEL), F32),
                   jax.ShapeDtypeStruct((bsz, A_HEADS, A_DK, A_DV), F32)],
        compiler_params=_params(("parallel",)),
        name="hgrn_step",
    )(proj3, proj3, proj3, proj3, lb, gn, state)
    return o.reshape(bsz, D_MODEL), s_new


GROUP_LANES = B_GROUPS * B_HEAD_DIM
HEAD_DIM_SHIFT = B_HEAD_DIM.bit_length() - 1
GROUPS_SHIFT = B_GROUPS.bit_length() - 1


def _t5_buckets():
    max_exact = N_BUCKETS // 2
    d = np.arange(WINDOW)
    large = max_exact + (np.log(np.maximum(d, 1).astype(np.float32) / max_exact)
                         / math.log(MAX_DISTANCE / max_exact)
                         * (N_BUCKETS - max_exact)).astype(np.int32)
    large = np.clip(large, 0, N_BUCKETS - 1)
    return np.where(d < max_exact, d, large).astype(np.int32)


_T5_BUCKETS = _t5_buckets()


def _swa_prompt_body(sink_ref, q_ref, kp_ref, kc_ref, vp_ref, vc_ref, bias_ref, o_ref):
    n = pl.program_id(1)
    t_loc = lax.broadcasted_iota(jnp.int32, (WINDOW, 2 * WINDOW), 0)
    s_loc = lax.broadcasted_iota(jnp.int32, (WINDOW, 2 * WINDOW), 1)
    dist = t_loc + WINDOW - s_loc
    valid = (dist >= 0) & (dist < WINDOW) & ((n > 0) | (s_loc >= WINDOW))
    lane_head = lax.broadcasted_iota(jnp.int32, (WINDOW, GROUP_LANES), 1) >> HEAD_DIM_SHIFT
    for g in range(B_KV_HEADS):
        cs = slice(g * GROUP_LANES, (g + 1) * GROUP_LANES)
        qg = q_ref[:, cs].astype(F32)
        kcat = jnp.concatenate([kp_ref[:, cs], kc_ref[:, cs]], axis=0)
        vcat = jnp.concatenate([vp_ref[:, cs], vc_ref[:, cs]], axis=0)
        qstack = jnp.concatenate(
            [jnp.where(lane_head == j, qg, 0.0).astype(BF16) for j in range(B_GROUPS)], axis=0)
        logits = lax.dot_general(qstack, kcat, _TN, preferred_element_type=F32)
        acc = jnp.zeros((WINDOW, GROUP_LANES), F32)
        for j in range(B_GROUPS):
            head = g * B_GROUPS + j
            lg = logits[j * WINDOW:(j + 1) * WINDOW] + bias_ref[head]
            lg = jnp.where(valid, lg, MASK_VALUE)
            sink = sink_ref[head]
            m = jnp.maximum(jnp.max(lg, axis=-1, keepdims=True), sink)
            p = jnp.exp(lg - m)
            denom = jnp.sum(p, axis=-1, keepdims=True) + jnp.exp(sink - m)
            pv = jnp.dot(p.astype(BF16), vcat, preferred_element_type=F32)
            acc = jnp.where(lane_head == j, pv / denom, acc)
        o_ref[:, cs] = acc.astype(o_ref.dtype)


def swa_prompt(q, kv_rep, sinks, bias_tab, bsz, t):
    nb = t // WINDOW
    cur = lambda c: pl.BlockSpec((WINDOW, D_MODEL), lambda b, n, c=c: (b * nb + n, c))
    prev = lambda c: pl.BlockSpec(
        (WINDOW, D_MODEL), lambda b, n, c=c: (b * nb + jnp.maximum(n - 1, 0), c))
    return pl.pallas_call(
        _swa_prompt_body,
        grid=(bsz, nb),
        in_specs=[pl.BlockSpec(memory_space=pltpu.SMEM),
                  cur(0), prev(0), cur(0), prev(1), cur(1),
                  pl.BlockSpec((B_HEADS, WINDOW, 2 * WINDOW), lambda b, n: (0, 0, 0))],
        out_specs=pl.BlockSpec((WINDOW, D_MODEL), lambda b, n: (b * nb + n, 0)),
        out_shape=jax.ShapeDtypeStruct((bsz * t, D_MODEL), BF16),
        compiler_params=_params(("parallel", "arbitrary")),
        name="swa_prompt",
    )(sinks, q, kv_rep, kv_rep, kv_rep, kv_rep, bias_tab)


def _swa_step_body(sink_ref, qz_ref, k_ref, v_ref, bias_ref, fold_ref, o_ref):
    qz = qz_ref[0].astype(BF16)
    logits = lax.dot_general(qz, k_ref[0].astype(BF16), _TN, preferred_element_type=F32)
    logits = logits + bias_ref[...]
    sink = sink_ref[...]
    m = jnp.maximum(jnp.max(logits, axis=-1, keepdims=True), sink)
    p = jnp.exp(logits - m)
    denom = jnp.sum(p, axis=-1, keepdims=True) + jnp.exp(sink - m)
    pv = jnp.dot(p.astype(BF16), v_ref[0].astype(BF16), preferred_element_type=F32) / denom
    head_kv = lax.broadcasted_iota(jnp.int32, pv.shape, 0) >> GROUPS_SHIFT
    lane_kv = lax.broadcasted_iota(jnp.int32, pv.shape, 1) >> HEAD_DIM_SHIFT
    own = jnp.where(head_kv == lane_kv, pv, 0.0).astype(BF16)
    o_ref[0] = jnp.dot(own, fold_ref[...], preferred_element_type=F32)


def swa_step(qz, k_cache, v_cache, sinks, bias_row):
    bsz = qz.shape[0]
    kv_w = B_KV_HEADS * B_HEAD_DIM
    fold = np.zeros((kv_w, B_HEAD_DIM), np.float32)
    fold[np.arange(kv_w), np.arange(kv_w) % B_HEAD_DIM] = 1.0
    return pl.pallas_call(
        _swa_step_body,
        grid=(bsz,),
        in_specs=[pl.BlockSpec((B_HEADS, 1), lambda b: (0, 0)),
                  pl.BlockSpec((1, B_HEADS, kv_w), lambda b: (b, 0, 0)),
                  pl.BlockSpec((1, WINDOW, kv_w), lambda b: (b, 0, 0)),
                  pl.BlockSpec((1, WINDOW, kv_w), lambda b: (b, 0, 0)),
                  pl.BlockSpec((B_HEADS, WINDOW), lambda b: (0, 0)),
                  pl.BlockSpec((kv_w, B_HEAD_DIM), lambda b: (0, 0))],
        out_specs=pl.BlockSpec((1, B_HEADS, B_HEAD_DIM), lambda b: (b, 0, 0)),
        out_shape=jax.ShapeDtypeStruct((bsz, B_HEADS, B_HEAD_DIM), F32),
        compiler_params=_params(("parallel",)),
        name="swa_step",
    )(sinks.reshape(B_HEADS, 1), qz, k_cache, v_cache, bias_row, jnp.asarray(fold, BF16))


def _rmsnorm_body(x_ref, w_ref, o_ref):
    x = x_ref[...]
    o_ref[...] = x * lax.rsqrt(jnp.mean(x * x, axis=-1, keepdims=True) + EPS) * w_ref[...]


def rmsnorm(x, w, tm):
    m, d = x.shape
    return pl.pallas_call(
        _rmsnorm_body,
        grid=(m // tm,),
        in_specs=[pl.BlockSpec((tm, d), lambda i: (i, 0)), pl.BlockSpec((1, d), lambda i: (0, 0))],
        out_specs=pl.BlockSpec((tm, d), lambda i: (i, 0)),
        out_shape=jax.ShapeDtypeStruct((m, d), F32),
        compiler_params=_params(("parallel",)),
        name="final_rmsnorm",
    )(x, w.reshape(1, d))


def _trunk(x, mods, kv_mod, per_token, hgrn_state0, k_buf, v_buf, wts, lbs, w_kv3, w_kv_rep3,
           bias_tab, bias_row):
    (w_in_a, w_o_a, gnorm_a, w_q_b, w_o_b, sinks_b, w_ffn_in, w_ffn_out, final_norm_w) = wts
    prompt = k_buf is None
    bsz, t, _ = x.shape
    m = bsz * t
    tm = 1024 if prompt else m
    common = dict(tm=tm, per_token=per_token, rows_per_batch=t)
    h = x.reshape(m, D_MODEL)
    states = []
    kv = k_state = v_state = None
    for l in range(DEPTH):
        a1, s1, g1, a2, s2, g2 = mods[l]
        if l < N_A_LAYERS:
            proj = fused_mm(h, w_in_a, l, n_out=4 * D_MODEL, tn=512, mod=(a1, s1), **common)
            lb = lbs[l].reshape(1, D_MODEL)
            gn = gnorm_a[l].reshape(1, A_DV)
            if prompt:
                mix, s_new = hgrn_scan_prompt(proj, lb, gn, bsz, t)
            else:
                mix, s_new = hgrn_step(proj, lb, gn, hgrn_state0, l)
            states.append(s_new)
            h = fused_mm(mix, w_o_a, l, n_out=D_MODEL, tn=512, res=(h, g1),
                         epilogue="residual", **common)
        else:
            j = l - N_A_LAYERS
            scale = 1.0 / math.sqrt(B_HEAD_DIM)
            if prompt:
                q = fused_mm(h, w_q_b, j, n_out=D_MODEL, tn=512, mod=(a1, s1), out_dtype=BF16,
                             out_scale=scale, **common)
                mix = swa_prompt(q, kv, sinks_b[j], bias_tab, bsz, t)
            else:
                q = fused_mm(h, w_q_b, j, n_out=D_MODEL, tn=512, mod=(a1, s1),
                             out_scale=scale, **common)
                lane_kv = (np.arange(B_KV_HEADS * B_HEAD_DIM) // B_HEAD_DIM)[None, :]
                head_kv = (np.arange(B_HEADS) // B_GROUPS)[:, None]
                qz = jnp.where((lane_kv == head_kv)[None],
                               jnp.tile(q.reshape(m, B_HEADS, B_HEAD_DIM), (1, 1, B_KV_HEADS)), 0.0)
                mix = swa_step(qz, k_state.reshape(m, WINDOW, -1), v_state.reshape(m, WINDOW, -1),
                               sinks_b[j], bias_row).reshape(m, D_MODEL)
            h = fused_mm(mix, w_o_b, j, n_out=D_MODEL, tn=512, res=(h, g1),
                         epilogue="residual", **common)
        act = fused_mm(h, w_ffn_in, l, n_out=D_FF, tn=256, col_blocks=(0, D_FF // 256),
                       mod=(a2, s2), epilogue="swiglu", out_dtype=BF16, **common)
        h = fused_mm(act, w_ffn_out, l, n_out=D_MODEL, tn=512, res=(h, g2),
                     epilogue="residual", **common)
        if l == N_A_LAYERS - 1:
            kv_w = B_KV_HEADS * B_HEAD_DIM
            if prompt:
                kv = fused_mm(h, w_kv_rep3, 0, n_out=2 * D_MODEL, tn=512, mod=kv_mod,
                              out_dtype=BF16, **common)
                tail = h.reshape(bsz, t, D_MODEL)[:, -WINDOW:].reshape(bsz * WINDOW, D_MODEL)
                kv_tail = fused_mm(tail, w_kv3, 0, n_out=2 * kv_w, tn=2 * kv_w, mod=kv_mod,
                                   tm=WINDOW, per_token=False, rows_per_batch=WINDOW)
                kv_tail = kv_tail.reshape(bsz, WINDOW, 2, B_KV_HEADS, B_HEAD_DIM)
                k_state, v_state = kv_tail[:, :, 0], kv_tail[:, :, 1]
            else:
                kv_new = fused_mm(h, w_kv3, 0, n_out=2 * kv_w, tn=2 * kv_w, mod=kv_mod, **common)
                kv_new = kv_new.reshape(m, 1, 2, B_KV_HEADS, B_HEAD_DIM)
                k_state = jnp.concatenate([k_buf[:, 1:], kv_new[:, :, 0]], axis=1)
                v_state = jnp.concatenate([v_buf[:, 1:], kv_new[:, :, 1]], axis=1)
    y = rmsnorm(h, final_norm_w, tm=min(tm, 512)).reshape(bsz, t, D_MODEL)
    return y, jnp.stack(states, axis=1), k_state, v_state


def kernel(x_prompt, x_sample, state_hgrn, cache_swa_k, cache_swa_v, c_prompt, c_sample,
           w_in_a, w_o_a, gnorm_a, lb_a, w_kv, w_ada_kv, b_ada_kv, kv_norm_w, w_q_b, w_o_b,
           sinks_b, rel_bias, norm_w, w_ada, b_ada, w_ffn_in, w_ffn_out, final_norm_w):
    n_p, n_s = c_prompt.shape[0], c_sample.shape[0]
    rows = -(-(n_p + n_s) // 8) * 8
    c_all = jnp.concatenate(
        [c_prompt, c_sample, jnp.zeros((rows - n_p - n_s, D_MODEL), F32)], axis=0)

    zeros, ones = jnp.zeros((DEPTH, D_MODEL), F32), jnp.ones((DEPTH, D_MODEL), F32)
    p_ada = jnp.stack([zeros, norm_w[:, 0], zeros, zeros, norm_w[:, 1], zeros], axis=1)
    q_ada = jnp.stack([ones, norm_w[:, 0], ones, ones, norm_w[:, 1], ones], axis=1)
    ada = ada_project(c_all, w_ada, b_ada.reshape(DEPTH, 6, 1, D_MODEL),
                      p_ada.reshape(DEPTH, 6, 1, D_MODEL), q_ada.reshape(DEPTH, 6, 1, D_MODEL))
    p_kv = jnp.stack([zeros[0], kv_norm_w]).reshape(1, 2, 1, D_MODEL)
    q_kv = jnp.stack([ones[0], kv_norm_w]).reshape(1, 2, 1, D_MODEL)
    ada_kv = ada_project(c_all, w_ada_kv.reshape(1, D_MODEL, 2 * D_MODEL),
                         b_ada_kv.reshape(1, 2, 1, D_MODEL), p_kv, q_kv)

    def split_mods(arr, n_chunks, lo, hi, per_token):
        out = []
        for c in range(n_chunks):
            v = arr[lo:hi, c * D_MODEL:(c + 1) * D_MODEL]
            out.append(v if per_token else v.reshape(hi - lo, 1, D_MODEL))
        return out

    def mods_for(lo, hi, per_token):
        layers = []
        for l in range(DEPTH):
            sh1, a1, g1, sh2, a2, g2 = split_mods(ada[l], 6, lo, hi, per_token)
            layers.append((a1, sh1, g1, a2, sh2, g2))
        sh_kv, a_kv = split_mods(ada_kv[0], 2, lo, hi, per_token)
        return layers, (a_kv, sh_kv)

    lb_sm = jax.nn.softmax(lb_a.astype(F32), axis=0)
    lbs = jnp.cumsum(lb_sm, axis=0) - lb_sm[0:1]

    kv_w = B_KV_HEADS * B_HEAD_DIM
    w_k = w_kv[:, :kv_w].reshape(D_MODEL, B_KV_HEADS, 1, B_HEAD_DIM)
    w_v = w_kv[:, kv_w:].reshape(D_MODEL, B_KV_HEADS, 1, B_HEAD_DIM)
    rep = lambda w: jnp.tile(w, (1, 1, B_GROUPS, 1)).reshape(D_MODEL, D_MODEL)
    w_kv_rep3 = jnp.concatenate([rep(w_k), rep(w_v)], axis=1)[None]
    w_kv3 = w_kv[None]

    rb = rel_bias.astype(F32)[_T5_BUCKETS]
    dist = np.arange(WINDOW)[:, None] + WINDOW - np.arange(2 * WINDOW)[None, :]
    bias_tab = rb[np.clip(dist, 0, WINDOW - 1)].transpose(2, 0, 1)
    bias_row = rb[WINDOW - 1 - np.arange(WINDOW)].T

    wts = (w_in_a, w_o_a, gnorm_a, w_q_b, w_o_b, sinks_b, w_ffn_in, w_ffn_out, final_norm_w)
    mods_p, kv_mod_p = mods_for(0, n_p, False)
    mods_s, kv_mod_s = mods_for(n_p, n_p + n_s, True)
    y_p, st_p, k_p, v_p = _trunk(x_prompt, mods_p, kv_mod_p, False, None, None, None, wts, lbs,
                                 w_kv3, w_kv_rep3, bias_tab, bias_row)
    y_s, st_s, k_s, v_s = _trunk(x_sample, mods_s, kv_mod_s, True, state_hgrn, cache_swa_k,
                                 cache_swa_v, wts, lbs, w_kv3, w_kv_rep3, bias_tab, bias_row)
    return (y_p, y_s, st_p, st_s, k_p, v_p, k_s, v_s)
```

```python
import functools
import math

import numpy as np
import jax
import jax.numpy as jnp
from jax import lax
from jax.experimental import pallas as pl
from jax.experimental.pallas import tpu as pltpu

F32 = jnp.float32
BF16 = jnp.bfloat16

D_MODEL = 1024
DEPTH = 4
N_A_LAYERS = 2
A_HEADS = 8
A_DK = 128
A_DV = 128
F_MIN = 1e-30
B_HEAD_DIM = 64
B_HEADS = 16
B_KV_HEADS = 4
B_GROUPS = 4
WINDOW = 128
MASK_VALUE = -1e30
N_BUCKETS = 32
MAX_DISTANCE = 128
D_FF = 2816
EPS = 1e-6

CHUNK = 128
N_LEVELS = 7
ROW_TILE = 512
COL_TILE = 1024
FF_TILE = 256
VMEM_LIMIT_BYTES = 48 * 1024 * 1024


def _params(sem):
    return pltpu.CompilerParams(dimension_semantics=sem, vmem_limit_bytes=VMEM_LIMIT_BYTES)


def _row_tile(m, per_token, rows_per_batch):
    return min(m, ROW_TILE) if per_token else min(m, ROW_TILE, rows_per_batch)


def _sigmoid(x):
    return 1.0 / (1.0 + jnp.exp(-x))


def _silu(x):
    return x * _sigmoid(x)


def _norm_mod(x, a, s):
    y = x * lax.rsqrt(jnp.mean(x * x, axis=-1, keepdims=True) + EPS)
    return (y * a + s).astype(BF16)


def _resident(shape, layer):
    return pl.BlockSpec((None,) + shape, lambda i: (layer,) + (0,) * len(shape),
                        pipeline_mode=pl.Buffered(1))


def _row_operand(width, tm, per_token, rows_per_batch):
    if per_token:
        return pl.BlockSpec((tm, width), lambda i: (i, 0))
    return pl.BlockSpec((1, 1, width), lambda i: ((i * tm) // rows_per_batch, 0, 0))


def _row_value(ref, per_token):
    return ref[...] if per_token else ref[0]


def _norm_proj_body(*refs, per_token, epilogue, out_scale):
    x_ref, a_ref, s_ref, w_ref = refs[:4]
    xn = _norm_mod(x_ref[...], _row_value(a_ref, per_token), _row_value(s_ref, per_token))

    tn = min(COL_TILE, w_ref.shape[1])

    def proj(c):
        return jnp.dot(xn, w_ref[:, c:c + tn], preferred_element_type=F32)

    if epilogue == "plain":
        (o_ref,) = refs[4:]
        for c in range(0, o_ref.shape[1], tn):
            acc = proj(c)
            if out_scale is not None:
                acc = acc * out_scale
            o_ref[:, c:c + tn] = acc.astype(o_ref.dtype)
    elif epilogue == "hgrn":
        lb_ref, o_ref, lf_ref = refs[4:]
        d = D_MODEL
        lb = lb_ref[...]
        o_ref[:, 0:d] = _silu(proj(0)).astype(o_ref.dtype)
        f = lb + (1.0 - lb) * _sigmoid(proj(d))
        o_ref[:, d:2 * d] = (1.0 - f).astype(o_ref.dtype)
        lf_ref[...] = jnp.log2(jnp.maximum(f, F_MIN))
        o_ref[:, 2 * d:3 * d] = proj(2 * d).astype(o_ref.dtype)
        o_ref[:, 3 * d:4 * d] = _silu(proj(3 * d)).astype(o_ref.dtype)
    else:
        raise ValueError(epilogue)


def norm_proj(x, mod, w, layer, *, per_token, rows_per_batch, epilogue="plain", lb=None,
              out_dtype=BF16, out_scale=None):
    m, k = x.shape
    n = w.shape[-1]
    tm = _row_tile(m, per_token, rows_per_batch)
    assert m % tm == 0 and (per_token or rows_per_batch % tm == 0)
    row = functools.partial(_row_operand, tm=tm, per_token=per_token, rows_per_batch=rows_per_batch)
    in_specs = [pl.BlockSpec((tm, k), lambda i: (i, 0)), row(k), row(k), _resident((k, n), layer)]
    args = [x, mod[0], mod[1], w]
    out_specs = [pl.BlockSpec((tm, n), lambda i: (i, 0))]
    out_shape = [jax.ShapeDtypeStruct((m, n), out_dtype)]
    if epilogue == "hgrn":
        in_specs.append(pl.BlockSpec((1, D_MODEL), lambda i: (0, 0)))
        args.append(lb)
        out_specs.append(pl.BlockSpec((tm, D_MODEL), lambda i: (i, 0)))
        out_shape.append(jax.ShapeDtypeStruct((m, D_MODEL), F32))
    outs = pl.pallas_call(
        functools.partial(_norm_proj_body, per_token=per_token, epilogue=epilogue,
                          out_scale=out_scale),
        grid=(m // tm,),
        in_specs=in_specs,
        out_specs=out_specs,
        out_shape=out_shape,
        compiler_params=_params(("parallel",)),
        name=f"norm_proj_{epilogue}_m{m}_n{n}",
    )(*args)
    return outs if epilogue == "hgrn" else outs[0]


def _post_ffn_body(*refs, per_token, final_norm):
    (mix_ref, h_ref, g1_ref, a2_ref, s2_ref, g2_ref, wo_ref, win_ref, wout_ref) = refs[:9]
    rest = refs[9:]
    fw_ref = rest[0] if final_norm else None
    o_ref, act_ref = rest[-2:]
    row = functools.partial(_row_value, per_token=per_token)
    h_mid = h_ref[...] + row(g1_ref) * jnp.dot(mix_ref[...].astype(BF16), wo_ref[...],
                                               preferred_element_type=F32)
    xn = _norm_mod(h_mid, row(a2_ref), row(s2_ref))
    for c in range(0, D_FF, FF_TILE):
        gate = jnp.dot(xn, win_ref[:, c:c + FF_TILE], preferred_element_type=F32)
        up = jnp.dot(xn, win_ref[:, D_FF + c:D_FF + c + FF_TILE], preferred_element_type=F32)
        act_ref[:, c:c + FF_TILE] = (_silu(gate) * up).astype(BF16)
    out = h_mid + row(g2_ref) * jnp.dot(act_ref[...], wout_ref[...], preferred_element_type=F32)
    if final_norm:
        out = out * lax.rsqrt(jnp.mean(out * out, axis=-1, keepdims=True) + EPS) * fw_ref[...]
    o_ref[...] = out


def post_ffn(mix, h, g1, mod2, g2, w_o, lo, w_ffn_in, w_ffn_out, lf, *, per_token,
             rows_per_batch, final_w=None):
    m, d = h.shape
    tm = _row_tile(m, per_token, rows_per_batch)
    assert m % tm == 0 and (per_token or rows_per_batch % tm == 0)
    row = functools.partial(_row_operand, d, tm, per_token, rows_per_batch)
    tile = pl.BlockSpec((tm, d), lambda i: (i, 0))
    in_specs = [tile, tile, row(), row(), row(), row(),
                _resident((d, d), lo), _resident((d, 2 * D_FF), lf), _resident((D_FF, d), lf)]
    args = [mix, h, g1, mod2[0], mod2[1], g2, w_o, w_ffn_in, w_ffn_out]
    if final_w is not None:
        in_specs.append(pl.BlockSpec((1, d), lambda i: (0, 0)))
        args.append(final_w.reshape(1, d))
    return pl.pallas_call(
        functools.partial(_post_ffn_body, per_token=per_token, final_norm=final_w is not None),
        grid=(m // tm,),
        in_specs=in_specs,
        out_specs=tile,
        out_shape=jax.ShapeDtypeStruct((m, d), F32),
        scratch_shapes=[pltpu.VMEM((tm, D_FF), BF16)],
        compiler_params=_params(("parallel",)),
        name=f"post_ffn_m{m}",
    )(*args)


def _ada_body(c_ref, w_ref, b_ref, p_ref, q_ref, o_ref):
    c = _silu(c_ref[...]).astype(BF16)
    acc = jnp.dot(c, w_ref[...].astype(BF16), preferred_element_type=F32)
    o_ref[...] = p_ref[...] + q_ref[...] * (acc + b_ref[...])


def ada_project(c_all, w, b, p, q):
    n_l, _, n_cols = w.shape
    n_c = n_cols // D_MODEL
    rows = c_all.shape[0]
    vec_spec = pl.BlockSpec((None, None, 1, D_MODEL), lambda l, j: (l, j, 0, 0))
    return pl.pallas_call(
        _ada_body,
        grid=(n_l, n_c),
        in_specs=[
            pl.BlockSpec((rows, D_MODEL), lambda l, j: (0, 0)),
            pl.BlockSpec((None, D_MODEL, D_MODEL), lambda l, j: (l, 0, j)),
            vec_spec, vec_spec, vec_spec,
        ],
        out_specs=pl.BlockSpec((None, rows, D_MODEL), lambda l, j: (l, 0, j)),
        out_shape=jax.ShapeDtypeStruct((n_l, rows, n_cols), F32),
        compiler_params=_params(("parallel", "parallel")),
        name="ada_project",
    )(c_all, w, b, p, q)


def _scan_tables():
    c = CHUNK
    t = np.arange(c)[:, None]
    u = np.arange(c)[None, :]
    blocks = [(u <= t), (u > t)]
    for level in range(1, N_LEVELS + 1):
        p = N_LEVELS - level
        odd = ((t >> p) & 1) == 1
        start = (t >> p) << p
        end = (((t >> p) + 1) << p) - 1
        blocks.append(np.where(odd, (u >= start) & (u <= t), (u > t) & (u <= end)))
    sums = np.concatenate(blocks, axis=0).astype(np.float32)
    x = np.arange(c)[:, None] ^ np.arange(c)[None, :]
    msb = np.floor(np.log2(np.maximum(x, 1))).astype(np.int32)
    lvl = np.where(x == 0, 0, N_LEVELS - msb)
    lvl = np.where(np.arange(c)[None, :] > np.arange(c)[:, None], -1, lvl).astype(np.int32)
    return sums, lvl


_SCAN_SUMS, _SCAN_LEVELS = _scan_tables()

_TN = (((1,), (1,)), ((), ()))
_TM = (((0,), (0,)), ((), ()))


def _level_mix(q, kk, row, level):
    m = CHUNK >> level
    if m >= 8:
        parts = [(q if (i & 1) else kk)[i * m:(i + 1) * m] for i in range(CHUNK // m)]
        return jnp.concatenate(parts, axis=0)
    return jnp.where(((row >> (N_LEVELS - level)) & 1) == 1, q, kk)


def _hgrn_scan_body(q_ref, k_ref, v_ref, g_ref, lf_ref, gn_ref, sums_ref, lvl_ref,
                    o_ref, st_ref, state_ref, ex_ref):
    n = pl.program_id(1)

    @pl.when(n == 0)
    def _():
        state_ref[...] = jnp.zeros_like(state_ref)

    log2_f = lf_ref[...]
    hi = log2_f.astype(BF16)
    rem = log2_f - hi.astype(F32)
    mid = rem.astype(BF16)
    lo = (rem - mid.astype(F32)).astype(BF16)
    ex_ref[...] = jnp.dot(sums_ref[...], jnp.concatenate([hi, mid, lo], axis=0),
                          preferred_element_type=F32)

    q_bf, k_bf, v = q_ref[...], k_ref[...], v_ref[...]
    q, kk = q_bf.astype(F32), k_bf.astype(F32)
    b_incl = ex_ref[0:CHUNK, :]
    q_dec = (q * jnp.exp2(b_incl)).astype(BF16)
    k_dec = (kk * jnp.exp2(ex_ref[CHUNK:2 * CHUNK, :])).astype(BF16)
    d_end = jnp.exp2(b_incl[CHUNK - 1:CHUNK, :])
    heads = [slice(h * A_DK, (h + 1) * A_DK) for h in range(A_HEADS)]

    lvl = lvl_ref[...]
    row = lax.broadcasted_iota(jnp.int32, (CHUNK, D_MODEL), 0)
    atts = [jnp.where(lvl == 0, lax.dot_general(q_bf[:, sl], k_bf[:, sl], _TN,
                                                preferred_element_type=F32), 0.0)
            for sl in heads]
    for level in range(1, N_LEVELS + 1):
        e = jnp.exp2(ex_ref[(level + 1) * CHUNK:(level + 2) * CHUNK, :])
        mix = (_level_mix(q, kk, row, level) * e).astype(BF16)
        for h, sl in enumerate(heads):
            a = lax.dot_general(mix[:, sl], mix[:, sl], _TN, preferred_element_type=F32)
            atts[h] = jnp.where(lvl == level, a, atts[h])

    gn = gn_ref[...]
    gate = g_ref[...].astype(F32)
    for h, sl in enumerate(heads):
        s_t = state_ref[h]
        o = (lax.dot_general(q_dec[:, sl], s_t.astype(BF16), _TN, preferred_element_type=F32)
             + jnp.dot(atts[h].astype(BF16), v[:, sl], preferred_element_type=F32))
        state_ref[h] = d_end[:, sl] * s_t + lax.dot_general(
            v[:, sl], k_dec[:, sl], _TM, preferred_element_type=F32)
        y = o * lax.rsqrt(jnp.mean(o * o, axis=-1, keepdims=True) + EPS)
        o_ref[:, sl] = (y * gn * gate[:, sl]).astype(o_ref.dtype)

    @pl.when(n == pl.num_programs(1) - 1)
    def _():
        for h in range(A_HEADS):
            st_ref[0, h] = state_ref[h].T


def hgrn_scan_prompt(qkvg, lf, gn, bsz, t):
    nc = t // CHUNK
    sec = lambda k: pl.BlockSpec((CHUNK, D_MODEL), lambda b, n, k=k: (b * nc + n, k))
    n_sum = _SCAN_SUMS.shape[0]
    return pl.pallas_call(
        _hgrn_scan_body,
        grid=(bsz, nc),
        in_specs=[sec(0), sec(1), sec(2), sec(3), sec(0),
                  pl.BlockSpec((1, A_DV), lambda b, n: (0, 0)),
                  pl.BlockSpec((n_sum, 3 * CHUNK), lambda b, n: (0, 0)),
                  pl.BlockSpec((CHUNK, CHUNK), lambda b, n: (0, 0))],
        out_specs=[pl.BlockSpec((CHUNK, D_MODEL), lambda b, n: (b * nc + n, 0)),
                   pl.BlockSpec((1, A_HEADS, A_DK, A_DV), lambda b, n: (b, 0, 0, 0))],
        out_shape=[jax.ShapeDtypeStruct((bsz * t, D_MODEL), BF16),
                   jax.ShapeDtypeStruct((bsz, A_HEADS, A_DK, A_DV), F32)],
        scratch_shapes=[pltpu.VMEM((A_HEADS, A_DV, A_DK), F32),
                        pltpu.VMEM((n_sum, D_MODEL), F32)],
        compiler_params=_params(("parallel", "arbitrary")),
        name="hgrn_scan",
    )(qkvg, qkvg, qkvg, qkvg, lf, gn, jnp.asarray(np.tile(_SCAN_SUMS, (1, 3)), BF16),
      jnp.asarray(_SCAN_LEVELS))


def _hgrn_step_body(*refs):
    q_ref, k_ref, v_ref, g_ref, lf_ref, gn_ref, s_ref = refs[:7]
    o_ref, so_ref = refs[-2:]
    decay = jnp.exp2(lf_ref[0])
    q, kk, v, gate = q_ref[0], k_ref[0], v_ref[0], g_ref[0]
    gn = gn_ref[...]

    def column(rowvec):
        return jnp.broadcast_to(rowvec, (A_DK, A_DK)).T

    for h in range(A_HEADS):
        sl = slice(h * A_DK, (h + 1) * A_DK)
        s_new = column(decay[:, sl]) * s_ref[0, h] + column(kk[:, sl]) * v[:, sl]
        so_ref[0, h] = s_new
        o = jnp.sum(column(q[:, sl]) * s_new, axis=0, keepdims=True)
        y = o * lax.rsqrt(jnp.mean(o * o, axis=-1, keepdims=True) + EPS)
        o_ref[0, :, sl] = y * gn * gate[:, sl]


def hgrn_step(qkvg, lf, gn, state, layer, new_state):
    bsz = qkvg.shape[0]
    qkvg3 = qkvg.reshape(bsz, 1, 4 * D_MODEL)
    sec = lambda k: pl.BlockSpec((1, 1, D_MODEL), lambda b, k=k: (b, 0, k))
    state_blk = pl.BlockSpec((1, None, A_HEADS, A_DK, A_DV), lambda b: (b, layer, 0, 0, 0))
    in_specs = [sec(0), sec(1), sec(2), sec(3), sec(0),
                pl.BlockSpec((1, A_DV), lambda b: (0, 0)), state_blk]
    args = [qkvg3, qkvg3, qkvg3, qkvg3, lf.reshape(bsz, 1, D_MODEL), gn, state]
    aliases = {}
    if new_state is not None:
        in_specs.append(pl.BlockSpec(memory_space=pl.ANY))
        args.append(new_state)
        aliases = {len(args) - 1: 1}
    o, s_new = pl.pallas_call(
        _hgrn_step_body,
        grid=(bsz,),
        in_specs=in_specs,
        out_specs=[pl.BlockSpec((1, 1, D_MODEL), lambda b: (b, 0, 0)), state_blk],
        out_shape=[jax.ShapeDtypeStruct((bsz, 1, D_MODEL), F32),
                   jax.ShapeDtypeStruct(state.shape, F32)],
        input_output_aliases=aliases,
        compiler_params=_params(("parallel",)),
        name="hgrn_step",
    )(*args)
    return o.reshape(bsz, D_MODEL), s_new


GROUP_LANES = B_GROUPS * B_HEAD_DIM
KV_WIDTH = B_KV_HEADS * B_HEAD_DIM
HEAD_DIM_SHIFT = B_HEAD_DIM.bit_length() - 1
GROUPS_SHIFT = B_GROUPS.bit_length() - 1


def _t5_buckets():
    max_exact = N_BUCKETS // 2
    d = np.arange(WINDOW)
    large = max_exact + (np.log(np.maximum(d, 1).astype(np.float32) / max_exact)
                         / math.log(MAX_DISTANCE / max_exact)
                         * (N_BUCKETS - max_exact)).astype(np.int32)
    large = np.clip(large, 0, N_BUCKETS - 1)
    return np.where(d < max_exact, d, large).astype(np.int32)


_T5_BUCKETS = _t5_buckets()


def _swa_prompt_body(sink_ref, q_ref, kp_ref, kc_ref, vp_ref, vc_ref, bias_ref, o_ref):
    n = pl.program_id(1)
    t_loc = lax.broadcasted_iota(jnp.int32, (WINDOW, 2 * WINDOW), 0)
    s_loc = lax.broadcasted_iota(jnp.int32, (WINDOW, 2 * WINDOW), 1)
    dist = t_loc + WINDOW - s_loc
    valid = (dist >= 0) & (dist < WINDOW) & ((n > 0) | (s_loc >= WINDOW))
    lane_head = lax.broadcasted_iota(jnp.int32, (WINDOW, GROUP_LANES), 1) >> HEAD_DIM_SHIFT
    for g in range(B_KV_HEADS):
        cs = slice(g * GROUP_LANES, (g + 1) * GROUP_LANES)
        qg = q_ref[:, cs].astype(F32)
        kcat = jnp.concatenate([kp_ref[:, cs], kc_ref[:, cs]], axis=0)
        vcat = jnp.concatenate([vp_ref[:, cs], vc_ref[:, cs]], axis=0)
        qstack = jnp.concatenate(
            [jnp.where(lane_head == j, qg, 0.0).astype(BF16) for j in range(B_GROUPS)], axis=0)
        logits = lax.dot_general(qstack, kcat, _TN, preferred_element_type=F32)
        acc = jnp.zeros((WINDOW, GROUP_LANES), F32)
        for j in range(B_GROUPS):
            head = g * B_GROUPS + j
            lg = logits[j * WINDOW:(j + 1) * WINDOW] + bias_ref[head]
            lg = jnp.where(valid, lg, MASK_VALUE)
            sink = sink_ref[head]
            m = jnp.maximum(jnp.max(lg, axis=-1, keepdims=True), sink)
            p = jnp.exp(lg - m)
            denom = jnp.sum(p, axis=-1, keepdims=True) + jnp.exp(sink - m)
            pv = jnp.dot(p.astype(BF16), vcat, preferred_element_type=F32)
            acc = jnp.where(lane_head == j, pv / denom, acc)
        o_ref[:, cs] = acc.astype(o_ref.dtype)


def swa_prompt(q, kv_rep, sinks, bias_tab, bsz, t):
    nb = t // WINDOW
    cur = lambda c: pl.BlockSpec((WINDOW, D_MODEL), lambda b, n, c=c: (b * nb + n, c))
    prev = lambda c: pl.BlockSpec(
        (WINDOW, D_MODEL), lambda b, n, c=c: (b * nb + jnp.maximum(n - 1, 0), c))
    return pl.pallas_call(
        _swa_prompt_body,
        grid=(bsz, nb),
        in_specs=[pl.BlockSpec(memory_space=pltpu.SMEM),
                  cur(0), prev(0), cur(0), prev(1), cur(1),
                  pl.BlockSpec((B_HEADS, WINDOW, 2 * WINDOW), lambda b, n: (0, 0, 0))],
        out_specs=pl.BlockSpec((WINDOW, D_MODEL), lambda b, n: (b * nb + n, 0)),
        out_shape=jax.ShapeDtypeStruct((bsz * t, D_MODEL), BF16),
        compiler_params=_params(("parallel", "arbitrary")),
        name="swa_prompt",
    )(sinks, q, kv_rep, kv_rep, kv_rep, kv_rep, bias_tab)


def _swa_step_body(*refs, shift_cache):
    sink_ref, qz_ref, k_ref, v_ref, bias_ref, fold_ref = refs[:6]
    if shift_cache:
        kn_ref, vn_ref, o_ref, ko_ref, vo_ref = refs[6:]
        last = lax.broadcasted_iota(jnp.int32, (WINDOW, KV_WIDTH), 0) == WINDOW - 1
        k = jnp.where(last, kn_ref[0], pltpu.roll(k_ref[0], WINDOW - 1, 0))
        v = jnp.where(last, vn_ref[0], pltpu.roll(v_ref[0], WINDOW - 1, 0))
        ko_ref[0] = k
        vo_ref[0] = v
    else:
        (o_ref,) = refs[6:]
        k, v = k_ref[0], v_ref[0]
    qz = qz_ref[0].astype(BF16)
    logits = lax.dot_general(qz, k.astype(BF16), _TN, preferred_element_type=F32)
    logits = logits + bias_ref[...]
    sink = sink_ref[...]
    m = jnp.maximum(jnp.max(logits, axis=-1, keepdims=True), sink)
    p = jnp.exp(logits - m)
    denom = jnp.sum(p, axis=-1, keepdims=True) + jnp.exp(sink - m)
    pv = jnp.dot(p.astype(BF16), v.astype(BF16), preferred_element_type=F32) / denom
    head_kv = lax.broadcasted_iota(jnp.int32, pv.shape, 0) >> GROUPS_SHIFT
    lane_kv = lax.broadcasted_iota(jnp.int32, pv.shape, 1) >> HEAD_DIM_SHIFT
    own = jnp.where(head_kv == lane_kv, pv, 0.0).astype(BF16)
    o_ref[0] = jnp.dot(own, fold_ref[...], preferred_element_type=F32)


def swa_step(qz, k_cache, v_cache, sinks, bias_row, kv_new=None):
    bsz = qz.shape[0]
    fold = np.zeros((KV_WIDTH, B_HEAD_DIM), np.float32)
    fold[np.arange(KV_WIDTH), np.arange(KV_WIDTH) % B_HEAD_DIM] = 1.0
    cache_blk = pl.BlockSpec((1, WINDOW, KV_WIDTH), lambda b: (b, 0, 0))
    in_specs = [pl.BlockSpec((B_HEADS, 1), lambda b: (0, 0)),
                pl.BlockSpec((1, B_HEADS, KV_WIDTH), lambda b: (b, 0, 0)),
                cache_blk, cache_blk,
                pl.BlockSpec((B_HEADS, WINDOW), lambda b: (0, 0)),
                pl.BlockSpec((KV_WIDTH, B_HEAD_DIM), lambda b: (0, 0))]
    args = [sinks.reshape(B_HEADS, 1), qz, k_cache, v_cache, bias_row, jnp.asarray(fold, BF16)]
    out_specs = [pl.BlockSpec((1, B_HEADS, B_HEAD_DIM), lambda b: (b, 0, 0))]
    out_shape = [jax.ShapeDtypeStruct((bsz, B_HEADS, B_HEAD_DIM), F32)]
    if kv_new is not None:
        row_blk = pl.BlockSpec((1, 1, KV_WIDTH), lambda b: (b, 0, 0))
        in_specs += [row_blk, row_blk]
        args += list(kv_new)
        out_specs += [cache_blk, cache_blk]
        out_shape += [jax.ShapeDtypeStruct(k_cache.shape, F32)] * 2
    outs = pl.pallas_call(
        functools.partial(_swa_step_body, shift_cache=kv_new is not None),
        grid=(bsz,),
        in_specs=in_specs,
        out_specs=out_specs,
        out_shape=out_shape,
        compiler_params=_params(("parallel",)),
        name="swa_step",
    )(*args)
    return outs if kv_new is not None else outs[0]


def _trunk(x, mods, kv_mod, per_token, hgrn_state0, k_buf, v_buf, wts, lbs, bias_tab, bias_row):
    (w_in_a, w_o_a, gnorm_a, w_q_b, w_o_b, sinks_b, w_ffn_in, w_ffn_out, final_norm_w,
     w_kv, w_kv_rep) = wts
    prompt = k_buf is None
    bsz, t, _ = x.shape
    m = bsz * t
    common = dict(per_token=per_token, rows_per_batch=t)
    h = x.reshape(m, D_MODEL)
    states = []
    new_state = None
    kv = k_state = v_state = kv_new = None
    for l in range(DEPTH):
        a1, s1, g1, a2, s2, g2 = mods[l]
        if l < N_A_LAYERS:
            qkvg, lf = norm_proj(h, (a1, s1), w_in_a, l, epilogue="hgrn",
                                 lb=lbs[l].reshape(1, D_MODEL), **common)
            gn = gnorm_a[l].reshape(1, A_DV)
            if prompt:
                mix, s_new = hgrn_scan_prompt(qkvg, lf, gn, bsz, t)
                states.append(s_new)
            else:
                mix, new_state = hgrn_step(qkvg.astype(F32), lf, gn, hgrn_state0, l, new_state)
            w_o, lo = w_o_a, l
        else:
            j = l - N_A_LAYERS
            scale = 1.0 / math.sqrt(B_HEAD_DIM)
            if prompt:
                q = norm_proj(h, (a1, s1), w_q_b, j, out_scale=scale, **common)
                mix = swa_prompt(q, kv, sinks_b[j], bias_tab, bsz, t)
            else:
                q = norm_proj(h, (a1, s1), w_q_b, j, out_scale=scale, out_dtype=F32, **common)
                lane_kv = (np.arange(KV_WIDTH) // B_HEAD_DIM)[None, :]
                head_kv = (np.arange(B_HEADS) // B_GROUPS)[:, None]
                qz = jnp.where((lane_kv == head_kv)[None],
                               jnp.tile(q.reshape(m, B_HEADS, B_HEAD_DIM), (1, 1, B_KV_HEADS)), 0.0)
                if j == 0:
                    mix, k_state, v_state = swa_step(qz, k_buf.reshape(m, WINDOW, KV_WIDTH),
                                                     v_buf.reshape(m, WINDOW, KV_WIDTH),
                                                     sinks_b[j], bias_row, kv_new)
                else:
                    mix = swa_step(qz, k_state, v_state, sinks_b[j], bias_row)
                mix = mix.reshape(m, D_MODEL)
            w_o, lo = w_o_b, j
        h = post_ffn(mix, h, g1, (a2, s2), g2, w_o, lo, w_ffn_in, w_ffn_out, l,
                     final_w=final_norm_w if l == DEPTH - 1 else None, **common)
        if l == N_A_LAYERS - 1:
            if prompt:
                kv = norm_proj(h, kv_mod, w_kv_rep, 0, **common)
                tail = h.reshape(bsz, t, D_MODEL)[:, -WINDOW:].reshape(bsz * WINDOW, D_MODEL)
                kv_tail = norm_proj(tail, kv_mod, w_kv, 0, per_token=False,
                                    rows_per_batch=WINDOW, out_dtype=F32)
                kv_tail = kv_tail.reshape(bsz, WINDOW, 2, B_KV_HEADS, B_HEAD_DIM)
                k_state, v_state = kv_tail[:, :, 0], kv_tail[:, :, 1]
            else:
                kv_row = norm_proj(h, kv_mod, w_kv, 0, out_dtype=F32, **common)
                kv_new = (kv_row[:, :KV_WIDTH].reshape(m, 1, KV_WIDTH),
                          kv_row[:, KV_WIDTH:].reshape(m, 1, KV_WIDTH))
    y = h.reshape(bsz, t, D_MODEL)
    if prompt:
        return y, jnp.stack(states, axis=1), k_state, v_state
    cache_shape = (m, WINDOW, B_KV_HEADS, B_HEAD_DIM)
    return y, new_state, k_state.reshape(cache_shape), v_state.reshape(cache_shape)


def kernel(x_prompt, x_sample, state_hgrn, cache_swa_k, cache_swa_v, c_prompt, c_sample,
           w_in_a, w_o_a, gnorm_a, lb_a, w_kv, w_ada_kv, b_ada_kv, kv_norm_w, w_q_b, w_o_b,
           sinks_b, rel_bias, norm_w, w_ada, b_ada, w_ffn_in, w_ffn_out, final_norm_w):
    n_p, n_s = c_prompt.shape[0], c_sample.shape[0]
    rows = -(-(n_p + n_s) // 8) * 8
    c_all = jnp.concatenate(
        [c_prompt, c_sample, jnp.zeros((rows - n_p - n_s, D_MODEL), F32)], axis=0)

    zeros, ones = jnp.zeros((DEPTH, D_MODEL), F32), jnp.ones((DEPTH, D_MODEL), F32)
    p_ada = jnp.stack([zeros, norm_w[:, 0], zeros, zeros, norm_w[:, 1], zeros], axis=1)
    q_ada = jnp.stack([ones, norm_w[:, 0], ones, ones, norm_w[:, 1], ones], axis=1)
    ada = ada_project(c_all, w_ada, b_ada.reshape(DEPTH, 6, 1, D_MODEL),
                      p_ada.reshape(DEPTH, 6, 1, D_MODEL), q_ada.reshape(DEPTH, 6, 1, D_MODEL))
    p_kv = jnp.stack([zeros[0], kv_norm_w]).reshape(1, 2, 1, D_MODEL)
    q_kv = jnp.stack([ones[0], kv_norm_w]).reshape(1, 2, 1, D_MODEL)
    ada_kv = ada_project(c_all, w_ada_kv.reshape(1, D_MODEL, 2 * D_MODEL),
                         b_ada_kv.reshape(1, 2, 1, D_MODEL), p_kv, q_kv)

    def split_mods(arr, n_chunks, lo, hi, per_token):
        out = []
        for c in range(n_chunks):
            v = arr[lo:hi, c * D_MODEL:(c + 1) * D_MODEL]
            out.append(v if per_token else v.reshape(hi - lo, 1, D_MODEL))
        return out

    def mods_for(lo, hi, per_token):
        layers = []
        for l in range(DEPTH):
            sh1, a1, g1, sh2, a2, g2 = split_mods(ada[l], 6, lo, hi, per_token)
            layers.append((a1, sh1, g1, a2, sh2, g2))
        sh_kv, a_kv = split_mods(ada_kv[0], 2, lo, hi, per_token)
        return layers, (a_kv, sh_kv)

    lb_sm = jax.nn.softmax(lb_a.astype(F32), axis=0)
    lbs = jnp.cumsum(lb_sm, axis=0) - lb_sm[0:1]

    w_k = w_kv[:, :KV_WIDTH].reshape(D_MODEL, B_KV_HEADS, 1, B_HEAD_DIM)
    w_v = w_kv[:, KV_WIDTH:].reshape(D_MODEL, B_KV_HEADS, 1, B_HEAD_DIM)
    rep = lambda w: jnp.tile(w, (1, 1, B_GROUPS, 1)).reshape(D_MODEL, D_MODEL)
    w_kv_rep = jnp.concatenate([rep(w_k), rep(w_v)], axis=1)[None].astype(BF16)

    rb = rel_bias.astype(F32)[_T5_BUCKETS]
    dist = np.arange(WINDOW)[:, None] + WINDOW - np.arange(2 * WINDOW)[None, :]
    onehot = (jnp.asarray(np.clip(dist, 0, WINDOW - 1))[:, :, None]
              == jnp.arange(WINDOW)[None, None, :]).astype(F32)
    bias_tab = jnp.einsum("tsd,dh->hts", onehot, rb, precision=lax.Precision.HIGHEST)
    bias_row = rb[::-1].T

    bf = lambda w: w.astype(BF16)
    wts = (bf(w_in_a), bf(w_o_a), gnorm_a, bf(w_q_b), bf(w_o_b), sinks_b, bf(w_ffn_in),
           bf(w_ffn_out), final_norm_w, bf(w_kv)[None], w_kv_rep)
    mods_p, kv_mod_p = mods_for(0, n_p, False)
    mods_s, kv_mod_s = mods_for(n_p, n_p + n_s, True)
    y_p, st_p, k_p, v_p = _trunk(x_prompt, mods_p, kv_mod_p, False, None, None, None, wts, lbs,
                                 bias_tab, bias_row)
    y_s, st_s, k_s, v_s = _trunk(x_sample, mods_s, kv_mod_s, True, state_hgrn, cache_swa_k,
                                 cache_swa_v, wts, lbs, bias_tab, bias_row)
    return (y_p, y_s, st_p, st_s, k_p, v_p, k_s, v_s)
```

```python
import functools
import math

import numpy as np
import jax
import jax.numpy as jnp
from jax import lax
from jax.experimental import pallas as pl
from jax.experimental.pallas import tpu as pltpu

F32 = jnp.float32
BF16 = jnp.bfloat16

D_MODEL = 1024
DEPTH = 4
N_A_LAYERS = 2
A_HEADS = 8
A_DK = 128
A_DV = 128
F_MIN = 1e-30
B_HEAD_DIM = 64
B_HEADS = 16
B_KV_HEADS = 4
B_GROUPS = 4
WINDOW = 128
MASK_VALUE = -1e30
N_BUCKETS = 32
MAX_DISTANCE = 128
D_FF = 2816
EPS = 1e-6

CHUNK = 128
N_LEVELS = 7
ROW_TILE = 512
COL_TILE = 1024
FF_TILE = 256
STEP_TOKENS = 4
ATTN_TOKENS = 8
VMEM_LIMIT_BYTES = 48 * 1024 * 1024


def _params(sem):
    return pltpu.CompilerParams(dimension_semantics=sem, vmem_limit_bytes=VMEM_LIMIT_BYTES)


def _row_tile(m, per_token, rows_per_batch):
    return min(m, ROW_TILE) if per_token else min(m, ROW_TILE, rows_per_batch)


def _sigmoid(x):
    return 1.0 / (1.0 + jnp.exp(-x))


def _silu(x):
    return x * _sigmoid(x)


def _norm_mod(x, a, s):
    y = x * lax.rsqrt(jnp.mean(x * x, axis=-1, keepdims=True) + EPS)
    return (y * a + s).astype(BF16)


def _resident(shape, layer):
    return pl.BlockSpec((None,) + shape, lambda i: (layer,) + (0,) * len(shape),
                        pipeline_mode=pl.Buffered(1))


def _row_operand(width, tm, per_token, rows_per_batch):
    if per_token:
        return pl.BlockSpec((tm, width), lambda i: (i, 0))
    return pl.BlockSpec((1, 1, width), lambda i: ((i * tm) // rows_per_batch, 0, 0))


def _row_value(ref, per_token):
    return ref[...] if per_token else ref[0]


def _norm_proj_body(*refs, per_token, epilogue, out_scale):
    x_ref, a_ref, s_ref, w_ref = refs[:4]
    xn = _norm_mod(x_ref[...], _row_value(a_ref, per_token), _row_value(s_ref, per_token))

    tn = min(COL_TILE, w_ref.shape[1])

    def proj(c):
        return jnp.dot(xn, w_ref[:, c:c + tn], preferred_element_type=F32)

    if epilogue == "plain":
        (o_ref,) = refs[4:]
        for c in range(0, o_ref.shape[1], tn):
            acc = proj(c)
            if out_scale is not None:
                acc = acc * out_scale
            o_ref[:, c:c + tn] = acc.astype(o_ref.dtype)
    elif epilogue == "hgrn":
        lb_ref, o_ref, lf_ref = refs[4:]
        d = D_MODEL
        lb = lb_ref[...]
        o_ref[:, 0:d] = _silu(proj(0)).astype(o_ref.dtype)
        f = lb + (1.0 - lb) * _sigmoid(proj(d))
        o_ref[:, d:2 * d] = (1.0 - f).astype(o_ref.dtype)
        lf_ref[...] = jnp.log2(jnp.maximum(f, F_MIN))
        o_ref[:, 2 * d:3 * d] = proj(2 * d).astype(o_ref.dtype)
        o_ref[:, 3 * d:4 * d] = _silu(proj(3 * d)).astype(o_ref.dtype)
    else:
        raise ValueError(epilogue)


def norm_proj(x, mod, w, layer, *, per_token, rows_per_batch, epilogue="plain", lb=None,
              out_dtype=BF16, out_scale=None):
    m, k = x.shape
    n = w.shape[-1]
    tm = _row_tile(m, per_token, rows_per_batch)
    assert m % tm == 0 and (per_token or rows_per_batch % tm == 0)
    row = functools.partial(_row_operand, tm=tm, per_token=per_token, rows_per_batch=rows_per_batch)
    in_specs = [pl.BlockSpec((tm, k), lambda i: (i, 0)), row(k), row(k), _resident((k, n), layer)]
    args = [x, mod[0], mod[1], w]
    out_specs = [pl.BlockSpec((tm, n), lambda i: (i, 0))]
    out_shape = [jax.ShapeDtypeStruct((m, n), out_dtype)]
    if epilogue == "hgrn":
        in_specs.append(pl.BlockSpec((1, D_MODEL), lambda i: (0, 0)))
        args.append(lb)
        out_specs.append(pl.BlockSpec((tm, D_MODEL), lambda i: (i, 0)))
        out_shape.append(jax.ShapeDtypeStruct((m, D_MODEL), F32))
    outs = pl.pallas_call(
        functools.partial(_norm_proj_body, per_token=per_token, epilogue=epilogue,
                          out_scale=out_scale),
        grid=(m // tm,),
        in_specs=in_specs,
        out_specs=out_specs,
        out_shape=out_shape,
        compiler_params=_params(("parallel",)),
        name=f"norm_proj_{epilogue}_m{m}_n{n}",
    )(*args)
    return outs if epilogue == "hgrn" else outs[0]


def _post_ffn_body(*refs, per_token, final_norm):
    (mix_ref, h_ref, g1_ref, a2_ref, s2_ref, g2_ref, wo_ref, win_ref, wout_ref) = refs[:9]
    rest = refs[9:]
    fw_ref = rest[0] if final_norm else None
    o_ref, act_ref = rest[-2:]
    row = functools.partial(_row_value, per_token=per_token)
    h_mid = h_ref[...] + row(g1_ref) * jnp.dot(mix_ref[...].astype(BF16), wo_ref[...],
                                               preferred_element_type=F32)
    xn = _norm_mod(h_mid, row(a2_ref), row(s2_ref))
    for c in range(0, D_FF, FF_TILE):
        gate = jnp.dot(xn, win_ref[:, c:c + FF_TILE], preferred_element_type=F32)
        up = jnp.dot(xn, win_ref[:, D_FF + c:D_FF + c + FF_TILE], preferred_element_type=F32)
        act_ref[:, c:c + FF_TILE] = (_silu(gate) * up).astype(BF16)
    out = h_mid + row(g2_ref) * jnp.dot(act_ref[...], wout_ref[...], preferred_element_type=F32)
    if final_norm:
        out = out * lax.rsqrt(jnp.mean(out * out, axis=-1, keepdims=True) + EPS) * fw_ref[...]
    o_ref[...] = out


def post_ffn(mix, h, g1, mod2, g2, w_o, lo, w_ffn_in, w_ffn_out, lf, *, per_token,
             rows_per_batch, final_w=None):
    m, d = h.shape
    tm = _row_tile(m, per_token, rows_per_batch)
    assert m % tm == 0 and (per_token or rows_per_batch % tm == 0)
    row = functools.partial(_row_operand, d, tm, per_token, rows_per_batch)
    tile = pl.BlockSpec((tm, d), lambda i: (i, 0))
    in_specs = [tile, tile, row(), row(), row(), row(),
                _resident((d, d), lo), _resident((d, 2 * D_FF), lf), _resident((D_FF, d), lf)]
    args = [mix, h, g1, mod2[0], mod2[1], g2, w_o, w_ffn_in, w_ffn_out]
    if final_w is not None:
        in_specs.append(pl.BlockSpec((1, d), lambda i: (0, 0)))
        args.append(final_w.reshape(1, d))
    return pl.pallas_call(
        functools.partial(_post_ffn_body, per_token=per_token, final_norm=final_w is not None),
        grid=(m // tm,),
        in_specs=in_specs,
        out_specs=tile,
        out_shape=jax.ShapeDtypeStruct((m, d), F32),
        scratch_shapes=[pltpu.VMEM((tm, D_FF), BF16)],
        compiler_params=_params(("parallel",)),
        name=f"post_ffn_m{m}",
    )(*args)


def _ada_body(c_ref, w_ref, b_ref, p_ref, q_ref, o_ref):
    c = _silu(c_ref[...]).astype(BF16)
    acc = jnp.dot(c, w_ref[...].astype(BF16), preferred_element_type=F32)
    o_ref[...] = p_ref[...] + q_ref[...] * (acc + b_ref[...])


def ada_project(c_all, w, b, p, q):
    n_l, _, n_cols = w.shape
    n_c = n_cols // D_MODEL
    rows = c_all.shape[0]
    vec_spec = pl.BlockSpec((None, None, 1, D_MODEL), lambda l, j: (l, j, 0, 0))
    return pl.pallas_call(
        _ada_body,
        grid=(n_l, n_c),
        in_specs=[
            pl.BlockSpec((rows, D_MODEL), lambda l, j: (0, 0)),
            pl.BlockSpec((None, D_MODEL, D_MODEL), lambda l, j: (l, 0, j)),
            vec_spec, vec_spec, vec_spec,
        ],
        out_specs=pl.BlockSpec((None, rows, D_MODEL), lambda l, j: (l, 0, j)),
        out_shape=jax.ShapeDtypeStruct((n_l, rows, n_cols), F32),
        compiler_params=_params(("parallel", "parallel")),
        name="ada_project",
    )(c_all, w, b, p, q)


MATMUL_LEVELS = (5, 6)


def _scan_tables():
    c = CHUNK
    t = np.arange(c)[:, None]
    u = np.arange(c)[None, :]
    blocks = [(u <= t)]
    for level in MATMUL_LEVELS:
        p = N_LEVELS - level
        odd = ((t >> p) & 1) == 1
        start = (t >> p) << p
        end = (((t >> p) + 1) << p) - 1
        blocks.append(np.where(odd, (u >= start) & (u <= t), (u > t) & (u <= end)))
    sums = np.concatenate(blocks, axis=0).astype(np.float32)
    x = np.arange(c)[:, None] ^ np.arange(c)[None, :]
    msb = np.floor(np.log2(np.maximum(x, 1))).astype(np.int32)
    lvl = np.where(x == 0, 0, N_LEVELS - msb)
    lvl = np.where(np.arange(c)[None, :] > np.arange(c)[:, None], -1, lvl).astype(np.int32)
    return sums, lvl


_SCAN_SUMS, _SCAN_LEVELS = _scan_tables()

_TN = (((1,), (1,)), ((), ()))
_TM = (((0,), (0,)), ((), ()))


def _level_exponent(ex_ref, log2_f, odd_rows, level):
    m = CHUNK >> level
    if level in MATMUL_LEVELS:
        i = 1 + MATMUL_LEVELS.index(level)
        return ex_ref[i * CHUNK:(i + 1) * CHUNK, :]
    if m == 1:
        return jnp.where(odd_rows, log2_f, 0.0)
    parts = []
    for i in range(CHUNK // m):
        blk = ex_ref[i * m:(i + 1) * m, :]
        if i & 1:
            parts.append(blk - ex_ref[i * m - 1:i * m, :])
        else:
            parts.append(ex_ref[(i + 1) * m - 1:(i + 1) * m, :] - blk)
    return jnp.concatenate(parts, axis=0)


def _level_mix(q, kk, odd_rows, level):
    m = CHUNK >> level
    if m >= 8:
        parts = [(q if (i & 1) else kk)[i * m:(i + 1) * m] for i in range(CHUNK // m)]
        return jnp.concatenate(parts, axis=0)
    return jnp.where(odd_rows, q, kk)


def _hgrn_scan_body(q_ref, k_ref, v_ref, g_ref, lf_ref, gn_ref, sums_ref, lvl_ref,
                    o_ref, st_ref, state_ref, ex_ref):
    n = pl.program_id(1)

    @pl.when(n == 0)
    def _():
        state_ref[...] = jnp.zeros_like(state_ref)

    log2_f = lf_ref[...]
    hi = log2_f.astype(BF16)
    rem = log2_f - hi.astype(F32)
    mid = rem.astype(BF16)
    lo = (rem - mid.astype(F32)).astype(BF16)
    ex_ref[...] = jnp.dot(sums_ref[...], jnp.concatenate([hi, mid, lo], axis=0),
                          preferred_element_type=F32)

    q_bf, k_bf, v = q_ref[...], k_ref[...], v_ref[...]
    q, kk = q_bf.astype(F32), k_bf.astype(F32)
    b_incl = ex_ref[0:CHUNK, :]
    b_end = ex_ref[CHUNK - 1:CHUNK, :]
    q_dec = (q * jnp.exp2(b_incl)).astype(BF16)
    k_dec = (kk * jnp.exp2(b_end - b_incl)).astype(BF16)
    d_end = jnp.exp2(b_end)
    heads = [slice(h * A_DK, (h + 1) * A_DK) for h in range(A_HEADS)]

    lvl = lvl_ref[...]
    row = lax.broadcasted_iota(jnp.int32, (CHUNK, D_MODEL), 0)
    atts = [jnp.where(lvl == 0, lax.dot_general(q_bf[:, sl], k_bf[:, sl], _TN,
                                                preferred_element_type=F32), 0.0)
            for sl in heads]
    for level in range(1, N_LEVELS + 1):
        odd_rows = ((row >> (N_LEVELS - level)) & 1) == 1
        e = jnp.exp2(_level_exponent(ex_ref, log2_f, odd_rows, level))
        mix = (_level_mix(q, kk, odd_rows, level) * e).astype(BF16)
        for h, sl in enumerate(heads):
            a = lax.dot_general(mix[:, sl], mix[:, sl], _TN, preferred_element_type=F32)
            atts[h] = jnp.where(lvl == level, a, atts[h])

    gn = gn_ref[...]
    gate = g_ref[...].astype(F32)
    for h, sl in enumerate(heads):
        s_t = state_ref[h]
        o = (lax.dot_general(q_dec[:, sl], s_t.astype(BF16), _TN, preferred_element_type=F32)
             + jnp.dot(atts[h].astype(BF16), v[:, sl], preferred_element_type=F32))
        state_ref[h] = d_end[:, sl] * s_t + lax.dot_general(
            v[:, sl], k_dec[:, sl], _TM, preferred_element_type=F32)
        y = o * lax.rsqrt(jnp.mean(o * o, axis=-1, keepdims=True) + EPS)
        o_ref[:, sl] = (y * gn * gate[:, sl]).astype(o_ref.dtype)

    @pl.when(n == pl.num_programs(1) - 1)
    def _():
        for h in range(A_HEADS):
            st_ref[0, h] = state_ref[h].T


def hgrn_scan_prompt(qkvg, lf, gn, bsz, t):
    nc = t // CHUNK
    sec = lambda k: pl.BlockSpec((CHUNK, D_MODEL), lambda b, n, k=k: (b * nc + n, k))
    n_sum = _SCAN_SUMS.shape[0]
    return pl.pallas_call(
        _hgrn_scan_body,
        grid=(bsz, nc),
        in_specs=[sec(0), sec(1), sec(2), sec(3), sec(0),
                  pl.BlockSpec((1, A_DV), lambda b, n: (0, 0)),
                  pl.BlockSpec((n_sum, 3 * CHUNK), lambda b, n: (0, 0)),
                  pl.BlockSpec((CHUNK, CHUNK), lambda b, n: (0, 0))],
        out_specs=[pl.BlockSpec((CHUNK, D_MODEL), lambda b, n: (b * nc + n, 0)),
                   pl.BlockSpec((1, A_HEADS, A_DK, A_DV), lambda b, n: (b, 0, 0, 0))],
        out_shape=[jax.ShapeDtypeStruct((bsz * t, D_MODEL), BF16),
                   jax.ShapeDtypeStruct((bsz, A_HEADS, A_DK, A_DV), F32)],
        scratch_shapes=[pltpu.VMEM((A_HEADS, A_DV, A_DK), F32),
                        pltpu.VMEM((n_sum, D_MODEL), F32)],
        compiler_params=_params(("parallel", "arbitrary")),
        name="hgrn_scan",
    )(qkvg, qkvg, qkvg, qkvg, lf, gn, jnp.asarray(np.tile(_SCAN_SUMS, (1, 3)), BF16),
      jnp.asarray(_SCAN_LEVELS))


def _hgrn_step_body(*refs):
    q_ref, k_ref, v_ref, g_ref, lf_ref, gn_ref, s_ref = refs[:7]
    o_ref, so_ref = refs[-2:]
    gn = gn_ref[...]

    def column(rowvec):
        return jnp.broadcast_to(rowvec, (A_DK, A_DK)).T

    for b in range(STEP_TOKENS):
        decay = jnp.exp2(lf_ref[b])
        q, kk, v, gate = q_ref[b], k_ref[b], v_ref[b], g_ref[b]
        for h in range(A_HEADS):
            sl = slice(h * A_DK, (h + 1) * A_DK)
            s_new = column(decay[:, sl]) * s_ref[b, h] + column(kk[:, sl]) * v[:, sl]
            so_ref[b, h] = s_new
            o = jnp.sum(column(q[:, sl]) * s_new, axis=0, keepdims=True)
            y = o * lax.rsqrt(jnp.mean(o * o, axis=-1, keepdims=True) + EPS)
            o_ref[b, :, sl] = y * gn * gate[:, sl]


def hgrn_step(qkvg, lf, gn, state, layer, new_state):
    bsz = qkvg.shape[0]
    qkvg3 = qkvg.reshape(bsz, 1, 4 * D_MODEL)
    nb = STEP_TOKENS
    assert bsz % nb == 0
    sec = lambda k: pl.BlockSpec((nb, 1, D_MODEL), lambda b, k=k: (b, 0, k))
    state_blk = pl.BlockSpec((nb, None, A_HEADS, A_DK, A_DV), lambda b: (b, layer, 0, 0, 0))
    in_specs = [sec(0), sec(1), sec(2), sec(3), sec(0),
                pl.BlockSpec((1, A_DV), lambda b: (0, 0)), state_blk]
    args = [qkvg3, qkvg3, qkvg3, qkvg3, lf.reshape(bsz, 1, D_MODEL), gn, state]
    aliases = {}
    if new_state is not None:
        in_specs.append(pl.BlockSpec(memory_space=pl.ANY))
        args.append(new_state)
        aliases = {len(args) - 1: 1}
    o, s_new = pl.pallas_call(
        _hgrn_step_body,
        grid=(bsz // nb,),
        in_specs=in_specs,
        out_specs=[pl.BlockSpec((nb, 1, D_MODEL), lambda b: (b, 0, 0)), state_blk],
        out_shape=[jax.ShapeDtypeStruct((bsz, 1, D_MODEL), F32),
                   jax.ShapeDtypeStruct(state.shape, F32)],
        input_output_aliases=aliases,
        compiler_params=_params(("parallel",)),
        name="hgrn_step",
    )(*args)
    return o.reshape(bsz, D_MODEL), s_new


GROUP_LANES = B_GROUPS * B_HEAD_DIM
KV_WIDTH = B_KV_HEADS * B_HEAD_DIM
HEAD_DIM_SHIFT = B_HEAD_DIM.bit_length() - 1
GROUPS_SHIFT = B_GROUPS.bit_length() - 1


def _t5_buckets():
    max_exact = N_BUCKETS // 2
    d = np.arange(WINDOW)
    large = max_exact + (np.log(np.maximum(d, 1).astype(np.float32) / max_exact)
                         / math.log(MAX_DISTANCE / max_exact)
                         * (N_BUCKETS - max_exact)).astype(np.int32)
    large = np.clip(large, 0, N_BUCKETS - 1)
    return np.where(d < max_exact, d, large).astype(np.int32)


_T5_BUCKETS = _t5_buckets()


LOG2E = math.log2(math.e)


def _swa_prompt_body(sink_ref, q_ref, kp_ref, kc_ref, vp_ref, vc_ref, bias_ref, o_ref):
    lane_head = lax.broadcasted_iota(jnp.int32, (WINDOW, GROUP_LANES), 1) >> HEAD_DIM_SHIFT
    for g in range(B_KV_HEADS):
        cs = slice(g * GROUP_LANES, (g + 1) * GROUP_LANES)
        qg = q_ref[:, cs].astype(F32)
        kcat = jnp.concatenate([kp_ref[:, cs], kc_ref[:, cs]], axis=0)
        vcat = jnp.concatenate([vp_ref[:, cs], vc_ref[:, cs]], axis=0)
        qstack = jnp.concatenate(
            [jnp.where(lane_head == j, qg, 0.0).astype(BF16) for j in range(B_GROUPS)], axis=0)
        logits = lax.dot_general(qstack, kcat, _TN, preferred_element_type=F32)
        acc = jnp.zeros((WINDOW, GROUP_LANES), F32)
        for j in range(B_GROUPS):
            head = g * B_GROUPS + j
            lg = logits[j * WINDOW:(j + 1) * WINDOW] + bias_ref[head]
            sink = sink_ref[head] * LOG2E
            m = jnp.maximum(jnp.max(lg, axis=-1, keepdims=True), sink)
            p = jnp.exp2(lg - m)
            denom = jnp.sum(p, axis=-1, keepdims=True) + jnp.exp2(sink - m)
            pv = jnp.dot(p.astype(BF16), vcat, preferred_element_type=F32)
            acc = jnp.where(lane_head == j, pv / denom, acc)
        o_ref[:, cs] = acc.astype(o_ref.dtype)


def swa_prompt(q, kv_rep, sinks, bias_tabs, bsz, t):
    nb = t // WINDOW
    cur = lambda c: pl.BlockSpec((WINDOW, D_MODEL), lambda b, n, c=c: (b * nb + n, c))
    prev = lambda c: pl.BlockSpec(
        (WINDOW, D_MODEL), lambda b, n, c=c: (b * nb + jnp.maximum(n - 1, 0), c))
    return pl.pallas_call(
        _swa_prompt_body,
        grid=(bsz, nb),
        in_specs=[pl.BlockSpec(memory_space=pltpu.SMEM),
                  cur(0), prev(0), cur(0), prev(1), cur(1),
                  pl.BlockSpec((None, B_HEADS, WINDOW, 2 * WINDOW),
                               lambda b, n: (jnp.minimum(n, 1), 0, 0, 0))],
        out_specs=pl.BlockSpec((WINDOW, D_MODEL), lambda b, n: (b * nb + n, 0)),
        out_shape=jax.ShapeDtypeStruct((bsz * t, D_MODEL), BF16),
        compiler_params=_params(("parallel", "arbitrary")),
        name="swa_prompt",
    )(sinks, q, kv_rep, kv_rep, kv_rep, kv_rep, bias_tabs)


def _swa_step_body(*refs, shift_cache):
    sink_ref, qz_ref, k_ref, v_ref, bias_ref, fold_ref = refs[:6]
    if shift_cache:
        kn_ref, vn_ref, o_ref, ko_ref, vo_ref = refs[6:]
        last = lax.broadcasted_iota(jnp.int32, (WINDOW, KV_WIDTH), 0) == WINDOW - 1
    else:
        (o_ref,) = refs[6:]
    sink = sink_ref[...] * LOG2E
    head_kv = lax.broadcasted_iota(jnp.int32, (B_HEADS, KV_WIDTH), 0) >> GROUPS_SHIFT
    lane_kv = lax.broadcasted_iota(jnp.int32, (B_HEADS, KV_WIDTH), 1) >> HEAD_DIM_SHIFT
    for b in range(ATTN_TOKENS):
        if shift_cache:
            k = jnp.where(last, kn_ref[b], pltpu.roll(k_ref[b], WINDOW - 1, 0))
            v = jnp.where(last, vn_ref[b], pltpu.roll(v_ref[b], WINDOW - 1, 0))
            ko_ref[b] = k
            vo_ref[b] = v
        else:
            k, v = k_ref[b], v_ref[b]
        qz = qz_ref[b].astype(BF16)
        logits = lax.dot_general(qz, k.astype(BF16), _TN, preferred_element_type=F32)
        logits = logits + bias_ref[...]
        m = jnp.maximum(jnp.max(logits, axis=-1, keepdims=True), sink)
        p = jnp.exp2(logits - m)
        denom = jnp.sum(p, axis=-1, keepdims=True) + jnp.exp2(sink - m)
        pv = jnp.dot(p.astype(BF16), v.astype(BF16), preferred_element_type=F32) / denom
        own = jnp.where(head_kv == lane_kv, pv, 0.0).astype(BF16)
        o_ref[b] = jnp.dot(own, fold_ref[...], preferred_element_type=F32)


def swa_step(qz, k_cache, v_cache, sinks, bias_row, kv_new=None):
    bsz = qz.shape[0]
    nt = ATTN_TOKENS
    assert bsz % nt == 0
    fold = np.zeros((KV_WIDTH, B_HEAD_DIM), np.float32)
    fold[np.arange(KV_WIDTH), np.arange(KV_WIDTH) % B_HEAD_DIM] = 1.0
    cache_blk = pl.BlockSpec((nt, WINDOW, KV_WIDTH), lambda b: (b, 0, 0))
    in_specs = [pl.BlockSpec((B_HEADS, 1), lambda b: (0, 0)),
                pl.BlockSpec((nt, B_HEADS, KV_WIDTH), lambda b: (b, 0, 0)),
                cache_blk, cache_blk,
                pl.BlockSpec((B_HEADS, WINDOW), lambda b: (0, 0)),
                pl.BlockSpec((KV_WIDTH, B_HEAD_DIM), lambda b: (0, 0))]
    args = [sinks.reshape(B_HEADS, 1), qz, k_cache, v_cache, bias_row, jnp.asarray(fold, BF16)]
    out_specs = [pl.BlockSpec((nt, B_HEADS, B_HEAD_DIM), lambda b: (b, 0, 0))]
    out_shape = [jax.ShapeDtypeStruct((bsz, B_HEADS, B_HEAD_DIM), F32)]
    if kv_new is not None:
        row_blk = pl.BlockSpec((nt, 1, KV_WIDTH), lambda b: (b, 0, 0))
        in_specs += [row_blk, row_blk]
        args += list(kv_new)
        out_specs += [cache_blk, cache_blk]
        out_shape += [jax.ShapeDtypeStruct(k_cache.shape, F32)] * 2
    outs = pl.pallas_call(
        functools.partial(_swa_step_body, shift_cache=kv_new is not None),
        grid=(bsz // nt,),
        in_specs=in_specs,
        out_specs=out_specs,
        out_shape=out_shape,
        compiler_params=_params(("parallel",)),
        name="swa_step",
    )(*args)
    return outs if kv_new is not None else outs[0]


def _trunk(x, mods, kv_mod, per_token, hgrn_state0, k_buf, v_buf, wts, lbs, bias_tab, bias_row):
    (w_in_a, w_o_a, gnorm_a, w_q_b, w_o_b, sinks_b, w_ffn_in, w_ffn_out, final_norm_w,
     w_kv, w_kv_rep) = wts
    prompt = k_buf is None
    bsz, t, _ = x.shape
    m = bsz * t
    common = dict(per_token=per_token, rows_per_batch=t)
    h = x.reshape(m, D_MODEL)
    states = []
    new_state = None
    kv = k_state = v_state = kv_new = None
    for l in range(DEPTH):
        a1, s1, g1, a2, s2, g2 = mods[l]
        if l < N_A_LAYERS:
            qkvg, lf = norm_proj(h, (a1, s1), w_in_a, l, epilogue="hgrn",
                                 lb=lbs[l].reshape(1, D_MODEL), **common)
            gn = gnorm_a[l].reshape(1, A_DV)
            if prompt:
                mix, s_new = hgrn_scan_prompt(qkvg, lf, gn, bsz, t)
                states.append(s_new)
            else:
                mix, new_state = hgrn_step(qkvg.astype(F32), lf, gn, hgrn_state0, l, new_state)
            w_o, lo = w_o_a, l
        else:
            j = l - N_A_LAYERS
            scale = LOG2E / math.sqrt(B_HEAD_DIM)
            if prompt:
                q = norm_proj(h, (a1, s1), w_q_b, j, out_scale=scale, **common)
                mix = swa_prompt(q, kv, sinks_b[j], bias_tab, bsz, t)
            else:
                q = norm_proj(h, (a1, s1), w_q_b, j, out_scale=scale, out_dtype=F32, **common)
                lane_kv = (np.arange(KV_WIDTH) // B_HEAD_DIM)[None, :]
                head_kv = (np.arange(B_HEADS) // B_GROUPS)[:, None]
                qz = jnp.where((lane_kv == head_kv)[None],
                               jnp.tile(q.reshape(m, B_HEADS, B_HEAD_DIM), (1, 1, B_KV_HEADS)), 0.0)
                if j == 0:
                    mix, k_state, v_state = swa_step(qz, k_buf.reshape(m, WINDOW, KV_WIDTH),
                                                     v_buf.reshape(m, WINDOW, KV_WIDTH),
                                                     sinks_b[j], bias_row, kv_new)
                else:
                    mix = swa_step(qz, k_state, v_state, sinks_b[j], bias_row)
                mix = mix.reshape(m, D_MODEL)
            w_o, lo = w_o_b, j
        h = post_ffn(mix, h, g1, (a2, s2), g2, w_o, lo, w_ffn_in, w_ffn_out, l,
                     final_w=final_norm_w if l == DEPTH - 1 else None, **common)
        if l == N_A_LAYERS - 1:
            if prompt:
                kv = norm_proj(h, kv_mod, w_kv_rep, 0, **common)
                tail = h.reshape(bsz, t, D_MODEL)[:, -WINDOW:].reshape(bsz * WINDOW, D_MODEL)
                kv_tail = norm_proj(tail, kv_mod, w_kv, 0, per_token=False,
                                    rows_per_batch=WINDOW, out_dtype=F32)
                kv_tail = kv_tail.reshape(bsz, WINDOW, 2, B_KV_HEADS, B_HEAD_DIM)
                k_state, v_state = kv_tail[:, :, 0], kv_tail[:, :, 1]
            else:
                kv_row = norm_proj(h, kv_mod, w_kv, 0, out_dtype=F32, **common)
                kv_new = (kv_row[:, :KV_WIDTH].reshape(m, 1, KV_WIDTH),
                          kv_row[:, KV_WIDTH:].reshape(m, 1, KV_WIDTH))
    y = h.reshape(bsz, t, D_MODEL)
    if prompt:
        return y, jnp.stack(states, axis=1), k_state, v_state
    cache_shape = (m, WINDOW, B_KV_HEADS, B_HEAD_DIM)
    return y, new_state, k_state.reshape(cache_shape), v_state.reshape(cache_shape)


def kernel(x_prompt, x_sample, state_hgrn, cache_swa_k, cache_swa_v, c_prompt, c_sample,
           w_in_a, w_o_a, gnorm_a, lb_a, w_kv, w_ada_kv, b_ada_kv, kv_norm_w, w_q_b, w_o_b,
           sinks_b, rel_bias, norm_w, w_ada, b_ada, w_ffn_in, w_ffn_out, final_norm_w):
    n_p, n_s = c_prompt.shape[0], c_sample.shape[0]
    rows = -(-(n_p + n_s) // 8) * 8
    c_all = jnp.concatenate(
        [c_prompt, c_sample, jnp.zeros((rows - n_p - n_s, D_MODEL), F32)], axis=0)

    zeros, ones = jnp.zeros((DEPTH, D_MODEL), F32), jnp.ones((DEPTH, D_MODEL), F32)
    p_ada = jnp.stack([zeros, norm_w[:, 0], zeros, zeros, norm_w[:, 1], zeros], axis=1)
    q_ada = jnp.stack([ones, norm_w[:, 0], ones, ones, norm_w[:, 1], ones], axis=1)
    ada = ada_project(c_all, w_ada, b_ada.reshape(DEPTH, 6, 1, D_MODEL),
                      p_ada.reshape(DEPTH, 6, 1, D_MODEL), q_ada.reshape(DEPTH, 6, 1, D_MODEL))
    p_kv = jnp.stack([zeros[0], kv_norm_w]).reshape(1, 2, 1, D_MODEL)
    q_kv = jnp.stack([ones[0], kv_norm_w]).reshape(1, 2, 1, D_MODEL)
    ada_kv = ada_project(c_all, w_ada_kv.reshape(1, D_MODEL, 2 * D_MODEL),
                         b_ada_kv.reshape(1, 2, 1, D_MODEL), p_kv, q_kv)

    def split_mods(arr, n_chunks, lo, hi, per_token):
        out = []
        for c in range(n_chunks):
            v = arr[lo:hi, c * D_MODEL:(c + 1) * D_MODEL]
            out.append(v if per_token else v.reshape(hi - lo, 1, D_MODEL))
        return out

    def mods_for(lo, hi, per_token):
        layers = []
        for l in range(DEPTH):
            sh1, a1, g1, sh2, a2, g2 = split_mods(ada[l], 6, lo, hi, per_token)
            layers.append((a1, sh1, g1, a2, sh2, g2))
        sh_kv, a_kv = split_mods(ada_kv[0], 2, lo, hi, per_token)
        return layers, (a_kv, sh_kv)

    lb_sm = jax.nn.softmax(lb_a.astype(F32), axis=0)
    lbs = jnp.cumsum(lb_sm, axis=0) - lb_sm[0:1]

    w_k = w_kv[:, :KV_WIDTH].reshape(D_MODEL, B_KV_HEADS, 1, B_HEAD_DIM)
    w_v = w_kv[:, KV_WIDTH:].reshape(D_MODEL, B_KV_HEADS, 1, B_HEAD_DIM)
    rep = lambda w: jnp.tile(w, (1, 1, B_GROUPS, 1)).reshape(D_MODEL, D_MODEL)
    w_kv_rep = jnp.concatenate([rep(w_k), rep(w_v)], axis=1)[None].astype(BF16)

    rb = rel_bias.astype(F32)[_T5_BUCKETS]
    dist = np.arange(WINDOW)[:, None] + WINDOW - np.arange(2 * WINDOW)[None, :]
    onehot = (jnp.asarray(np.clip(dist, 0, WINDOW - 1))[:, :, None]
              == jnp.arange(WINDOW)[None, None, :]).astype(F32)
    tab = jnp.einsum("tsd,dh->hts", onehot, rb, precision=lax.Precision.HIGHEST) * LOG2E
    in_band = (dist >= 0) & (dist < WINDOW)
    has_prev = np.arange(2 * WINDOW)[None, :] >= WINDOW
    bias_tab = jnp.stack([jnp.where(in_band & has_prev, tab, MASK_VALUE),
                          jnp.where(in_band, tab, MASK_VALUE)])
    bias_row = rb[::-1].T * LOG2E

    bf = lambda w: w.astype(BF16)
    wts = (bf(w_in_a), bf(w_o_a), gnorm_a, bf(w_q_b), bf(w_o_b), sinks_b, bf(w_ffn_in),
           bf(w_ffn_out), final_norm_w, bf(w_kv)[None], w_kv_rep)
    mods_p, kv_mod_p = mods_for(0, n_p, False)
    mods_s, kv_mod_s = mods_for(n_p, n_p + n_s, True)
    y_p, st_p, k_p, v_p = _trunk(x_prompt, mods_p, kv_mod_p, False, None, None, None, wts, lbs,
                                 bias_tab, bias_row)
    y_s, st_s, k_s, v_s = _trunk(x_sample, mods_s, kv_mod_s, True, state_hgrn, cache_swa_k,
                                 cache_swa_v, wts, lbs, bias_tab, bias_row)
    return (y_p, y_s, st_p, st_s, k_p, v_p, k_s, v_s)
```

```python
import functools
import math

import numpy as np
import jax
import jax.numpy as jnp
from jax import lax
from jax.experimental import pallas as pl
from jax.experimental.pallas import tpu as pltpu

F32 = jnp.float32
BF16 = jnp.bfloat16

D_MODEL = 1024
DEPTH = 4
N_A_LAYERS = 2
A_HEADS = 8
A_DK = 128
A_DV = 128
F_MIN = 1e-30
B_HEAD_DIM = 64
B_HEADS = 16
B_KV_HEADS = 4
B_GROUPS = 4
WINDOW = 128
MASK_VALUE = -1e30
N_BUCKETS = 32
MAX_DISTANCE = 128
D_FF = 2816
EPS = 1e-6
GROUP_LANES = B_GROUPS * B_HEAD_DIM
KV_WIDTH = B_KV_HEADS * B_HEAD_DIM
HEAD_DIM_SHIFT = B_HEAD_DIM.bit_length() - 1
GROUPS_SHIFT = B_GROUPS.bit_length() - 1

CHUNK = 128
N_LEVELS = 7
ROW_TILE = 512
COL_TILE = 1024
FF_TILE = 256
STEP_TOKENS = 4
ATTN_TOKENS = 8
VMEM_LIMIT_BYTES = 48 * 1024 * 1024


def _params(sem):
    return pltpu.CompilerParams(dimension_semantics=sem, vmem_limit_bytes=VMEM_LIMIT_BYTES)


def _row_tile(m, per_token, rows_per_batch):
    return min(m, ROW_TILE) if per_token else min(m, ROW_TILE, rows_per_batch)


def _sigmoid(x):
    return 0.5 * jnp.tanh(0.5 * x) + 0.5


def _silu(x):
    h = 0.5 * x
    return h * jnp.tanh(h) + h


def _norm_mod(x, a, s):
    y = x * lax.rsqrt(jnp.mean(x * x, axis=-1, keepdims=True) + EPS)
    return (y * a + s).astype(BF16)


def _resident(shape, layer):
    return pl.BlockSpec((None,) + shape, lambda i: (layer,) + (0,) * len(shape),
                        pipeline_mode=pl.Buffered(1))


def _row_operand(width, tm, per_token, rows_per_batch):
    if per_token:
        return pl.BlockSpec((tm, width), lambda i: (i, 0))
    return pl.BlockSpec((1, 1, width), lambda i: ((i * tm) // rows_per_batch, 0, 0))


def _row_value(ref, per_token):
    return ref[...] if per_token else ref[0]


def _norm_proj_body(*refs, per_token, epilogue, out_scale):
    x_ref, a_ref, s_ref, w_ref = refs[:4]
    xn = _norm_mod(x_ref[...], _row_value(a_ref, per_token), _row_value(s_ref, per_token))

    tn = min(COL_TILE, w_ref.shape[1])

    def proj(c):
        return jnp.dot(xn, w_ref[:, c:c + tn], preferred_element_type=F32)

    if epilogue == "plain":
        (o_ref,) = refs[4:]
        for c in range(0, o_ref.shape[1], tn):
            acc = proj(c)
            if out_scale is not None:
                acc = acc * out_scale
            o_ref[:, c:c + tn] = acc.astype(o_ref.dtype)
    elif epilogue == "kv_rep":
        (o_ref,) = refs[4:]
        acc = proj(0)
        low_half = lax.broadcasted_iota(jnp.int32, (acc.shape[0], 128), 1) < B_HEAD_DIM
        for c in range(acc.shape[1] // 128):
            x = acc[:, c * 128:(c + 1) * 128]
            swapped = pltpu.roll(x, B_HEAD_DIM, 1)
            for half, rep in enumerate((jnp.where(low_half, x, swapped),
                                        jnp.where(low_half, swapped, x))):
                rep = rep.astype(o_ref.dtype)
                base = (2 * c + half) * GROUP_LANES
                o_ref[:, base:base + 128] = rep
                o_ref[:, base + 128:base + 256] = rep
    elif epilogue == "hgrn":
        lb_ref, o_ref, lf_ref = refs[4:]
        d = D_MODEL
        lb = lb_ref[...]
        o_ref[:, 0:d] = _silu(proj(0)).astype(o_ref.dtype)
        f = lb + (1.0 - lb) * _sigmoid(proj(d))
        o_ref[:, d:2 * d] = (1.0 - f).astype(o_ref.dtype)
        lf_ref[...] = jnp.log2(jnp.maximum(f, F_MIN))
        o_ref[:, 2 * d:3 * d] = proj(2 * d).astype(o_ref.dtype)
        o_ref[:, 3 * d:4 * d] = _silu(proj(3 * d)).astype(o_ref.dtype)
    else:
        raise ValueError(epilogue)


def norm_proj(x, mod, w, layer, *, per_token, rows_per_batch, epilogue="plain", lb=None,
              out_dtype=BF16, out_scale=None):
    m, k = x.shape
    n = w.shape[-1]
    tm = _row_tile(m, per_token, rows_per_batch)
    assert m % tm == 0 and (per_token or rows_per_batch % tm == 0)
    row = functools.partial(_row_operand, tm=tm, per_token=per_token, rows_per_batch=rows_per_batch)
    in_specs = [pl.BlockSpec((tm, k), lambda i: (i, 0)), row(k), row(k), _resident((k, n), layer)]
    args = [x, mod[0], mod[1], w]
    n_out = n * B_GROUPS if epilogue == "kv_rep" else n
    out_specs = [pl.BlockSpec((tm, n_out), lambda i: (i, 0))]
    out_shape = [jax.ShapeDtypeStruct((m, n_out), out_dtype)]
    if epilogue == "hgrn":
        in_specs.append(pl.BlockSpec((1, D_MODEL), lambda i: (0, 0)))
        args.append(lb)
        out_specs.append(pl.BlockSpec((tm, D_MODEL), lambda i: (i, 0)))
        out_shape.append(jax.ShapeDtypeStruct((m, D_MODEL), F32))
    outs = pl.pallas_call(
        functools.partial(_norm_proj_body, per_token=per_token, epilogue=epilogue,
                          out_scale=out_scale),
        grid=(m // tm,),
        in_specs=in_specs,
        out_specs=out_specs,
        out_shape=out_shape,
        compiler_params=_params(("parallel",)),
        name=f"norm_proj_{epilogue}_m{m}_n{n}",
    )(*args)
    return outs if epilogue == "hgrn" else outs[0]


def _post_ffn_body(*refs, per_token, final_norm):
    (mix_ref, h_ref, g1_ref, a2_ref, s2_ref, g2_ref, wo_ref, win_ref, wout_ref) = refs[:9]
    rest = refs[9:]
    fw_ref = rest[0] if final_norm else None
    o_ref, act_ref = rest[-2:]
    row = functools.partial(_row_value, per_token=per_token)
    h_mid = h_ref[...] + row(g1_ref) * jnp.dot(mix_ref[...].astype(BF16), wo_ref[...],
                                               preferred_element_type=F32)
    xn = _norm_mod(h_mid, row(a2_ref), row(s2_ref))
    for c in range(0, D_FF, FF_TILE):
        gate = jnp.dot(xn, win_ref[:, c:c + FF_TILE], preferred_element_type=F32)
        up = jnp.dot(xn, win_ref[:, D_FF + c:D_FF + c + FF_TILE], preferred_element_type=F32)
        act_ref[:, c:c + FF_TILE] = (_silu(gate) * up).astype(BF16)
    out = h_mid + row(g2_ref) * jnp.dot(act_ref[...], wout_ref[...], preferred_element_type=F32)
    if final_norm:
        out = out * lax.rsqrt(jnp.mean(out * out, axis=-1, keepdims=True) + EPS) * fw_ref[...]
    o_ref[...] = out


def post_ffn(mix, h, g1, mod2, g2, w_o, lo, w_ffn_in, w_ffn_out, lf, *, per_token,
             rows_per_batch, final_w=None):
    m, d = h.shape
    tm = _row_tile(m, per_token, rows_per_batch)
    assert m % tm == 0 and (per_token or rows_per_batch % tm == 0)
    row = functools.partial(_row_operand, d, tm, per_token, rows_per_batch)
    tile = pl.BlockSpec((tm, d), lambda i: (i, 0))
    in_specs = [tile, tile, row(), row(), row(), row(),
                _resident((d, d), lo), _resident((d, 2 * D_FF), lf), _resident((D_FF, d), lf)]
    args = [mix, h, g1, mod2[0], mod2[1], g2, w_o, w_ffn_in, w_ffn_out]
    if final_w is not None:
        in_specs.append(pl.BlockSpec((1, d), lambda i: (0, 0)))
        args.append(final_w.reshape(1, d))
    return pl.pallas_call(
        functools.partial(_post_ffn_body, per_token=per_token, final_norm=final_w is not None),
        grid=(m // tm,),
        in_specs=in_specs,
        out_specs=tile,
        out_shape=jax.ShapeDtypeStruct((m, d), F32),
        scratch_shapes=[pltpu.VMEM((tm, D_FF), BF16)],
        compiler_params=_params(("parallel",)),
        name=f"post_ffn_m{m}",
    )(*args)


def _ada_body(c_ref, w_ref, b_ref, p_ref, q_ref, o_ref):
    c = _silu(c_ref[...]).astype(BF16)
    acc = jnp.dot(c, w_ref[...].astype(BF16), preferred_element_type=F32)
    o_ref[...] = p_ref[...] + q_ref[...] * (acc + b_ref[...])


def ada_project(c_all, w, b, p, q):
    n_l, _, n_cols = w.shape
    n_c = n_cols // D_MODEL
    rows = c_all.shape[0]
    vec_spec = pl.BlockSpec((None, None, 1, D_MODEL), lambda l, j: (l, j, 0, 0))
    return pl.pallas_call(
        _ada_body,
        grid=(n_l, n_c),
        in_specs=[
            pl.BlockSpec((rows, D_MODEL), lambda l, j: (0, 0)),
            pl.BlockSpec((None, D_MODEL, D_MODEL), lambda l, j: (l, 0, j)),
            vec_spec, vec_spec, vec_spec,
        ],
        out_specs=pl.BlockSpec((None, rows, D_MODEL), lambda l, j: (l, 0, j)),
        out_shape=jax.ShapeDtypeStruct((n_l, rows, n_cols), F32),
        compiler_params=_params(("parallel", "parallel")),
        name="ada_project",
    )(c_all, w, b, p, q)


MATMUL_LEVELS = (5, 6)


def _scan_tables():
    c = CHUNK
    t = np.arange(c)[:, None]
    u = np.arange(c)[None, :]
    blocks = [(u <= t)]
    for level in MATMUL_LEVELS:
        p = N_LEVELS - level
        odd = ((t >> p) & 1) == 1
        start = (t >> p) << p
        end = (((t >> p) + 1) << p) - 1
        blocks.append(np.where(odd, (u >= start) & (u <= t), (u > t) & (u <= end)))
    sums = np.concatenate(blocks, axis=0).astype(np.float32)
    x = np.arange(c)[:, None] ^ np.arange(c)[None, :]
    msb = np.floor(np.log2(np.maximum(x, 1))).astype(np.int32)
    lvl = np.where(x == 0, 0, N_LEVELS - msb)
    lvl = np.where(np.arange(c)[None, :] > np.arange(c)[:, None], -1, lvl).astype(np.int32)
    return sums, lvl


_SCAN_SUMS, _SCAN_LEVELS = _scan_tables()

_TN = (((1,), (1,)), ((), ()))
_TM = (((0,), (0,)), ((), ()))


def _level_exponent(ex_ref, log2_f, odd_rows, level):
    m = CHUNK >> level
    if level in MATMUL_LEVELS:
        i = 1 + MATMUL_LEVELS.index(level)
        return ex_ref[i * CHUNK:(i + 1) * CHUNK, :]
    if m == 1:
        return jnp.where(odd_rows, log2_f, 0.0)
    parts = []
    for i in range(CHUNK // m):
        blk = ex_ref[i * m:(i + 1) * m, :]
        if i & 1:
            parts.append(blk - ex_ref[i * m - 1:i * m, :])
        else:
            parts.append(ex_ref[(i + 1) * m - 1:(i + 1) * m, :] - blk)
    return jnp.concatenate(parts, axis=0)


def _level_mix(q, kk, odd_rows, level):
    m = CHUNK >> level
    if m >= 8:
        parts = [(q if (i & 1) else kk)[i * m:(i + 1) * m] for i in range(CHUNK // m)]
        return jnp.concatenate(parts, axis=0)
    return jnp.where(odd_rows, q, kk)


def _hgrn_scan_body(q_ref, k_ref, v_ref, g_ref, lf_ref, gn_ref, sums_ref, lvl_ref,
                    o_ref, st_ref, state_ref, ex_ref):
    n = pl.program_id(1)

    @pl.when(n == 0)
    def _():
        state_ref[...] = jnp.zeros_like(state_ref)

    log2_f = lf_ref[...]
    hi = log2_f.astype(BF16)
    rem = log2_f - hi.astype(F32)
    mid = rem.astype(BF16)
    lo = (rem - mid.astype(F32)).astype(BF16)
    ex_ref[...] = jnp.dot(sums_ref[...], jnp.concatenate([hi, mid, lo], axis=0),
                          preferred_element_type=F32)

    q_bf, k_bf, v = q_ref[...], k_ref[...], v_ref[...]
    q, kk = q_bf.astype(F32), k_bf.astype(F32)
    b_incl = ex_ref[0:CHUNK, :]
    b_end = ex_ref[CHUNK - 1:CHUNK, :]
    q_dec = (q * jnp.exp2(b_incl)).astype(BF16)
    k_dec = (kk * jnp.exp2(b_end - b_incl)).astype(BF16)
    d_end = jnp.exp2(b_end)
    heads = [slice(h * A_DK, (h + 1) * A_DK) for h in range(A_HEADS)]

    lvl = lvl_ref[...]
    row = lax.broadcasted_iota(jnp.int32, (CHUNK, D_MODEL), 0)
    atts = [jnp.where(lvl == 0, lax.dot_general(q_bf[:, sl], k_bf[:, sl], _TN,
                                                preferred_element_type=F32), 0.0)
            for sl in heads]
    for level in range(1, N_LEVELS + 1):
        odd_rows = ((row >> (N_LEVELS - level)) & 1) == 1
        e = jnp.exp2(_level_exponent(ex_ref, log2_f, odd_rows, level))
        mix = (_level_mix(q, kk, odd_rows, level) * e).astype(BF16)
        for h, sl in enumerate(heads):
            a = lax.dot_general(mix[:, sl], mix[:, sl], _TN, preferred_element_type=F32)
            atts[h] = jnp.where(lvl == level, a, atts[h])

    gn = gn_ref[...]
    gate = g_ref[...].astype(F32)
    for h, sl in enumerate(heads):
        s_t = state_ref[h]
        o = (lax.dot_general(q_dec[:, sl], s_t.astype(BF16), _TN, preferred_element_type=F32)
             + jnp.dot(atts[h].astype(BF16), v[:, sl], preferred_element_type=F32))
        state_ref[h] = d_end[:, sl] * s_t + lax.dot_general(
            v[:, sl], k_dec[:, sl], _TM, preferred_element_type=F32)
        y = o * lax.rsqrt(jnp.mean(o * o, axis=-1, keepdims=True) + EPS)
        o_ref[:, sl] = (y * gn * gate[:, sl]).astype(o_ref.dtype)

    @pl.when(n == pl.num_programs(1) - 1)
    def _():
        for h in range(A_HEADS):
            st_ref[0, h] = state_ref[h].T


def hgrn_scan_prompt(qkvg, lf, gn, bsz, t):
    nc = t // CHUNK
    sec = lambda k: pl.BlockSpec((CHUNK, D_MODEL), lambda b, n, k=k: (b * nc + n, k))
    n_sum = _SCAN_SUMS.shape[0]
    return pl.pallas_call(
        _hgrn_scan_body,
        grid=(bsz, nc),
        in_specs=[sec(0), sec(1), sec(2), sec(3), sec(0),
                  pl.BlockSpec((1, A_DV), lambda b, n: (0, 0)),
                  pl.BlockSpec((n_sum, 3 * CHUNK), lambda b, n: (0, 0)),
                  pl.BlockSpec((CHUNK, CHUNK), lambda b, n: (0, 0))],
        out_specs=[pl.BlockSpec((CHUNK, D_MODEL), lambda b, n: (b * nc + n, 0)),
                   pl.BlockSpec((1, A_HEADS, A_DK, A_DV), lambda b, n: (b, 0, 0, 0))],
        out_shape=[jax.ShapeDtypeStruct((bsz * t, D_MODEL), BF16),
                   jax.ShapeDtypeStruct((bsz, A_HEADS, A_DK, A_DV), F32)],
        scratch_shapes=[pltpu.VMEM((A_HEADS, A_DV, A_DK), F32),
                        pltpu.VMEM((n_sum, D_MODEL), F32)],
        compiler_params=_params(("parallel", "arbitrary")),
        name="hgrn_scan",
    )(qkvg, qkvg, qkvg, qkvg, lf, gn, jnp.asarray(np.tile(_SCAN_SUMS, (1, 3)), BF16),
      jnp.asarray(_SCAN_LEVELS))


STEP_ROWS = 16


def _step_columns(decay, kk, q, v_row):
    r = lax.broadcasted_iota(jnp.int32, (STEP_ROWS, A_DK), 0)
    hi = decay.astype(BF16).astype(F32)
    rem = decay - hi
    mid = rem.astype(BF16).astype(F32)
    lo = rem - mid
    lhs = jnp.where(r == 0, hi, jnp.where(r == 1, mid, jnp.where(r == 2, lo,
                                                                 jnp.where(r == 3, kk, 0.0))))
    parts = [jnp.where(r < 3, 1.0, 0.0), jnp.where(r == 3, v_row, 0.0)]
    if q is not None:
        lhs = jnp.where(r == 4, q, lhs)
        parts.append(jnp.where(r == 4, 1.0, 0.0))
    rhs = jnp.concatenate(parts, axis=1).astype(BF16)
    out = lax.dot_general(lhs.astype(BF16), rhs, _TM, preferred_element_type=F32)
    return [out[:, i * A_DV:(i + 1) * A_DV] for i in range(len(parts))]


def _hgrn_step_body(*refs, n_replay, write_state):
    q_ref, k_ref, v_ref, g_ref, lf_ref, gn_ref, s_ref = refs[:7]
    replay = [refs[7 + 4 * i:11 + 4 * i] for i in range(n_replay)]
    outs = refs[7 + 4 * n_replay:]
    o_ref = outs[0]
    gn = gn_ref[...]
    for b in range(STEP_TOKENS):
        decay = jnp.exp2(lf_ref[b])
        q, kk, v, gate = q_ref[b], k_ref[b], v_ref[b], g_ref[b]
        for h in range(A_HEADS):
            sl = slice(h * A_DK, (h + 1) * A_DK)
            dec_m, kv_m, q_m = _step_columns(decay[:, sl], kk[:, sl], q[:, sl], v[:, sl])
            s_new = dec_m * s_ref[b, h] + kv_m
            if write_state:
                outs[1][b, n_replay, h] = s_new
            o = jnp.sum(q_m * s_new, axis=0, keepdims=True)
            y = o * lax.rsqrt(jnp.mean(o * o, axis=-1, keepdims=True) + EPS)
            o_ref[b, :, sl] = y * gn * gate[:, sl]
        for i, (kp_ref, vp_ref, lfp_ref, sp_ref) in enumerate(replay):
            decay_p = jnp.exp2(lfp_ref[b])
            for h in range(A_HEADS):
                sl = slice(h * A_DK, (h + 1) * A_DK)
                dec_m, kv_m = _step_columns(decay_p[:, sl], kp_ref[b][:, sl], None,
                                            vp_ref[b][:, sl])
                outs[1][b, i, h] = dec_m * sp_ref[b, h] + kv_m


def hgrn_step(qkvg, lf, gn, state, layer, earlier=None):
    bsz = qkvg.shape[0]
    nb = STEP_TOKENS
    assert bsz % nb == 0
    sec = lambda k: pl.BlockSpec((nb, 1, D_MODEL), lambda b, k=k: (b, 0, k))
    state_blk = lambda l: pl.BlockSpec((nb, None, A_HEADS, A_DK, A_DV),
                                       lambda b, l=l: (b, l, 0, 0, 0))
    as3 = lambda a: a.reshape(bsz, 1, a.shape[-1])
    in_specs = [sec(0), sec(1), sec(2), sec(3), sec(0),
                pl.BlockSpec((1, A_DV), lambda b: (0, 0)), state_blk(layer)]
    args = [as3(qkvg)] * 4 + [as3(lf), gn, state]
    write_state = earlier is not None
    for i, (qkvg_p, lf_p) in enumerate(earlier or ()):
        in_specs += [sec(1), sec(2), sec(0), state_blk(i)]
        args += [as3(qkvg_p), as3(qkvg_p), as3(lf_p), state]
    out_specs = [pl.BlockSpec((nb, 1, D_MODEL), lambda b: (b, 0, 0))]
    out_shape = [jax.ShapeDtypeStruct((bsz, 1, D_MODEL), F32)]
    if write_state:
        assert len(earlier) == layer
        n_l = layer + 1
        out_specs.append(pl.BlockSpec((nb, n_l, A_HEADS, A_DK, A_DV), lambda b: (b, 0, 0, 0, 0)))
        out_shape.append(jax.ShapeDtypeStruct((bsz, n_l, A_HEADS, A_DK, A_DV), F32))
    outs = pl.pallas_call(
        functools.partial(_hgrn_step_body, n_replay=len(earlier or ()), write_state=write_state),
        grid=(bsz // nb,),
        in_specs=in_specs,
        out_specs=out_specs,
        out_shape=out_shape,
        compiler_params=_params(("parallel",)),
        name="hgrn_step",
    )(*args)
    o = outs[0].reshape(bsz, D_MODEL)
    return (o, outs[1]) if write_state else (o, None)


def _t5_buckets():
    max_exact = N_BUCKETS // 2
    d = np.arange(WINDOW)
    large = max_exact + (np.log(np.maximum(d, 1).astype(np.float32) / max_exact)
                         / math.log(MAX_DISTANCE / max_exact)
                         * (N_BUCKETS - max_exact)).astype(np.int32)
    large = np.clip(large, 0, N_BUCKETS - 1)
    return np.where(d < max_exact, d, large).astype(np.int32)


_T5_BUCKETS = _t5_buckets()


LOG2E = math.log2(math.e)


def _swa_prompt_body(sink_ref, q_ref, kp_ref, kc_ref, vp_ref, vc_ref, bias_ref, o_ref):
    lane_head = lax.broadcasted_iota(jnp.int32, (WINDOW, GROUP_LANES), 1) >> HEAD_DIM_SHIFT
    for g in range(B_KV_HEADS):
        cs = slice(g * GROUP_LANES, (g + 1) * GROUP_LANES)
        qg = q_ref[:, cs].astype(F32)
        kcat = jnp.concatenate([kp_ref[:, cs], kc_ref[:, cs]], axis=0)
        vcat = jnp.concatenate([vp_ref[:, cs], vc_ref[:, cs]], axis=0)
        qstack = jnp.concatenate(
            [jnp.where(lane_head == j, qg, 0.0).astype(BF16) for j in range(B_GROUPS)], axis=0)
        logits = lax.dot_general(qstack, kcat, _TN, preferred_element_type=F32)
        acc = jnp.zeros((WINDOW, GROUP_LANES), F32)
        for j in range(B_GROUPS):
            head = g * B_GROUPS + j
            lg = logits[j * WINDOW:(j + 1) * WINDOW] + bias_ref[head]
            sink = sink_ref[head] * LOG2E
            m = jnp.maximum(jnp.max(lg, axis=-1, keepdims=True), sink)
            p = jnp.exp2(lg - m)
            denom = jnp.sum(p, axis=-1, keepdims=True) + jnp.exp2(sink - m)
            pv = jnp.dot(p.astype(BF16), vcat, preferred_element_type=F32)
            acc = jnp.where(lane_head == j, pv / denom, acc)
        o_ref[:, cs] = acc.astype(o_ref.dtype)


def swa_prompt(q, kv_rep, sinks, bias_tabs, bsz, t):
    nb = t // WINDOW
    cur = lambda c: pl.BlockSpec((WINDOW, D_MODEL), lambda b, n, c=c: (b * nb + n, c))
    prev = lambda c: pl.BlockSpec(
        (WINDOW, D_MODEL), lambda b, n, c=c: (b * nb + jnp.maximum(n - 1, 0), c))
    return pl.pallas_call(
        _swa_prompt_body,
        grid=(bsz, nb),
        in_specs=[pl.BlockSpec(memory_space=pltpu.SMEM),
                  cur(0), prev(0), cur(0), prev(1), cur(1),
                  pl.BlockSpec((None, B_HEADS, WINDOW, 2 * WINDOW),
                               lambda b, n: (jnp.minimum(n, 1), 0, 0, 0))],
        out_specs=pl.BlockSpec((WINDOW, D_MODEL), lambda b, n: (b * nb + n, 0)),
        out_shape=jax.ShapeDtypeStruct((bsz * t, D_MODEL), BF16),
        compiler_params=_params(("parallel", "arbitrary")),
        name="swa_prompt",
    )(sinks, q, kv_rep, kv_rep, kv_rep, kv_rep, bias_tabs)


def _swa_step_body(*refs, shift_cache):
    sink_ref, qz_ref, k_ref, v_ref, bias_ref, fold_ref = refs[:6]
    if shift_cache:
        kn_ref, vn_ref, o_ref, ko_ref, vo_ref = refs[6:]
        last = lax.broadcasted_iota(jnp.int32, (WINDOW, KV_WIDTH), 0) == WINDOW - 1
    else:
        (o_ref,) = refs[6:]
    sink = sink_ref[...] * LOG2E
    head_kv = lax.broadcasted_iota(jnp.int32, (B_HEADS, KV_WIDTH), 0) >> GROUPS_SHIFT
    lane_kv = lax.broadcasted_iota(jnp.int32, (B_HEADS, KV_WIDTH), 1) >> HEAD_DIM_SHIFT
    for b in range(ATTN_TOKENS):
        if shift_cache:
            k = jnp.where(last, kn_ref[b], pltpu.roll(k_ref[b], WINDOW - 1, 0))
            v = jnp.where(last, vn_ref[b], pltpu.roll(v_ref[b], WINDOW - 1, 0))
            ko_ref[b] = k
            vo_ref[b] = v
        else:
            k, v = k_ref[b], v_ref[b]
        qz = qz_ref[b].astype(BF16)
        logits = lax.dot_general(qz, k.astype(BF16), _TN, preferred_element_type=F32)
        logits = logits + bias_ref[...]
        m = jnp.maximum(jnp.max(logits, axis=-1, keepdims=True), sink)
        p = jnp.exp2(logits - m)
        denom = jnp.sum(p, axis=-1, keepdims=True) + jnp.exp2(sink - m)
        pv = jnp.dot(p.astype(BF16), v.astype(BF16), preferred_element_type=F32) / denom
        own = jnp.where(head_kv == lane_kv, pv, 0.0).astype(BF16)
        o_ref[b] = jnp.dot(own, fold_ref[...], preferred_element_type=F32)


def swa_step(qz, k_cache, v_cache, sinks, bias_row, kv_new=None):
    bsz = qz.shape[0]
    nt = ATTN_TOKENS
    assert bsz % nt == 0
    fold = np.zeros((KV_WIDTH, B_HEAD_DIM), np.float32)
    fold[np.arange(KV_WIDTH), np.arange(KV_WIDTH) % B_HEAD_DIM] = 1.0
    cache_blk = pl.BlockSpec((nt, WINDOW, KV_WIDTH), lambda b: (b, 0, 0))
    in_specs = [pl.BlockSpec((B_HEADS, 1), lambda b: (0, 0)),
                pl.BlockSpec((nt, B_HEADS, KV_WIDTH), lambda b: (b, 0, 0)),
                cache_blk, cache_blk,
                pl.BlockSpec((B_HEADS, WINDOW), lambda b: (0, 0)),
                pl.BlockSpec((KV_WIDTH, B_HEAD_DIM), lambda b: (0, 0))]
    args = [sinks.reshape(B_HEADS, 1), qz, k_cache, v_cache, bias_row, jnp.asarray(fold, BF16)]
    out_specs = [pl.BlockSpec((nt, B_HEADS, B_HEAD_DIM), lambda b: (b, 0, 0))]
    out_shape = [jax.ShapeDtypeStruct((bsz, B_HEADS, B_HEAD_DIM), F32)]
    if kv_new is not None:
        row_blk = pl.BlockSpec((nt, 1, KV_WIDTH), lambda b: (b, 0, 0))
        in_specs += [row_blk, row_blk]
        args += list(kv_new)
        out_specs += [cache_blk, cache_blk]
        out_shape += [jax.ShapeDtypeStruct(k_cache.shape, F32)] * 2
    outs = pl.pallas_call(
        functools.partial(_swa_step_body, shift_cache=kv_new is not None),
        grid=(bsz // nt,),
        in_specs=in_specs,
        out_specs=out_specs,
        out_shape=out_shape,
        compiler_params=_params(("parallel",)),
        name="swa_step",
    )(*args)
    return outs if kv_new is not None else outs[0]


def _trunk(x, mods, kv_mod, per_token, hgrn_state0, k_buf, v_buf, wts, lbs, bias_tab, bias_row):
    (w_in_a, w_o_a, gnorm_a, w_q_b, w_o_b, sinks_b, w_ffn_in, w_ffn_out, final_norm_w,
     w_kv) = wts
    prompt = k_buf is None
    bsz, t, _ = x.shape
    m = bsz * t
    common = dict(per_token=per_token, rows_per_batch=t)
    h = x.reshape(m, D_MODEL)
    states = []
    step_inputs = []
    new_state = None
    kv = k_state = v_state = kv_new = None
    for l in range(DEPTH):
        a1, s1, g1, a2, s2, g2 = mods[l]
        if l < N_A_LAYERS:
            qkvg, lf = norm_proj(h, (a1, s1), w_in_a, l, epilogue="hgrn",
                                 lb=lbs[l].reshape(1, D_MODEL), **common)
            gn = gnorm_a[l].reshape(1, A_DV)
            if prompt:
                mix, s_new = hgrn_scan_prompt(qkvg, lf, gn, bsz, t)
                states.append(s_new)
            else:
                qkvg = qkvg.astype(F32)
                last_a = l == N_A_LAYERS - 1
                mix, new_state = hgrn_step(qkvg, lf, gn, hgrn_state0, l,
                                           earlier=list(step_inputs) if last_a else None)
                step_inputs.append((qkvg, lf))
            w_o, lo = w_o_a, l
        else:
            j = l - N_A_LAYERS
            scale = LOG2E / math.sqrt(B_HEAD_DIM)
            if prompt:
                q = norm_proj(h, (a1, s1), w_q_b, j, out_scale=scale, **common)
                mix = swa_prompt(q, kv, sinks_b[j], bias_tab, bsz, t)
            else:
                q = norm_proj(h, (a1, s1), w_q_b, j, out_scale=scale, out_dtype=F32, **common)
                lane_kv = (np.arange(KV_WIDTH) // B_HEAD_DIM)[None, :]
                head_kv = (np.arange(B_HEADS) // B_GROUPS)[:, None]
                qz = jnp.where((lane_kv == head_kv)[None],
                               jnp.tile(q.reshape(m, B_HEADS, B_HEAD_DIM), (1, 1, B_KV_HEADS)), 0.0)
                if j == 0:
                    mix, k_state, v_state = swa_step(qz, k_buf.reshape(m, WINDOW, KV_WIDTH),
                                                     v_buf.reshape(m, WINDOW, KV_WIDTH),
                                                     sinks_b[j], bias_row, kv_new)
                else:
                    mix = swa_step(qz, k_state, v_state, sinks_b[j], bias_row)
                mix = mix.reshape(m, D_MODEL)
            w_o, lo = w_o_b, j
        h = post_ffn(mix, h, g1, (a2, s2), g2, w_o, lo, w_ffn_in, w_ffn_out, l,
                     final_w=final_norm_w if l == DEPTH - 1 else None, **common)
        if l == N_A_LAYERS - 1:
            if prompt:
                kv = norm_proj(h, kv_mod, w_kv, 0, epilogue="kv_rep", **common)
                tail = h.reshape(bsz, t, D_MODEL)[:, -WINDOW:].reshape(bsz * WINDOW, D_MODEL)
                kv_tail = norm_proj(tail, kv_mod, w_kv, 0, per_token=False,
                                    rows_per_batch=WINDOW, out_dtype=F32)
                kv_tail = kv_tail.reshape(bsz, WINDOW, 2, B_KV_HEADS, B_HEAD_DIM)
                k_state, v_state = kv_tail[:, :, 0], kv_tail[:, :, 1]
            else:
                kv_row = norm_proj(h, kv_mod, w_kv, 0, out_dtype=F32, **common)
                kv_new = (kv_row[:, :KV_WIDTH].reshape(m, 1, KV_WIDTH),
                          kv_row[:, KV_WIDTH:].reshape(m, 1, KV_WIDTH))
    y = h.reshape(bsz, t, D_MODEL)
    if prompt:
        return y, jnp.stack(states, axis=1), k_state, v_state
    cache_shape = (m, WINDOW, B_KV_HEADS, B_HEAD_DIM)
    return y, new_state, k_state.reshape(cache_shape), v_state.reshape(cache_shape)


def kernel(x_prompt, x_sample, state_hgrn, cache_swa_k, cache_swa_v, c_prompt, c_sample,
           w_in_a, w_o_a, gnorm_a, lb_a, w_kv, w_ada_kv, b_ada_kv, kv_norm_w, w_q_b, w_o_b,
           sinks_b, rel_bias, norm_w, w_ada, b_ada, w_ffn_in, w_ffn_out, final_norm_w):
    n_p, n_s = c_prompt.shape[0], c_sample.shape[0]
    rows = -(-(n_p + n_s) // 8) * 8
    c_all = jnp.concatenate(
        [c_prompt, c_sample, jnp.zeros((rows - n_p - n_s, D_MODEL), F32)], axis=0)

    zeros, ones = jnp.zeros((DEPTH, D_MODEL), F32), jnp.ones((DEPTH, D_MODEL), F32)
    p_ada = jnp.stack([zeros, norm_w[:, 0], zeros, zeros, norm_w[:, 1], zeros], axis=1)
    q_ada = jnp.stack([ones, norm_w[:, 0], ones, ones, norm_w[:, 1], ones], axis=1)
    ada = ada_project(c_all, w_ada, b_ada.reshape(DEPTH, 6, 1, D_MODEL),
                      p_ada.reshape(DEPTH, 6, 1, D_MODEL), q_ada.reshape(DEPTH, 6, 1, D_MODEL))
    p_kv = jnp.stack([zeros[0], kv_norm_w]).reshape(1, 2, 1, D_MODEL)
    q_kv = jnp.stack([ones[0], kv_norm_w]).reshape(1, 2, 1, D_MODEL)
    ada_kv = ada_project(c_all, w_ada_kv.reshape(1, D_MODEL, 2 * D_MODEL),
                         b_ada_kv.reshape(1, 2, 1, D_MODEL), p_kv, q_kv)

    def split_mods(arr, n_chunks, lo, hi, per_token):
        out = []
        for c in range(n_chunks):
            v = arr[lo:hi, c * D_MODEL:(c + 1) * D_MODEL]
            out.append(v if per_token else v.reshape(hi - lo, 1, D_MODEL))
        return out

    def mods_for(lo, hi, per_token):
        layers = []
        for l in range(DEPTH):
            sh1, a1, g1, sh2, a2, g2 = split_mods(ada[l], 6, lo, hi, per_token)
            layers.append((a1, sh1, g1, a2, sh2, g2))
        sh_kv, a_kv = split_mods(ada_kv[0], 2, lo, hi, per_token)
        return layers, (a_kv, sh_kv)

    lb_sm = jax.nn.softmax(lb_a.astype(F32), axis=0)
    lbs = jnp.cumsum(lb_sm, axis=0) - lb_sm[0:1]

    rb = rel_bias.astype(F32)[_T5_BUCKETS]
    dist = np.arange(WINDOW)[:, None] + WINDOW - np.arange(2 * WINDOW)[None, :]
    onehot = (jnp.asarray(np.clip(dist, 0, WINDOW - 1))[:, :, None]
              == jnp.arange(WINDOW)[None, None, :]).astype(F32)
    tab = jnp.einsum("tsd,dh->hts", onehot, rb, precision=lax.Precision.HIGHEST) * LOG2E
    in_band = (dist >= 0) & (dist < WINDOW)
    has_prev = np.arange(2 * WINDOW)[None, :] >= WINDOW
    bias_tab = jnp.stack([jnp.where(in_band & has_prev, tab, MASK_VALUE),
                          jnp.where(in_band, tab, MASK_VALUE)])
    bias_row = rb[::-1].T * LOG2E

    bf = lambda w: w.astype(BF16)
    wts = (bf(w_in_a), bf(w_o_a), gnorm_a, bf(w_q_b), bf(w_o_b), sinks_b, bf(w_ffn_in),
           bf(w_ffn_out), final_norm_w, bf(w_kv)[None])
    mods_p, kv_mod_p = mods_for(0, n_p, False)
    mods_s, kv_mod_s = mods_for(n_p, n_p + n_s, True)
    y_p, st_p, k_p, v_p = _trunk(x_prompt, mods_p, kv_mod_p, False, None, None, None, wts, lbs,
                                 bias_tab, bias_row)
    y_s, st_s, k_s, v_s = _trunk(x_sample, mods_s, kv_mod_s, True, state_hgrn, cache_swa_k,
                                 cache_swa_v, wts, lbs, bias_tab, bias_row)
    return (y_p, y_s, st_p, st_s, k_p, v_p, k_s, v_s)
```

```python
import functools
import math

import numpy as np
import jax
import jax.numpy as jnp
from jax import lax
from jax.experimental import pallas as pl
from jax.experimental.pallas import tpu as pltpu

F32 = jnp.float32
BF16 = jnp.bfloat16

D_MODEL = 1024
DEPTH = 4
N_A_LAYERS = 2
A_HEADS = 8
A_DK = 128
A_DV = 128
F_MIN = 1e-30
B_HEAD_DIM = 64
B_HEADS = 16
B_KV_HEADS = 4
B_GROUPS = 4
WINDOW = 128
MASK_VALUE = -1e30
N_BUCKETS = 32
MAX_DISTANCE = 128
D_FF = 2816
EPS = 1e-6
GROUP_LANES = B_GROUPS * B_HEAD_DIM
KV_WIDTH = B_KV_HEADS * B_HEAD_DIM
HEAD_DIM_SHIFT = B_HEAD_DIM.bit_length() - 1
GROUPS_SHIFT = B_GROUPS.bit_length() - 1

CHUNK = 128
N_LEVELS = 7
ROW_TILE = 512
COL_TILE = 1024
FF_TILE = 256
STEP_TOKENS = 4
ATTN_TOKENS = 8
VMEM_LIMIT_BYTES = 48 * 1024 * 1024


def _params(sem):
    return pltpu.CompilerParams(dimension_semantics=sem, vmem_limit_bytes=VMEM_LIMIT_BYTES)


def _row_tile(m, per_token, rows_per_batch):
    return min(m, ROW_TILE) if per_token else min(m, ROW_TILE, rows_per_batch)


def _sigmoid(x):
    return 0.5 * jnp.tanh(0.5 * x) + 0.5


def _silu(x):
    h = 0.5 * x
    return h * jnp.tanh(h) + h


def _norm_mod(x, a, s):
    y = x * lax.rsqrt(jnp.mean(x * x, axis=-1, keepdims=True) + EPS)
    return (y * a + s).astype(BF16)


def _resident(shape, layer):
    return pl.BlockSpec((None,) + shape, lambda i: (layer,) + (0,) * len(shape),
                        pipeline_mode=pl.Buffered(1))


def _row_operand(width, tm, per_token, rows_per_batch):
    if per_token:
        return pl.BlockSpec((tm, width), lambda i: (i, 0))
    return pl.BlockSpec((1, 1, width), lambda i: ((i * tm) // rows_per_batch, 0, 0))


def _row_value(ref, per_token):
    return ref[...] if per_token else ref[0]


def _norm_proj_body(*refs, per_token, epilogue, out_scale):
    x_ref, a_ref, s_ref, w_ref = refs[:4]
    xn = _norm_mod(x_ref[...], _row_value(a_ref, per_token), _row_value(s_ref, per_token))

    tn = min(COL_TILE, w_ref.shape[1])

    def proj(c):
        return jnp.dot(xn, w_ref[:, c:c + tn], preferred_element_type=F32)

    if epilogue == "plain":
        (o_ref,) = refs[4:]
        for c in range(0, o_ref.shape[1], tn):
            acc = proj(c)
            if out_scale is not None:
                acc = acc * out_scale
            o_ref[:, c:c + tn] = acc.astype(o_ref.dtype)
    elif epilogue == "kv_rep":
        (o_ref,) = refs[4:]
        acc = proj(0)
        low_half = lax.broadcasted_iota(jnp.int32, (acc.shape[0], 128), 1) < B_HEAD_DIM
        for c in range(acc.shape[1] // 128):
            x = acc[:, c * 128:(c + 1) * 128]
            swapped = pltpu.roll(x, B_HEAD_DIM, 1)
            for half, rep in enumerate((jnp.where(low_half, x, swapped),
                                        jnp.where(low_half, swapped, x))):
                rep = rep.astype(o_ref.dtype)
                base = (2 * c + half) * GROUP_LANES
                o_ref[:, base:base + 128] = rep
                o_ref[:, base + 128:base + 256] = rep
    elif epilogue == "hgrn":
        lb_ref, o_ref, lf_ref = refs[4:]
        d = D_MODEL
        lb = lb_ref[...]
        o_ref[:, 0:d] = _silu(proj(0)).astype(o_ref.dtype)
        f = lb + (1.0 - lb) * _sigmoid(proj(d))
        o_ref[:, d:2 * d] = (1.0 - f).astype(o_ref.dtype)
        lf_ref[...] = jnp.log2(jnp.maximum(f, F_MIN))
        o_ref[:, 2 * d:3 * d] = proj(2 * d).astype(o_ref.dtype)
        o_ref[:, 3 * d:4 * d] = _silu(proj(3 * d)).astype(o_ref.dtype)
    else:
        raise ValueError(epilogue)


def norm_proj(x, mod, w, layer, *, per_token, rows_per_batch, epilogue="plain", lb=None,
              out_dtype=BF16, out_scale=None):
    m, k = x.shape
    n = w.shape[-1]
    tm = _row_tile(m, per_token, rows_per_batch)
    assert m % tm == 0 and (per_token or rows_per_batch % tm == 0)
    row = functools.partial(_row_operand, tm=tm, per_token=per_token, rows_per_batch=rows_per_batch)
    in_specs = [pl.BlockSpec((tm, k), lambda i: (i, 0)), row(k), row(k), _resident((k, n), layer)]
    args = [x, mod[0], mod[1], w]
    n_out = n * B_GROUPS if epilogue == "kv_rep" else n
    out_specs = [pl.BlockSpec((tm, n_out), lambda i: (i, 0))]
    out_shape = [jax.ShapeDtypeStruct((m, n_out), out_dtype)]
    if epilogue == "hgrn":
        in_specs.append(pl.BlockSpec((1, D_MODEL), lambda i: (0, 0)))
        args.append(lb)
        out_specs.append(pl.BlockSpec((tm, D_MODEL), lambda i: (i, 0)))
        out_shape.append(jax.ShapeDtypeStruct((m, D_MODEL), F32))
    outs = pl.pallas_call(
        functools.partial(_norm_proj_body, per_token=per_token, epilogue=epilogue,
                          out_scale=out_scale),
        grid=(m // tm,),
        in_specs=in_specs,
        out_specs=out_specs,
        out_shape=out_shape,
        compiler_params=_params(("parallel",)),
        name=f"norm_proj_{epilogue}_m{m}_n{n}",
    )(*args)
    return outs if epilogue == "hgrn" else outs[0]


def _post_ffn_body(*refs, per_token, final_norm):
    (mix_ref, h_ref, g1_ref, a2_ref, s2_ref, g2_ref, wo_ref, win_ref, wout_ref) = refs[:9]
    rest = refs[9:]
    fw_ref = rest[0] if final_norm else None
    o_ref, act_ref = rest[-2:]
    row = functools.partial(_row_value, per_token=per_token)
    h_mid = h_ref[...] + row(g1_ref) * jnp.dot(mix_ref[...].astype(BF16), wo_ref[...],
                                               preferred_element_type=F32)
    xn = _norm_mod(h_mid, row(a2_ref), row(s2_ref))
    for c in range(0, D_FF, FF_TILE):
        gate = jnp.dot(xn, win_ref[:, c:c + FF_TILE], preferred_element_type=F32)
        up = jnp.dot(xn, win_ref[:, D_FF + c:D_FF + c + FF_TILE], preferred_element_type=F32)
        act_ref[:, c:c + FF_TILE] = (_silu(gate) * up).astype(BF16)
    out = h_mid + row(g2_ref) * jnp.dot(act_ref[...], wout_ref[...], preferred_element_type=F32)
    if final_norm:
        out = out * lax.rsqrt(jnp.mean(out * out, axis=-1, keepdims=True) + EPS) * fw_ref[...]
    o_ref[...] = out


def post_ffn(mix, h, g1, mod2, g2, w_o, lo, w_ffn_in, w_ffn_out, lf, *, per_token,
             rows_per_batch, final_w=None):
    m, d = h.shape
    tm = _row_tile(m, per_token, rows_per_batch)
    assert m % tm == 0 and (per_token or rows_per_batch % tm == 0)
    row = functools.partial(_row_operand, d, tm, per_token, rows_per_batch)
    tile = pl.BlockSpec((tm, d), lambda i: (i, 0))
    in_specs = [tile, tile, row(), row(), row(), row(),
                _resident((d, d), lo), _resident((d, 2 * D_FF), lf), _resident((D_FF, d), lf)]
    args = [mix, h, g1, mod2[0], mod2[1], g2, w_o, w_ffn_in, w_ffn_out]
    if final_w is not None:
        in_specs.append(pl.BlockSpec((1, d), lambda i: (0, 0)))
        args.append(final_w.reshape(1, d))
    return pl.pallas_call(
        functools.partial(_post_ffn_body, per_token=per_token, final_norm=final_w is not None),
        grid=(m // tm,),
        in_specs=in_specs,
        out_specs=tile,
        out_shape=jax.ShapeDtypeStruct((m, d), F32),
        scratch_shapes=[pltpu.VMEM((tm, D_FF), BF16)],
        compiler_params=_params(("parallel",)),
        name=f"post_ffn_m{m}",
    )(*args)


def _ada_body(c_ref, w_ref, b_ref, p_ref, q_ref, o_ref):
    c = _silu(c_ref[...]).astype(BF16)
    acc = jnp.dot(c, w_ref[...].astype(BF16), preferred_element_type=F32)
    o_ref[...] = p_ref[...] + q_ref[...] * (acc + b_ref[...])


def ada_project(c_all, w, b, p, q):
    n_l, _, n_cols = w.shape
    n_c = n_cols // D_MODEL
    rows = c_all.shape[0]
    vec_spec = pl.BlockSpec((None, None, 1, D_MODEL), lambda l, j: (l, j, 0, 0))
    return pl.pallas_call(
        _ada_body,
        grid=(n_l, n_c),
        in_specs=[
            pl.BlockSpec((rows, D_MODEL), lambda l, j: (0, 0)),
            pl.BlockSpec((None, D_MODEL, D_MODEL), lambda l, j: (l, 0, j)),
            vec_spec, vec_spec, vec_spec,
        ],
        out_specs=pl.BlockSpec((None, rows, D_MODEL), lambda l, j: (l, 0, j)),
        out_shape=jax.ShapeDtypeStruct((n_l, rows, n_cols), F32),
        compiler_params=_params(("parallel", "parallel")),
        name="ada_project",
    )(c_all, w, b, p, q)


MATMUL_LEVELS = (5, 6)


def _scan_tables():
    c = CHUNK
    t = np.arange(c)[:, None]
    u = np.arange(c)[None, :]
    blocks = [(u <= t)]
    for level in MATMUL_LEVELS:
        p = N_LEVELS - level
        odd = ((t >> p) & 1) == 1
        start = (t >> p) << p
        end = (((t >> p) + 1) << p) - 1
        blocks.append(np.where(odd, (u >= start) & (u <= t), (u > t) & (u <= end)))
    sums = np.concatenate(blocks, axis=0).astype(np.float32)
    x = np.arange(c)[:, None] ^ np.arange(c)[None, :]
    msb = np.floor(np.log2(np.maximum(x, 1))).astype(np.int32)
    lvl = np.where(x == 0, 0, N_LEVELS - msb)
    lvl = np.where(np.arange(c)[None, :] > np.arange(c)[:, None], -1, lvl).astype(np.int32)
    return sums, lvl


_SCAN_SUMS, _SCAN_LEVELS = _scan_tables()

_TN = (((1,), (1,)), ((), ()))
_TM = (((0,), (0,)), ((), ()))


HEAD_GROUP = 8
STEP_CHUNKS = 4


def _level_exponent(ex, log2_f, odd_rows, cols, level):
    m = CHUNK >> level
    if level in MATMUL_LEVELS:
        i = 1 + MATMUL_LEVELS.index(level)
        return ex[i * CHUNK:(i + 1) * CHUNK, cols]
    if m == 1:
        return jnp.where(odd_rows, log2_f[:, cols], 0.0)
    parts = []
    for i in range(CHUNK // m):
        blk = ex[i * m:(i + 1) * m, cols]
        if i & 1:
            parts.append(blk - ex[i * m - 1:i * m, cols])
        else:
            parts.append(ex[(i + 1) * m - 1:(i + 1) * m, cols] - blk)
    return jnp.concatenate(parts, axis=0)


def _level_mix(q, kk, odd_rows, level):
    m = CHUNK >> level
    if m >= 8:
        parts = [(q if (i & 1) else kk)[i * m:(i + 1) * m] for i in range(CHUNK // m)]
        return jnp.concatenate(parts, axis=0)
    return jnp.where(odd_rows, q, kk)


def _odd_blocks(x, m):
    return jnp.concatenate([x[i * m:(i + 1) * m] for i in range(1, CHUNK // m, 2)], axis=0)


def _level_update(att, mix, lvl, level):
    m = CHUNK >> level
    if m < 8:
        a = lax.dot_general(mix, mix, _TN, preferred_element_type=F32)
        return jnp.where(lvl == level, a, att)
    lhs = _odd_blocks(mix, m) if m >= 16 else mix
    a = lax.dot_general(lhs, mix, _TN, preferred_element_type=F32)
    if m < 16:
        a = _odd_blocks(a, m)
    tiles = []
    for i in range(CHUNK // m):
        rows = slice(i * m, (i + 1) * m)
        if i & 1:
            a_rows = a[(i // 2) * m:(i // 2 + 1) * m]
            tiles.append(jnp.where(lvl[rows] == level, a_rows, att[rows]))
        else:
            tiles.append(att[rows])
    return jnp.concatenate(tiles, axis=0)


def _hgrn_scan_body(q_ref, k_ref, v_ref, g_ref, lf_ref, gn_ref, sums_ref, lvl_ref,
                    o_ref, st_ref, state_ref, ex_ref):
    n = pl.program_id(1)

    @pl.when(n == 0)
    def _():
        state_ref[...] = jnp.zeros_like(state_ref)

    lvl = lvl_ref[...]
    row = lax.broadcasted_iota(jnp.int32, (CHUNK, A_DK), 0)
    gn = gn_ref[...]
    heads = [slice(h * A_DK, (h + 1) * A_DK) for h in range(A_HEADS)]
    for c in range(STEP_CHUNKS):
        rows = slice(c * CHUNK, (c + 1) * CHUNK)
        ex = ex_ref.at[c]
        log2_f = lf_ref[rows, :]
        hi = log2_f.astype(BF16)
        rem = log2_f - hi.astype(F32)
        mid = rem.astype(BF16)
        lo = (rem - mid.astype(F32)).astype(BF16)
        ex[...] = jnp.dot(sums_ref[...], jnp.concatenate([hi, mid, lo], axis=0),
                          preferred_element_type=F32)
        b_end = ex[CHUNK - 1:CHUNK, :]
        d_end = jnp.exp2(b_end)
        for g0 in range(0, A_HEADS, HEAD_GROUP):
            group = list(range(g0, g0 + HEAD_GROUP))
            q_bf = {h: q_ref[rows, heads[h]] for h in group}
            k_bf = {h: k_ref[rows, heads[h]] for h in group}
            q = {h: q_bf[h].astype(F32) for h in group}
            kk = {h: k_bf[h].astype(F32) for h in group}
            att = {h: jnp.where(lvl == 0, lax.dot_general(q_bf[h], k_bf[h], _TN,
                                                          preferred_element_type=F32), 0.0)
                   for h in group}
            for level in range(1, N_LEVELS + 1):
                odd_rows = ((row >> (N_LEVELS - level)) & 1) == 1
                for h in group:
                    e = jnp.exp2(_level_exponent(ex, log2_f, odd_rows, heads[h], level))
                    mix = (_level_mix(q[h], kk[h], odd_rows, level) * e).astype(BF16)
                    att[h] = _level_update(att[h], mix, lvl, level)
            for h in group:
                sl = heads[h]
                b_incl = ex[0:CHUNK, sl]
                q_dec = (q[h] * jnp.exp2(b_incl)).astype(BF16)
                k_dec = (kk[h] * jnp.exp2(b_end[:, sl] - b_incl)).astype(BF16)
                v = v_ref[rows, sl]
                s_t = state_ref[h]
                o = (lax.dot_general(q_dec, s_t.astype(BF16), _TN, preferred_element_type=F32)
                     + jnp.dot(att[h].astype(BF16), v, preferred_element_type=F32))
                state_ref[h] = d_end[:, sl] * s_t + lax.dot_general(
                    v, k_dec, _TM, preferred_element_type=F32)
                y = o * lax.rsqrt(jnp.mean(o * o, axis=-1, keepdims=True) + EPS)
                o_ref[rows, sl] = (y * gn * g_ref[rows, sl].astype(F32)).astype(o_ref.dtype)

    @pl.when(n == pl.num_programs(1) - 1)
    def _():
        for h in range(A_HEADS):
            st_ref[0, h] = state_ref[h].T


def hgrn_scan_prompt(qkvg, lf, gn, bsz, t):
    tr = STEP_CHUNKS * CHUNK
    ns = t // tr
    assert t % tr == 0
    sec = lambda k: pl.BlockSpec((tr, D_MODEL), lambda b, n, k=k: (b * ns + n, k))
    n_sum = _SCAN_SUMS.shape[0]
    return pl.pallas_call(
        _hgrn_scan_body,
        grid=(bsz, ns),
        in_specs=[sec(0), sec(1), sec(2), sec(3), sec(0),
                  pl.BlockSpec((1, A_DV), lambda b, n: (0, 0)),
                  pl.BlockSpec((n_sum, 3 * CHUNK), lambda b, n: (0, 0)),
                  pl.BlockSpec((CHUNK, CHUNK), lambda b, n: (0, 0))],
        out_specs=[pl.BlockSpec((tr, D_MODEL), lambda b, n: (b * ns + n, 0)),
                   pl.BlockSpec((1, A_HEADS, A_DK, A_DV), lambda b, n: (b, 0, 0, 0))],
        out_shape=[jax.ShapeDtypeStruct((bsz * t, D_MODEL), BF16),
                   jax.ShapeDtypeStruct((bsz, A_HEADS, A_DK, A_DV), F32)],
        scratch_shapes=[pltpu.VMEM((A_HEADS, A_DV, A_DK), F32),
                        pltpu.VMEM((STEP_CHUNKS, n_sum, D_MODEL), F32)],
        compiler_params=_params(("parallel", "arbitrary")),
        name="hgrn_scan",
    )(qkvg, qkvg, qkvg, qkvg, lf, gn, jnp.asarray(np.tile(_SCAN_SUMS, (1, 3)), BF16),
      jnp.asarray(_SCAN_LEVELS))


STEP_ROWS = 16


def _step_columns(decay, kk, q, v_row):
    r = lax.broadcasted_iota(jnp.int32, (STEP_ROWS, A_DK), 0)
    hi = decay.astype(BF16).astype(F32)
    rem = decay - hi
    mid = rem.astype(BF16).astype(F32)
    lo = rem - mid
    lhs = jnp.where(r == 0, hi, jnp.where(r == 1, mid, jnp.where(r == 2, lo,
                                                                 jnp.where(r == 3, kk, 0.0))))
    parts = [jnp.where(r < 3, 1.0, 0.0), jnp.where(r == 3, v_row, 0.0)]
    if q is not None:
        lhs = jnp.where(r == 4, q, lhs)
        parts.append(jnp.where(r == 4, 1.0, 0.0))
    rhs = jnp.concatenate(parts, axis=1).astype(BF16)
    out = lax.dot_general(lhs.astype(BF16), rhs, _TM, preferred_element_type=F32)
    return [out[:, i * A_DV:(i + 1) * A_DV] for i in range(len(parts))]


def _hgrn_step_body(*refs, n_replay, write_state):
    q_ref, k_ref, v_ref, g_ref, lf_ref, gn_ref, s_ref = refs[:7]
    replay = [refs[7 + 4 * i:11 + 4 * i] for i in range(n_replay)]
    outs = refs[7 + 4 * n_replay:]
    o_ref = outs[0]
    gn = gn_ref[...]
    for b in range(STEP_TOKENS):
        decay = jnp.exp2(lf_ref[b])
        q, kk, v, gate = q_ref[b], k_ref[b], v_ref[b], g_ref[b]
        for h in range(A_HEADS):
            sl = slice(h * A_DK, (h + 1) * A_DK)
            dec_m, kv_m, q_m = _step_columns(decay[:, sl], kk[:, sl], q[:, sl], v[:, sl])
            s_new = dec_m * s_ref[b, h] + kv_m
            if write_state:
                outs[1][b, n_replay, h] = s_new
            o = jnp.sum(q_m * s_new, axis=0, keepdims=True)
            y = o * lax.rsqrt(jnp.mean(o * o, axis=-1, keepdims=True) + EPS)
            o_ref[b, :, sl] = y * gn * gate[:, sl]
        for i, (kp_ref, vp_ref, lfp_ref, sp_ref) in enumerate(replay):
            decay_p = jnp.exp2(lfp_ref[b])
            for h in range(A_HEADS):
                sl = slice(h * A_DK, (h + 1) * A_DK)
                dec_m, kv_m = _step_columns(decay_p[:, sl], kp_ref[b][:, sl], None,
                                            vp_ref[b][:, sl])
                outs[1][b, i, h] = dec_m * sp_ref[b, h] + kv_m


def hgrn_step(qkvg, lf, gn, state, layer, earlier=None):
    bsz = qkvg.shape[0]
    nb = STEP_TOKENS
    assert bsz % nb == 0
    sec = lambda k: pl.BlockSpec((nb, 1, D_MODEL), lambda b, k=k: (b, 0, k))
    state_blk = lambda l: pl.BlockSpec((nb, None, A_HEADS, A_DK, A_DV),
                                       lambda b, l=l: (b, l, 0, 0, 0))
    as3 = lambda a: a.reshape(bsz, 1, a.shape[-1])
    in_specs = [sec(0), sec(1), sec(2), sec(3), sec(0),
                pl.BlockSpec((1, A_DV), lambda b: (0, 0)), state_blk(layer)]
    args = [as3(qkvg)] * 4 + [as3(lf), gn, state]
    write_state = earlier is not None
    for i, (qkvg_p, lf_p) in enumerate(earlier or ()):
        in_specs += [sec(1), sec(2), sec(0), state_blk(i)]
        args += [as3(qkvg_p), as3(qkvg_p), as3(lf_p), state]
    out_specs = [pl.BlockSpec((nb, 1, D_MODEL), lambda b: (b, 0, 0))]
    out_shape = [jax.ShapeDtypeStruct((bsz, 1, D_MODEL), F32)]
    if write_state:
        assert len(earlier) == layer
        n_l = layer + 1
        out_specs.append(pl.BlockSpec((nb, n_l, A_HEADS, A_DK, A_DV), lambda b: (b, 0, 0, 0, 0)))
        out_shape.append(jax.ShapeDtypeStruct((bsz, n_l, A_HEADS, A_DK, A_DV), F32))
    outs = pl.pallas_call(
        functools.partial(_hgrn_step_body, n_replay=len(earlier or ()), write_state=write_state),
        grid=(bsz // nb,),
        in_specs=in_specs,
        out_specs=out_specs,
        out_shape=out_shape,
        compiler_params=_params(("parallel",)),
        name="hgrn_step",
    )(*args)
    o = outs[0].reshape(bsz, D_MODEL)
    return (o, outs[1]) if write_state else (o, None)


def _t5_buckets():
    max_exact = N_BUCKETS // 2
    d = np.arange(WINDOW)
    large = max_exact + (np.log(np.maximum(d, 1).astype(np.float32) / max_exact)
                         / math.log(MAX_DISTANCE / max_exact)
                         * (N_BUCKETS - max_exact)).astype(np.int32)
    large = np.clip(large, 0, N_BUCKETS - 1)
    return np.where(d < max_exact, d, large).astype(np.int32)


_T5_BUCKETS = _t5_buckets()


LOG2E = math.log2(math.e)


def _swa_prompt_body(sink_ref, q_ref, kp_ref, kc_ref, vp_ref, vc_ref, bias_ref, o_ref):
    lane_head = lax.broadcasted_iota(jnp.int32, (WINDOW, GROUP_LANES), 1) >> HEAD_DIM_SHIFT
    from_prev = (lax.broadcasted_iota(jnp.int32, (WINDOW, WINDOW), 1)
                 > lax.broadcasted_iota(jnp.int32, (WINDOW, WINDOW), 0))
    groups = [slice(g * GROUP_LANES, (g + 1) * GROUP_LANES) for g in range(B_KV_HEADS)]
    all_logits = []
    for cs in groups:
        qg = q_ref[:, cs].astype(F32)
        kcat = jnp.concatenate([kp_ref[:, cs], kc_ref[:, cs]], axis=0)
        qstack = jnp.concatenate(
            [jnp.where(lane_head == j, qg, 0.0).astype(BF16) for j in range(B_GROUPS)], axis=0)
        all_logits.append(lax.dot_general(qstack, kcat, _TN, preferred_element_type=F32))
    for g, cs in enumerate(groups):
        logits = all_logits[g]
        vcat = jnp.concatenate([vp_ref[:, cs], vc_ref[:, cs]], axis=0)
        acc = jnp.zeros((WINDOW, GROUP_LANES), F32)
        for j in range(B_GROUPS):
            head = g * B_GROUPS + j
            rows = slice(j * WINDOW, (j + 1) * WINDOW)
            lg = jnp.where(from_prev, logits[rows, :WINDOW], logits[rows, WINDOW:]) + bias_ref[head]
            sink = sink_ref[head] * LOG2E
            m = jnp.maximum(jnp.max(lg, axis=-1, keepdims=True), sink)
            p = jnp.exp2(lg - m)
            denom = jnp.sum(p, axis=-1, keepdims=True) + jnp.exp2(sink - m)
            p_prev = jnp.where(from_prev, p, 0.0)
            p_both = jnp.concatenate([p_prev, p - p_prev], axis=1).astype(BF16)
            pv = jnp.dot(p_both, vcat, preferred_element_type=F32)
            acc = jnp.where(lane_head == j, pv / denom, acc)
        o_ref[:, cs] = acc.astype(o_ref.dtype)


def swa_prompt(q, kv_rep, sinks, bias_tabs, bsz, t):
    nb = t // WINDOW
    cur = lambda c: pl.BlockSpec((WINDOW, D_MODEL), lambda b, n, c=c: (b * nb + n, c))
    prev = lambda c: pl.BlockSpec(
        (WINDOW, D_MODEL), lambda b, n, c=c: (b * nb + jnp.maximum(n - 1, 0), c))
    return pl.pallas_call(
        _swa_prompt_body,
        grid=(bsz, nb),
        in_specs=[pl.BlockSpec(memory_space=pltpu.SMEM),
                  cur(0), prev(0), cur(0), prev(1), cur(1),
                  pl.BlockSpec((None, B_HEADS, WINDOW, WINDOW),
                               lambda b, n: (jnp.minimum(n, 1), 0, 0, 0))],
        out_specs=pl.BlockSpec((WINDOW, D_MODEL), lambda b, n: (b * nb + n, 0)),
        out_shape=jax.ShapeDtypeStruct((bsz * t, D_MODEL), BF16),
        compiler_params=_params(("parallel", "arbitrary")),
        name="swa_prompt",
    )(sinks, q, kv_rep, kv_rep, kv_rep, kv_rep, bias_tabs)


def _swa_step_body(*refs, shift_cache):
    sink_ref, qz_ref, k_ref, v_ref, bias_ref, fold_ref = refs[:6]
    if shift_cache:
        kn_ref, vn_ref, o_ref, ko_ref, vo_ref = refs[6:]
        last = lax.broadcasted_iota(jnp.int32, (WINDOW, KV_WIDTH), 0) == WINDOW - 1
    else:
        (o_ref,) = refs[6:]
    sink = sink_ref[...] * LOG2E
    head_kv = lax.broadcasted_iota(jnp.int32, (B_HEADS, KV_WIDTH), 0) >> GROUPS_SHIFT
    lane_kv = lax.broadcasted_iota(jnp.int32, (B_HEADS, KV_WIDTH), 1) >> HEAD_DIM_SHIFT
    for b in range(ATTN_TOKENS):
        if shift_cache:
            k = jnp.where(last, kn_ref[b], pltpu.roll(k_ref[b], WINDOW - 1, 0))
            v = jnp.where(last, vn_ref[b], pltpu.roll(v_ref[b], WINDOW - 1, 0))
            ko_ref[b] = k
            vo_ref[b] = v
        else:
            k, v = k_ref[b], v_ref[b]
        qz = qz_ref[b].astype(BF16)
        logits = lax.dot_general(qz, k.astype(BF16), _TN, preferred_element_type=F32)
        logits = logits + bias_ref[...]
        m = jnp.maximum(jnp.max(logits, axis=-1, keepdims=True), sink)
        p = jnp.exp2(logits - m)
        denom = jnp.sum(p, axis=-1, keepdims=True) + jnp.exp2(sink - m)
        pv = jnp.dot(p.astype(BF16), v.astype(BF16), preferred_element_type=F32) / denom
        own = jnp.where(head_kv == lane_kv, pv, 0.0).astype(BF16)
        o_ref[b] = jnp.dot(own, fold_ref[...], preferred_element_type=F32)


def swa_step(qz, k_cache, v_cache, sinks, bias_row, kv_new=None):
    bsz = qz.shape[0]
    nt = ATTN_TOKENS
    assert bsz % nt == 0
    fold = np.zeros((KV_WIDTH, B_HEAD_DIM), np.float32)
    fold[np.arange(KV_WIDTH), np.arange(KV_WIDTH) % B_HEAD_DIM] = 1.0
    cache_blk = pl.BlockSpec((nt, WINDOW, KV_WIDTH), lambda b: (b, 0, 0))
    in_specs = [pl.BlockSpec((B_HEADS, 1), lambda b: (0, 0)),
                pl.BlockSpec((nt, B_HEADS, KV_WIDTH), lambda b: (b, 0, 0)),
                cache_blk, cache_blk,
                pl.BlockSpec((B_HEADS, WINDOW), lambda b: (0, 0)),
                pl.BlockSpec((KV_WIDTH, B_HEAD_DIM), lambda b: (0, 0))]
    args = [sinks.reshape(B_HEADS, 1), qz, k_cache, v_cache, bias_row, jnp.asarray(fold, BF16)]
    out_specs = [pl.BlockSpec((nt, B_HEADS, B_HEAD_DIM), lambda b: (b, 0, 0))]
    out_shape = [jax.ShapeDtypeStruct((bsz, B_HEADS, B_HEAD_DIM), F32)]
    if kv_new is not None:
        row_blk = pl.BlockSpec((nt, 1, KV_WIDTH), lambda b: (b, 0, 0))
        in_specs += [row_blk, row_blk]
        args += list(kv_new)
        out_specs += [cache_blk, cache_blk]
        out_shape += [jax.ShapeDtypeStruct(k_cache.shape, F32)] * 2
    outs = pl.pallas_call(
        functools.partial(_swa_step_body, shift_cache=kv_new is not None),
        grid=(bsz // nt,),
        in_specs=in_specs,
        out_specs=out_specs,
        out_shape=out_shape,
        compiler_params=_params(("parallel",)),
        name="swa_step",
    )(*args)
    return outs if kv_new is not None else outs[0]


def _trunk(x, mods, kv_mod, per_token, hgrn_state0, k_buf, v_buf, wts, lbs, bias_tab, bias_row):
    (w_in_a, w_o_a, gnorm_a, w_q_b, w_o_b, sinks_b, w_ffn_in, w_ffn_out, final_norm_w,
     w_kv) = wts
    prompt = k_buf is None
    bsz, t, _ = x.shape
    m = bsz * t
    common = dict(per_token=per_token, rows_per_batch=t)
    h = x.reshape(m, D_MODEL)
    states = []
    step_inputs = []
    new_state = None
    kv = k_state = v_state = kv_new = None
    for l in range(DEPTH):
        a1, s1, g1, a2, s2, g2 = mods[l]
        if l < N_A_LAYERS:
            qkvg, lf = norm_proj(h, (a1, s1), w_in_a, l, epilogue="hgrn",
                                 lb=lbs[l].reshape(1, D_MODEL), **common)
            gn = gnorm_a[l].reshape(1, A_DV)
            if prompt:
                mix, s_new = hgrn_scan_prompt(qkvg, lf, gn, bsz, t)
                states.append(s_new)
            else:
                qkvg = qkvg.astype(F32)
                last_a = l == N_A_LAYERS - 1
                mix, new_state = hgrn_step(qkvg, lf, gn, hgrn_state0, l,
                                           earlier=list(step_inputs) if last_a else None)
                step_inputs.append((qkvg, lf))
            w_o, lo = w_o_a, l
        else:
            j = l - N_A_LAYERS
            scale = LOG2E / math.sqrt(B_HEAD_DIM)
            if prompt:
                q = norm_proj(h, (a1, s1), w_q_b, j, out_scale=scale, **common)
                mix = swa_prompt(q, kv, sinks_b[j], bias_tab, bsz, t)
            else:
                q = norm_proj(h, (a1, s1), w_q_b, j, out_scale=scale, out_dtype=F32, **common)
                lane_kv = (np.arange(KV_WIDTH) // B_HEAD_DIM)[None, :]
                head_kv = (np.arange(B_HEADS) // B_GROUPS)[:, None]
                qz = jnp.where((lane_kv == head_kv)[None],
                               jnp.tile(q.reshape(m, B_HEADS, B_HEAD_DIM), (1, 1, B_KV_HEADS)), 0.0)
                if j == 0:
                    mix, k_state, v_state = swa_step(qz, k_buf.reshape(m, WINDOW, KV_WIDTH),
                                                     v_buf.reshape(m, WINDOW, KV_WIDTH),
                                                     sinks_b[j], bias_row, kv_new)
                else:
                    mix = swa_step(qz, k_state, v_state, sinks_b[j], bias_row)
                mix = mix.reshape(m, D_MODEL)
            w_o, lo = w_o_b, j
        h = post_ffn(mix, h, g1, (a2, s2), g2, w_o, lo, w_ffn_in, w_ffn_out, l,
                     final_w=final_norm_w if l == DEPTH - 1 else None, **common)
        if l == N_A_LAYERS - 1:
            if prompt:
                kv = norm_proj(h, kv_mod, w_kv, 0, epilogue="kv_rep", **common)
                tail = h.reshape(bsz, t, D_MODEL)[:, -WINDOW:].reshape(bsz * WINDOW, D_MODEL)
                kv_tail = norm_proj(tail, kv_mod, w_kv, 0, per_token=False,
                                    rows_per_batch=WINDOW, out_dtype=F32)
                kv_tail = kv_tail.reshape(bsz, WINDOW, 2, B_KV_HEADS, B_HEAD_DIM)
                k_state, v_state = kv_tail[:, :, 0], kv_tail[:, :, 1]
            else:
                kv_row = norm_proj(h, kv_mod, w_kv, 0, out_dtype=F32, **common)
                kv_new = (kv_row[:, :KV_WIDTH].reshape(m, 1, KV_WIDTH),
                          kv_row[:, KV_WIDTH:].reshape(m, 1, KV_WIDTH))
    y = h.reshape(bsz, t, D_MODEL)
    if prompt:
        return y, jnp.stack(states, axis=1), k_state, v_state
    cache_shape = (m, WINDOW, B_KV_HEADS, B_HEAD_DIM)
    return y, new_state, k_state.reshape(cache_shape), v_state.reshape(cache_shape)


def kernel(x_prompt, x_sample, state_hgrn, cache_swa_k, cache_swa_v, c_prompt, c_sample,
           w_in_a, w_o_a, gnorm_a, lb_a, w_kv, w_ada_kv, b_ada_kv, kv_norm_w, w_q_b, w_o_b,
           sinks_b, rel_bias, norm_w, w_ada, b_ada, w_ffn_in, w_ffn_out, final_norm_w):
    n_p, n_s = c_prompt.shape[0], c_sample.shape[0]
    rows = -(-(n_p + n_s) // 8) * 8
    c_all = jnp.concatenate(
        [c_prompt, c_sample, jnp.zeros((rows - n_p - n_s, D_MODEL), F32)], axis=0)

    zeros, ones = jnp.zeros((DEPTH, D_MODEL), F32), jnp.ones((DEPTH, D_MODEL), F32)
    p_ada = jnp.stack([zeros, norm_w[:, 0], zeros, zeros, norm_w[:, 1], zeros], axis=1)
    q_ada = jnp.stack([ones, norm_w[:, 0], ones, ones, norm_w[:, 1], ones], axis=1)
    ada = ada_project(c_all, w_ada, b_ada.reshape(DEPTH, 6, 1, D_MODEL),
                      p_ada.reshape(DEPTH, 6, 1, D_MODEL), q_ada.reshape(DEPTH, 6, 1, D_MODEL))
    p_kv = jnp.stack([zeros[0], kv_norm_w]).reshape(1, 2, 1, D_MODEL)
    q_kv = jnp.stack([ones[0], kv_norm_w]).reshape(1, 2, 1, D_MODEL)
    ada_kv = ada_project(c_all, w_ada_kv.reshape(1, D_MODEL, 2 * D_MODEL),
                         b_ada_kv.reshape(1, 2, 1, D_MODEL), p_kv, q_kv)

    def split_mods(arr, n_chunks, lo, hi, per_token):
        out = []
        for c in range(n_chunks):
            v = arr[lo:hi, c * D_MODEL:(c + 1) * D_MODEL]
            out.append(v if per_token else v.reshape(hi - lo, 1, D_MODEL))
        return out

    def mods_for(lo, hi, per_token):
        layers = []
        for l in range(DEPTH):
            sh1, a1, g1, sh2, a2, g2 = split_mods(ada[l], 6, lo, hi, per_token)
            layers.append((a1, sh1, g1, a2, sh2, g2))
        sh_kv, a_kv = split_mods(ada_kv[0], 2, lo, hi, per_token)
        return layers, (a_kv, sh_kv)

    lb_sm = jax.nn.softmax(lb_a.astype(F32), axis=0)
    lbs = jnp.cumsum(lb_sm, axis=0) - lb_sm[0:1]

    rb = rel_bias.astype(F32)[_T5_BUCKETS]
    t_idx, j_idx = np.arange(WINDOW)[:, None], np.arange(WINDOW)[None, :]
    from_prev = j_idx > t_idx
    dist = np.where(from_prev, t_idx + WINDOW - j_idx, t_idx - j_idx)
    onehot = (jnp.asarray(dist)[:, :, None] == jnp.arange(WINDOW)[None, None, :]).astype(F32)
    tab = jnp.einsum("tjd,dh->htj", onehot, rb, precision=lax.Precision.HIGHEST) * LOG2E
    bias_tab = jnp.stack([jnp.where(from_prev, MASK_VALUE, tab),
                          tab])
    bias_row = rb[::-1].T * LOG2E

    bf = lambda w: w.astype(BF16)
    wts = (bf(w_in_a), bf(w_o_a), gnorm_a, bf(w_q_b), bf(w_o_b), sinks_b, bf(w_ffn_in),
           bf(w_ffn_out), final_norm_w, bf(w_kv)[None])
    mods_p, kv_mod_p = mods_for(0, n_p, False)
    mods_s, kv_mod_s = mods_for(n_p, n_p + n_s, True)
    y_p, st_p, k_p, v_p = _trunk(x_prompt, mods_p, kv_mod_p, False, None, None, None, wts, lbs,
                                 bias_tab, bias_row)
    y_s, st_s, k_s, v_s = _trunk(x_sample, mods_s, kv_mod_s, True, state_hgrn, cache_swa_k,
                                 cache_swa_v, wts, lbs, bias_tab, bias_row)
    return (y_p, y_s, st_p, st_s, k_p, v_p, k_s, v_s)
```

```python
import functools
import math

import numpy as np
import jax
import jax.numpy as jnp
from jax import lax
from jax.experimental import pallas as pl
from jax.experimental.pallas import tpu as pltpu

F32 = jnp.float32
BF16 = jnp.bfloat16

D_MODEL = 1024
DEPTH = 4
N_A_LAYERS = 2
A_HEADS = 8
A_DK = 128
A_DV = 128
F_MIN = 1e-30
B_HEAD_DIM = 64
B_HEADS = 16
B_KV_HEADS = 4
B_GROUPS = 4
WINDOW = 128
MASK_VALUE = -1e30
N_BUCKETS = 32
MAX_DISTANCE = 128
D_FF = 2816
EPS = 1e-6
GROUP_LANES = B_GROUPS * B_HEAD_DIM
KV_WIDTH = B_KV_HEADS * B_HEAD_DIM
HEAD_DIM_SHIFT = B_HEAD_DIM.bit_length() - 1
GROUPS_SHIFT = B_GROUPS.bit_length() - 1

CHUNK = 128
N_LEVELS = 7
ROW_TILE = 512
COL_TILE = 1024
FF_TILE = 256
STEP_TOKENS = 4
ATTN_TOKENS = 8
VMEM_LIMIT_BYTES = 48 * 1024 * 1024


def _params(sem):
    return pltpu.CompilerParams(dimension_semantics=sem, vmem_limit_bytes=VMEM_LIMIT_BYTES)


def _row_tile(m, per_token, rows_per_batch):
    return min(m, ROW_TILE) if per_token else min(m, ROW_TILE, rows_per_batch)


def _sigmoid(x):
    return 0.5 * jnp.tanh(0.5 * x) + 0.5


def _silu(x):
    h = 0.5 * x
    return h * jnp.tanh(h) + h


def _norm_mod(x, a, s):
    y = x * lax.rsqrt(jnp.mean(x * x, axis=-1, keepdims=True) + EPS)
    return (y * a + s).astype(BF16)


def _resident(shape, layer):
    return pl.BlockSpec((None,) + shape, lambda i: (layer,) + (0,) * len(shape),
                        pipeline_mode=pl.Buffered(1))


def _row_operand(width, tm, per_token, rows_per_batch):
    if per_token:
        return pl.BlockSpec((tm, width), lambda i: (i, 0))
    return pl.BlockSpec((1, 1, width), lambda i: ((i * tm) // rows_per_batch, 0, 0))


def _row_value(ref, per_token):
    return ref[...] if per_token else ref[0]


def _norm_proj_body(*refs, per_token, epilogue, out_scale):
    x_ref, a_ref, s_ref, w_ref = refs[:4]
    xn = _norm_mod(x_ref[...], _row_value(a_ref, per_token), _row_value(s_ref, per_token))

    tn = min(COL_TILE, w_ref.shape[1])

    def proj(c):
        return jnp.dot(xn, w_ref[:, c:c + tn], preferred_element_type=F32)

    if epilogue == "plain":
        (o_ref,) = refs[4:]
        for c in range(0, o_ref.shape[1], tn):
            acc = proj(c)
            if out_scale is not None:
                acc = acc * out_scale
            o_ref[:, c:c + tn] = acc.astype(o_ref.dtype)
    elif epilogue == "kv_rep":
        (o_ref,) = refs[4:]
        acc = proj(0)
        low_half = lax.broadcasted_iota(jnp.int32, (acc.shape[0], 128), 1) < B_HEAD_DIM
        for c in range(acc.shape[1] // 128):
            x = acc[:, c * 128:(c + 1) * 128]
            swapped = pltpu.roll(x, B_HEAD_DIM, 1)
            for half, rep in enumerate((jnp.where(low_half, x, swapped),
                                        jnp.where(low_half, swapped, x))):
                rep = rep.astype(o_ref.dtype)
                base = (2 * c + half) * GROUP_LANES
                o_ref[:, base:base + 128] = rep
                o_ref[:, base + 128:base + 256] = rep
    elif epilogue == "hgrn":
        lb_ref, o_ref, lf_ref = refs[4:]
        d = D_MODEL
        lb = lb_ref[...]
        o_ref[:, 0:d] = _silu(proj(0)).astype(o_ref.dtype)
        f = lb + (1.0 - lb) * _sigmoid(proj(d))
        o_ref[:, d:2 * d] = (1.0 - f).astype(o_ref.dtype)
        lf_ref[...] = jnp.log2(jnp.maximum(f, F_MIN))
        o_ref[:, 2 * d:3 * d] = proj(2 * d).astype(o_ref.dtype)
        o_ref[:, 3 * d:4 * d] = _silu(proj(3 * d)).astype(o_ref.dtype)
    else:
        raise ValueError(epilogue)


def norm_proj(x, mod, w, layer, *, per_token, rows_per_batch, epilogue="plain", lb=None,
              out_dtype=BF16, out_scale=None):
    m, k = x.shape
    n = w.shape[-1]
    tm = _row_tile(m, per_token, rows_per_batch)
    assert m % tm == 0 and (per_token or rows_per_batch % tm == 0)
    row = functools.partial(_row_operand, tm=tm, per_token=per_token, rows_per_batch=rows_per_batch)
    in_specs = [pl.BlockSpec((tm, k), lambda i: (i, 0)), row(k), row(k), _resident((k, n), layer)]
    args = [x, mod[0], mod[1], w]
    n_out = n * B_GROUPS if epilogue == "kv_rep" else n
    out_specs = [pl.BlockSpec((tm, n_out), lambda i: (i, 0))]
    out_shape = [jax.ShapeDtypeStruct((m, n_out), out_dtype)]
    if epilogue == "hgrn":
        in_specs.append(pl.BlockSpec((1, D_MODEL), lambda i: (0, 0)))
        args.append(lb)
        out_specs.append(pl.BlockSpec((tm, D_MODEL), lambda i: (i, 0)))
        out_shape.append(jax.ShapeDtypeStruct((m, D_MODEL), F32))
    outs = pl.pallas_call(
        functools.partial(_norm_proj_body, per_token=per_token, epilogue=epilogue,
                          out_scale=out_scale),
        grid=(m // tm,),
        in_specs=in_specs,
        out_specs=out_specs,
        out_shape=out_shape,
        compiler_params=_params(("parallel",)),
        name=f"norm_proj_{epilogue}_m{m}_n{n}",
    )(*args)
    return outs if epilogue == "hgrn" else outs[0]


def _post_ffn_body(*refs, per_token, final_norm):
    (mix_ref, h_ref, g1_ref, a2_ref, s2_ref, g2_ref, wo_ref, win_ref, wout_ref) = refs[:9]
    rest = refs[9:]
    fw_ref = rest[0] if final_norm else None
    o_ref, act_ref = rest[-2:]
    row = functools.partial(_row_value, per_token=per_token)
    h_mid = h_ref[...] + row(g1_ref) * jnp.dot(mix_ref[...].astype(BF16), wo_ref[...],
                                               preferred_element_type=F32)
    xn = _norm_mod(h_mid, row(a2_ref), row(s2_ref))
    for c in range(0, D_FF, FF_TILE):
        gate = jnp.dot(xn, win_ref[:, c:c + FF_TILE], preferred_element_type=F32)
        up = jnp.dot(xn, win_ref[:, D_FF + c:D_FF + c + FF_TILE], preferred_element_type=F32)
        act_ref[:, c:c + FF_TILE] = (_silu(gate) * up).astype(BF16)
    out = h_mid + row(g2_ref) * jnp.dot(act_ref[...], wout_ref[...], preferred_element_type=F32)
    if final_norm:
        out = out * lax.rsqrt(jnp.mean(out * out, axis=-1, keepdims=True) + EPS) * fw_ref[...]
    o_ref[...] = out


def post_ffn(mix, h, g1, mod2, g2, w_o, lo, w_ffn_in, w_ffn_out, lf, *, per_token,
             rows_per_batch, final_w=None):
    m, d = h.shape
    tm = _row_tile(m, per_token, rows_per_batch)
    assert m % tm == 0 and (per_token or rows_per_batch % tm == 0)
    row = functools.partial(_row_operand, d, tm, per_token, rows_per_batch)
    tile = pl.BlockSpec((tm, d), lambda i: (i, 0))
    in_specs = [tile, tile, row(), row(), row(), row(),
                _resident((d, d), lo), _resident((d, 2 * D_FF), lf), _resident((D_FF, d), lf)]
    args = [mix, h, g1, mod2[0], mod2[1], g2, w_o, w_ffn_in, w_ffn_out]
    if final_w is not None:
        in_specs.append(pl.BlockSpec((1, d), lambda i: (0, 0)))
        args.append(final_w.reshape(1, d))
    return pl.pallas_call(
        functools.partial(_post_ffn_body, per_token=per_token, final_norm=final_w is not None),
        grid=(m // tm,),
        in_specs=in_specs,
        out_specs=tile,
        out_shape=jax.ShapeDtypeStruct((m, d), F32),
        scratch_shapes=[pltpu.VMEM((tm, D_FF), BF16)],
        compiler_params=_params(("parallel",)),
        name=f"post_ffn_m{m}",
    )(*args)


def _ada_body(c_ref, w_ref, b_ref, p_ref, q_ref, o_ref):
    c = _silu(c_ref[...]).astype(BF16)
    acc = jnp.dot(c, w_ref[...].astype(BF16), preferred_element_type=F32)
    o_ref[...] = p_ref[...] + q_ref[...] * (acc + b_ref[...])


def ada_project(c_all, w, b, p, q):
    n_l, _, n_cols = w.shape
    n_c = n_cols // D_MODEL
    rows = c_all.shape[0]
    vec_spec = pl.BlockSpec((None, None, 1, D_MODEL), lambda l, j: (l, j, 0, 0))
    return pl.pallas_call(
        _ada_body,
        grid=(n_l, n_c),
        in_specs=[
            pl.BlockSpec((rows, D_MODEL), lambda l, j: (0, 0)),
            pl.BlockSpec((None, D_MODEL, D_MODEL), lambda l, j: (l, 0, j)),
            vec_spec, vec_spec, vec_spec,
        ],
        out_specs=pl.BlockSpec((None, rows, D_MODEL), lambda l, j: (l, 0, j)),
        out_shape=jax.ShapeDtypeStruct((n_l, rows, n_cols), F32),
        compiler_params=_params(("parallel", "parallel")),
        name="ada_project",
    )(c_all, w, b, p, q)


MATMUL_LEVELS = (5, 6)


def _scan_tables():
    c = CHUNK
    t = np.arange(c)[:, None]
    u = np.arange(c)[None, :]
    blocks = [(u <= t)]
    for level in MATMUL_LEVELS:
        p = N_LEVELS - level
        odd = ((t >> p) & 1) == 1
        start = (t >> p) << p
        end = (((t >> p) + 1) << p) - 1
        blocks.append(np.where(odd, (u >= start) & (u <= t), (u > t) & (u <= end)))
    sums = np.concatenate(blocks, axis=0).astype(np.float32)
    x = np.arange(c)[:, None] ^ np.arange(c)[None, :]
    msb = np.floor(np.log2(np.maximum(x, 1))).astype(np.int32)
    lvl = np.where(x == 0, 0, N_LEVELS - msb)
    lvl = np.where(np.arange(c)[None, :] > np.arange(c)[:, None], -1, lvl).astype(np.int32)
    return sums, lvl


_SCAN_SUMS, _SCAN_LEVELS = _scan_tables()

_TN = (((1,), (1,)), ((), ()))
_TM = (((0,), (0,)), ((), ()))


HEAD_GROUP = 8
STEP_CHUNKS = 4


def _level_exponent(ex, log2_f, odd_rows, cols, level):
    m = CHUNK >> level
    if level in MATMUL_LEVELS:
        i = 1 + MATMUL_LEVELS.index(level)
        return ex[i * CHUNK:(i + 1) * CHUNK, cols]
    if m == 1:
        return jnp.where(odd_rows, log2_f[:, cols], 0.0)
    parts = []
    for i in range(CHUNK // m):
        blk = ex[i * m:(i + 1) * m, cols]
        if i & 1:
            parts.append(blk - ex[i * m - 1:i * m, cols])
        else:
            parts.append(ex[(i + 1) * m - 1:(i + 1) * m, cols] - blk)
    return jnp.concatenate(parts, axis=0)


def _level_mix(q, kk, odd_rows, level):
    m = CHUNK >> level
    if m >= 8:
        parts = [(q if (i & 1) else kk)[i * m:(i + 1) * m] for i in range(CHUNK // m)]
        return jnp.concatenate(parts, axis=0)
    return jnp.where(odd_rows, q, kk)


def _odd_blocks(x, m):
    return jnp.concatenate([x[i * m:(i + 1) * m] for i in range(1, CHUNK // m, 2)], axis=0)


def _level_update(att, mix, lvl, level):
    m = CHUNK >> level
    if m < 8:
        a = lax.dot_general(mix, mix, _TN, preferred_element_type=F32)
        return jnp.where(lvl == level, a, att)
    lhs = _odd_blocks(mix, m) if m >= 16 else mix
    a = lax.dot_general(lhs, mix, _TN, preferred_element_type=F32)
    if m < 16:
        a = _odd_blocks(a, m)
    tiles = []
    for i in range(CHUNK // m):
        rows = slice(i * m, (i + 1) * m)
        if i & 1:
            a_rows = a[(i // 2) * m:(i // 2 + 1) * m]
            tiles.append(jnp.where(lvl[rows] == level, a_rows, att[rows]))
        else:
            tiles.append(att[rows])
    return jnp.concatenate(tiles, axis=0)


def _hgrn_scan_body(q_ref, k_ref, v_ref, g_ref, lf_ref, gn_ref, sums_ref, lvl_ref,
                    o_ref, st_ref, state_ref, ex_ref):
    n = pl.program_id(1)

    @pl.when(n == 0)
    def _():
        state_ref[...] = jnp.zeros_like(state_ref)

    lvl = lvl_ref[...]
    row = lax.broadcasted_iota(jnp.int32, (CHUNK, A_DK), 0)
    gn = gn_ref[...]
    heads = [slice(h * A_DK, (h + 1) * A_DK) for h in range(A_HEADS)]
    for c in range(STEP_CHUNKS):
        rows = slice(c * CHUNK, (c + 1) * CHUNK)
        ex = ex_ref.at[c]
        log2_f = lf_ref[rows, :]
        hi = log2_f.astype(BF16)
        rem = log2_f - hi.astype(F32)
        mid = rem.astype(BF16)
        lo = (rem - mid.astype(F32)).astype(BF16)
        ex[...] = jnp.dot(sums_ref[...], jnp.concatenate([hi, mid, lo], axis=0),
                          preferred_element_type=F32)
        b_end = ex[CHUNK - 1:CHUNK, :]
        d_end = jnp.exp2(b_end)
        for g0 in range(0, A_HEADS, HEAD_GROUP):
            group = list(range(g0, g0 + HEAD_GROUP))
            q_bf = {h: q_ref[rows, heads[h]] for h in group}
            k_bf = {h: k_ref[rows, heads[h]] for h in group}
            q = {h: q_bf[h].astype(F32) for h in group}
            kk = {h: k_bf[h].astype(F32) for h in group}
            att = {h: jnp.where(lvl == 0, lax.dot_general(q_bf[h], k_bf[h], _TN,
                                                          preferred_element_type=F32), 0.0)
                   for h in group}
            for level in range(1, N_LEVELS + 1):
                odd_rows = ((row >> (N_LEVELS - level)) & 1) == 1
                for h in group:
                    e = jnp.exp2(_level_exponent(ex, log2_f, odd_rows, heads[h], level))
                    mix = (_level_mix(q[h], kk[h], odd_rows, level) * e).astype(BF16)
                    att[h] = _level_update(att[h], mix, lvl, level)
            for h in group:
                sl = heads[h]
                b_incl = ex[0:CHUNK, sl]
                q_dec = (q[h] * jnp.exp2(b_incl)).astype(BF16)
                k_dec = (kk[h] * jnp.exp2(b_end[:, sl] - b_incl)).astype(BF16)
                v = v_ref[rows, sl]
                s_t = state_ref[h]
                o = (lax.dot_general(q_dec, s_t.astype(BF16), _TN, preferred_element_type=F32)
                     + jnp.dot(att[h].astype(BF16), v, preferred_element_type=F32))
                state_ref[h] = d_end[:, sl] * s_t + lax.dot_general(
                    v, k_dec, _TM, preferred_element_type=F32)
                y = o * lax.rsqrt(jnp.mean(o * o, axis=-1, keepdims=True) + EPS)
                o_ref[rows, sl] = (y * gn * g_ref[rows, sl].astype(F32)).astype(o_ref.dtype)

    @pl.when(n == pl.num_programs(1) - 1)
    def _():
        for h in range(A_HEADS):
            st_ref[0, h] = state_ref[h].T


def hgrn_scan_prompt(qkvg, lf, gn, bsz, t):
    tr = STEP_CHUNKS * CHUNK
    ns = t // tr
    assert t % tr == 0
    sec = lambda k: pl.BlockSpec((tr, D_MODEL), lambda b, n, k=k: (b * ns + n, k))
    n_sum = _SCAN_SUMS.shape[0]
    return pl.pallas_call(
        _hgrn_scan_body,
        grid=(bsz, ns),
        in_specs=[sec(0), sec(1), sec(2), sec(3), sec(0),
                  pl.BlockSpec((1, A_DV), lambda b, n: (0, 0)),
                  pl.BlockSpec((n_sum, 3 * CHUNK), lambda b, n: (0, 0)),
                  pl.BlockSpec((CHUNK, CHUNK), lambda b, n: (0, 0))],
        out_specs=[pl.BlockSpec((tr, D_MODEL), lambda b, n: (b * ns + n, 0)),
                   pl.BlockSpec((1, A_HEADS, A_DK, A_DV), lambda b, n: (b, 0, 0, 0))],
        out_shape=[jax.ShapeDtypeStruct((bsz * t, D_MODEL), BF16),
                   jax.ShapeDtypeStruct((bsz, A_HEADS, A_DK, A_DV), F32)],
        scratch_shapes=[pltpu.VMEM((A_HEADS, A_DV, A_DK), F32),
                        pltpu.VMEM((STEP_CHUNKS, n_sum, D_MODEL), F32)],
        compiler_params=_params(("parallel", "arbitrary")),
        name="hgrn_scan",
    )(qkvg, qkvg, qkvg, qkvg, lf, gn, jnp.asarray(np.tile(_SCAN_SUMS, (1, 3)), BF16),
      jnp.asarray(_SCAN_LEVELS))


STEP_ROWS = 16


def _step_columns(decay, kk, q, v_row):
    r = lax.broadcasted_iota(jnp.int32, (STEP_ROWS, A_DK), 0)
    hi = decay.astype(BF16).astype(F32)
    rem = decay - hi
    mid = rem.astype(BF16).astype(F32)
    lo = rem - mid
    lhs = jnp.where(r == 0, hi, jnp.where(r == 1, mid, jnp.where(r == 2, lo,
                                                                 jnp.where(r == 3, kk, 0.0))))
    parts = [jnp.where(r < 3, 1.0, 0.0), jnp.where(r == 3, v_row, 0.0)]
    if q is not None:
        lhs = jnp.where(r == 4, q, lhs)
        parts.append(jnp.where(r == 4, 1.0, 0.0))
    rhs = jnp.concatenate(parts, axis=1).astype(BF16)
    out = lax.dot_general(lhs.astype(BF16), rhs, _TM, preferred_element_type=F32)
    return [out[:, i * A_DV:(i + 1) * A_DV] for i in range(len(parts))]


def _hgrn_step_body(*refs, n_replay, write_state):
    q_ref, k_ref, v_ref, g_ref, lf_ref, gn_ref, s_ref = refs[:7]
    replay = [refs[7 + 4 * i:11 + 4 * i] for i in range(n_replay)]
    outs = refs[7 + 4 * n_replay:]
    o_ref = outs[0]
    gn = gn_ref[...]
    for b in range(STEP_TOKENS):
        decay = jnp.exp2(lf_ref[b])
        q, kk, v, gate = q_ref[b], k_ref[b], v_ref[b], g_ref[b]
        for h in range(A_HEADS):
            sl = slice(h * A_DK, (h + 1) * A_DK)
            if write_state:
                dec_m, kv_m, q_m = _step_columns(decay[:, sl], kk[:, sl], q[:, sl], v[:, sl])
                s_new = dec_m * s_ref[b, h] + kv_m
                outs[1][b, n_replay, h] = s_new
                o = jnp.sum(q_m * s_new, axis=0, keepdims=True)
            else:
                r = lax.broadcasted_iota(jnp.int32, (STEP_ROWS, A_DK), 0)
                lhs = jnp.where(r == 0, q[:, sl] * decay[:, sl], 0.0).astype(BF16)
                ones = jnp.where(r == 0, 1.0, 0.0).astype(BF16)
                qd_m = lax.dot_general(lhs, ones, _TM, preferred_element_type=F32)
                qk = jnp.sum(q[:, sl] * kk[:, sl], axis=-1, keepdims=True)
                o = jnp.sum(qd_m * s_ref[b, h], axis=0, keepdims=True) + qk * v[:, sl]
            y = o * lax.rsqrt(jnp.mean(o * o, axis=-1, keepdims=True) + EPS)
            o_ref[b, :, sl] = y * gn * gate[:, sl]
        for i, (kp_ref, vp_ref, lfp_ref, sp_ref) in enumerate(replay):
            decay_p = jnp.exp2(lfp_ref[b])
            for h in range(A_HEADS):
                sl = slice(h * A_DK, (h + 1) * A_DK)
                dec_m, kv_m = _step_columns(decay_p[:, sl], kp_ref[b][:, sl], None,
                                            vp_ref[b][:, sl])
                outs[1][b, i, h] = dec_m * sp_ref[b, h] + kv_m


def hgrn_step(qkvg, lf, gn, state, layer, earlier=None):
    bsz = qkvg.shape[0]
    nb = STEP_TOKENS
    assert bsz % nb == 0
    sec = lambda k: pl.BlockSpec((nb, 1, D_MODEL), lambda b, k=k: (b, 0, k))
    state_blk = lambda l: pl.BlockSpec((nb, None, A_HEADS, A_DK, A_DV),
                                       lambda b, l=l: (b, l, 0, 0, 0))
    as3 = lambda a: a.reshape(bsz, 1, a.shape[-1])
    in_specs = [sec(0), sec(1), sec(2), sec(3), sec(0),
                pl.BlockSpec((1, A_DV), lambda b: (0, 0)), state_blk(layer)]
    args = [as3(qkvg)] * 4 + [as3(lf), gn, state]
    write_state = earlier is not None
    for i, (qkvg_p, lf_p) in enumerate(earlier or ()):
        in_specs += [sec(1), sec(2), sec(0), state_blk(i)]
        args += [as3(qkvg_p), as3(qkvg_p), as3(lf_p), state]
    out_specs = [pl.BlockSpec((nb, 1, D_MODEL), lambda b: (b, 0, 0))]
    out_shape = [jax.ShapeDtypeStruct((bsz, 1, D_MODEL), F32)]
    if write_state:
        assert len(earlier) == layer
        n_l = layer + 1
        out_specs.append(pl.BlockSpec((nb, n_l, A_HEADS, A_DK, A_DV), lambda b: (b, 0, 0, 0, 0)))
        out_shape.append(jax.ShapeDtypeStruct((bsz, n_l, A_HEADS, A_DK, A_DV), F32))
    outs = pl.pallas_call(
        functools.partial(_hgrn_step_body, n_replay=len(earlier or ()), write_state=write_state),
        grid=(bsz // nb,),
        in_specs=in_specs,
        out_specs=out_specs,
        out_shape=out_shape,
        compiler_params=_params(("parallel",)),
        name="hgrn_step",
    )(*args)
    o = outs[0].reshape(bsz, D_MODEL)
    return (o, outs[1]) if write_state else (o, None)


def _t5_buckets():
    max_exact = N_BUCKETS // 2
    d = np.arange(WINDOW)
    large = max_exact + (np.log(np.maximum(d, 1).astype(np.float32) / max_exact)
                         / math.log(MAX_DISTANCE / max_exact)
                         * (N_BUCKETS - max_exact)).astype(np.int32)
    large = np.clip(large, 0, N_BUCKETS - 1)
    return np.where(d < max_exact, d, large).astype(np.int32)


_T5_BUCKETS = _t5_buckets()


LOG2E = math.log2(math.e)


SWA_BLOCKS = 4


def _swa_prompt_body(sink_ref, q_ref, kp_ref, kc_ref, vp_ref, vc_ref, bias0_ref, bias_ref, o_ref):
    lane_head = lax.broadcasted_iota(jnp.int32, (WINDOW, GROUP_LANES), 1) >> HEAD_DIM_SHIFT
    from_prev = (lax.broadcasted_iota(jnp.int32, (WINDOW, WINDOW), 1)
                 > lax.broadcasted_iota(jnp.int32, (WINDOW, WINDOW), 0))
    groups = [slice(g * GROUP_LANES, (g + 1) * GROUP_LANES) for g in range(B_KV_HEADS)]
    for blk in range(SWA_BLOCKS):
        rows = slice(blk * WINDOW, (blk + 1) * WINDOW)
        prev_rows = slice((blk - 1) * WINDOW, blk * WINDOW)
        table = bias0_ref if blk == 0 else bias_ref

        def keys(prev_ref, cur_ref, cs):
            prev = prev_ref[:, cs] if blk == 0 else cur_ref[prev_rows, cs]
            return jnp.concatenate([prev, cur_ref[rows, cs]], axis=0)

        all_logits = []
        for cs in groups:
            qg = q_ref[rows, cs].astype(F32)
            qstack = jnp.concatenate(
                [jnp.where(lane_head == j, qg, 0.0).astype(BF16) for j in range(B_GROUPS)], axis=0)
            all_logits.append(lax.dot_general(qstack, keys(kp_ref, kc_ref, cs), _TN,
                                              preferred_element_type=F32))
        for g, cs in enumerate(groups):
            logits = all_logits[g]
            vcat = keys(vp_ref, vc_ref, cs)
            acc = jnp.zeros((WINDOW, GROUP_LANES), F32)
            for j in range(B_GROUPS):
                head = g * B_GROUPS + j
                hr = slice(j * WINDOW, (j + 1) * WINDOW)
                lg = jnp.where(from_prev, logits[hr, :WINDOW], logits[hr, WINDOW:]) + table[head]
                sink = sink_ref[head] * LOG2E
                m = jnp.maximum(jnp.max(lg, axis=-1, keepdims=True), sink)
                p = jnp.exp2(lg - m)
                denom = jnp.sum(p, axis=-1, keepdims=True) + jnp.exp2(sink - m)
                p_prev = jnp.where(from_prev, p, 0.0)
                p_both = jnp.concatenate([p_prev, p - p_prev], axis=1).astype(BF16)
                pv = jnp.dot(p_both, vcat, preferred_element_type=F32)
                acc = jnp.where(lane_head == j, pv / denom, acc)
            o_ref[rows, cs] = acc.astype(o_ref.dtype)


def swa_prompt(q, kv_rep, sinks, bias_tabs, bsz, t):
    tr = SWA_BLOCKS * WINDOW
    ns = t // tr
    assert t % tr == 0
    cur = lambda c: pl.BlockSpec((tr, D_MODEL), lambda b, n, c=c: (b * ns + n, c))
    prev = lambda c: pl.BlockSpec(
        (WINDOW, D_MODEL),
        lambda b, n, c=c: (jnp.maximum((b * ns + n) * SWA_BLOCKS - 1, 0), c))
    table = lambda first: pl.BlockSpec(
        (None, B_HEADS, WINDOW, WINDOW),
        (lambda b, n: (jnp.minimum(n, 1), 0, 0, 0)) if first else (lambda b, n: (1, 0, 0, 0)))
    return pl.pallas_call(
        _swa_prompt_body,
        grid=(bsz, ns),
        in_specs=[pl.BlockSpec(memory_space=pltpu.SMEM),
                  cur(0), prev(0), cur(0), prev(1), cur(1), table(True), table(False)],
        out_specs=pl.BlockSpec((tr, D_MODEL), lambda b, n: (b * ns + n, 0)),
        out_shape=jax.ShapeDtypeStruct((bsz * t, D_MODEL), BF16),
        compiler_params=_params(("parallel", "arbitrary")),
        name="swa_prompt",
    )(sinks, q, kv_rep, kv_rep, kv_rep, kv_rep, bias_tabs, bias_tabs)


def _swa_step_body(*refs, shift_cache):
    sink_ref, qz_ref, k_ref, v_ref, bias_ref, fold_ref = refs[:6]
    if shift_cache:
        kn_ref, vn_ref, o_ref, ko_ref, vo_ref = refs[6:]
        last = lax.broadcasted_iota(jnp.int32, (WINDOW, KV_WIDTH), 0) == WINDOW - 1
    else:
        (o_ref,) = refs[6:]
    sink = sink_ref[...] * LOG2E
    head_kv = lax.broadcasted_iota(jnp.int32, (B_HEADS, KV_WIDTH), 0) >> GROUPS_SHIFT
    lane_kv = lax.broadcasted_iota(jnp.int32, (B_HEADS, KV_WIDTH), 1) >> HEAD_DIM_SHIFT
    for b in range(ATTN_TOKENS):
        if shift_cache:
            k = jnp.where(last, kn_ref[b], pltpu.roll(k_ref[b], WINDOW - 1, 0))
            v = jnp.where(last, vn_ref[b], pltpu.roll(v_ref[b], WINDOW - 1, 0))
            ko_ref[b] = k
            vo_ref[b] = v
        else:
            k, v = k_ref[b], v_ref[b]
        qz = qz_ref[b].astype(BF16)
        logits = lax.dot_general(qz, k.astype(BF16), _TN, preferred_element_type=F32)
        logits = logits + bias_ref[...]
        m = jnp.maximum(jnp.max(logits, axis=-1, keepdims=True), sink)
        p = jnp.exp2(logits - m)
        denom = jnp.sum(p, axis=-1, keepdims=True) + jnp.exp2(sink - m)
        pv = jnp.dot(p.astype(BF16), v.astype(BF16), preferred_element_type=F32) / denom
        own = jnp.where(head_kv == lane_kv, pv, 0.0).astype(BF16)
        o_ref[b] = jnp.dot(own, fold_ref[...], preferred_element_type=F32)


def swa_step(qz, k_cache, v_cache, sinks, bias_row, kv_new=None):
    bsz = qz.shape[0]
    nt = ATTN_TOKENS
    assert bsz % nt == 0
    fold = np.zeros((KV_WIDTH, B_HEAD_DIM), np.float32)
    fold[np.arange(KV_WIDTH), np.arange(KV_WIDTH) % B_HEAD_DIM] = 1.0
    cache_blk = pl.BlockSpec((nt, WINDOW, KV_WIDTH), lambda b: (b, 0, 0))
    in_specs = [pl.BlockSpec((B_HEADS, 1), lambda b: (0, 0)),
                pl.BlockSpec((nt, B_HEADS, KV_WIDTH), lambda b: (b, 0, 0)),
                cache_blk, cache_blk,
                pl.BlockSpec((B_HEADS, WINDOW), lambda b: (0, 0)),
                pl.BlockSpec((KV_WIDTH, B_HEAD_DIM), lambda b: (0, 0))]
    args = [sinks.reshape(B_HEADS, 1), qz, k_cache, v_cache, bias_row, jnp.asarray(fold, BF16)]
    out_specs = [pl.BlockSpec((nt, B_HEADS, B_HEAD_DIM), lambda b: (b, 0, 0))]
    out_shape = [jax.ShapeDtypeStruct((bsz, B_HEADS, B_HEAD_DIM), F32)]
    if kv_new is not None:
        row_blk = pl.BlockSpec((nt, 1, KV_WIDTH), lambda b: (b, 0, 0))
        in_specs += [row_blk, row_blk]
        args += list(kv_new)
        out_specs += [cache_blk, cache_blk]
        out_shape += [jax.ShapeDtypeStruct(k_cache.shape, F32)] * 2
    outs = pl.pallas_call(
        functools.partial(_swa_step_body, shift_cache=kv_new is not None),
        grid=(bsz // nt,),
        in_specs=in_specs,
        out_specs=out_specs,
        out_shape=out_shape,
        compiler_params=_params(("parallel",)),
        name="swa_step",
    )(*args)
    return outs if kv_new is not None else outs[0]


def _trunk(x, mods, kv_mod, per_token, hgrn_state0, k_buf, v_buf, wts, lbs, bias_tab, bias_row):
    (w_in_a, w_o_a, gnorm_a, w_q_b, w_o_b, sinks_b, w_ffn_in, w_ffn_out, final_norm_w,
     w_kv) = wts
    prompt = k_buf is None
    bsz, t, _ = x.shape
    m = bsz * t
    common = dict(per_token=per_token, rows_per_batch=t)
    h = x.reshape(m, D_MODEL)
    states = []
    step_inputs = []
    new_state = None
    kv = k_state = v_state = kv_new = None
    for l in range(DEPTH):
        a1, s1, g1, a2, s2, g2 = mods[l]
        if l < N_A_LAYERS:
            qkvg, lf = norm_proj(h, (a1, s1), w_in_a, l, epilogue="hgrn",
                                 lb=lbs[l].reshape(1, D_MODEL), **common)
            gn = gnorm_a[l].reshape(1, A_DV)
            if prompt:
                mix, s_new = hgrn_scan_prompt(qkvg, lf, gn, bsz, t)
                states.append(s_new)
            else:
                qkvg = qkvg.astype(F32)
                last_a = l == N_A_LAYERS - 1
                mix, new_state = hgrn_step(qkvg, lf, gn, hgrn_state0, l,
                                           earlier=list(step_inputs) if last_a else None)
                step_inputs.append((qkvg, lf))
            w_o, lo = w_o_a, l
        else:
            j = l - N_A_LAYERS
            scale = LOG2E / math.sqrt(B_HEAD_DIM)
            if prompt:
                q = norm_proj(h, (a1, s1), w_q_b, j, out_scale=scale, **common)
                mix = swa_prompt(q, kv, sinks_b[j], bias_tab, bsz, t)
            else:
                q = norm_proj(h, (a1, s1), w_q_b, j, out_scale=scale, out_dtype=F32, **common)
                lane_kv = (np.arange(KV_WIDTH) // B_HEAD_DIM)[None, :]
                head_kv = (np.arange(B_HEADS) // B_GROUPS)[:, None]
                qz = jnp.where((lane_kv == head_kv)[None],
                               jnp.tile(q.reshape(m, B_HEADS, B_HEAD_DIM), (1, 1, B_KV_HEADS)), 0.0)
                if j == 0:
                    mix, k_state, v_state = swa_step(qz, k_buf.reshape(m, WINDOW, KV_WIDTH),
                                                     v_buf.reshape(m, WINDOW, KV_WIDTH),
                                                     sinks_b[j], bias_row, kv_new)
                else:
                    mix = swa_step(qz, k_state, v_state, sinks_b[j], bias_row)
                mix = mix.reshape(m, D_MODEL)
            w_o, lo = w_o_b, j
        h = post_ffn(mix, h, g1, (a2, s2), g2, w_o, lo, w_ffn_in, w_ffn_out, l,
                     final_w=final_norm_w if l == DEPTH - 1 else None, **common)
        if l == N_A_LAYERS - 1:
            if prompt:
                kv = norm_proj(h, kv_mod, w_kv, 0, epilogue="kv_rep", **common)
                tail = h.reshape(bsz, t, D_MODEL)[:, -WINDOW:].reshape(bsz * WINDOW, D_MODEL)
                kv_tail = norm_proj(tail, kv_mod, w_kv, 0, per_token=False,
                                    rows_per_batch=WINDOW, out_dtype=F32)
                kv_tail = kv_tail.reshape(bsz, WINDOW, 2, B_KV_HEADS, B_HEAD_DIM)
                k_state, v_state = kv_tail[:, :, 0], kv_tail[:, :, 1]
            else:
                kv_row = norm_proj(h, kv_mod, w_kv, 0, out_dtype=F32, **common)
                kv_new = (kv_row[:, :KV_WIDTH].reshape(m, 1, KV_WIDTH),
                          kv_row[:, KV_WIDTH:].reshape(m, 1, KV_WIDTH))
    y = h.reshape(bsz, t, D_MODEL)
    if prompt:
        return y, jnp.stack(states, axis=1), k_state, v_state
    cache_shape = (m, WINDOW, B_KV_HEADS, B_HEAD_DIM)
    return y, new_state, k_state.reshape(cache_shape), v_state.reshape(cache_shape)


def kernel(x_prompt, x_sample, state_hgrn, cache_swa_k, cache_swa_v, c_prompt, c_sample,
           w_in_a, w_o_a, gnorm_a, lb_a, w_kv, w_ada_kv, b_ada_kv, kv_norm_w, w_q_b, w_o_b,
           sinks_b, rel_bias, norm_w, w_ada, b_ada, w_ffn_in, w_ffn_out, final_norm_w):
    n_p, n_s = c_prompt.shape[0], c_sample.shape[0]
    rows = -(-(n_p + n_s) // 8) * 8
    c_all = jnp.concatenate(
        [c_prompt, c_sample, jnp.zeros((rows - n_p - n_s, D_MODEL), F32)], axis=0)

    zeros, ones = jnp.zeros((DEPTH, D_MODEL), F32), jnp.ones((DEPTH, D_MODEL), F32)
    p_ada = jnp.stack([zeros, norm_w[:, 0], zeros, zeros, norm_w[:, 1], zeros], axis=1)
    q_ada = jnp.stack([ones, norm_w[:, 0], ones, ones, norm_w[:, 1], ones], axis=1)
    ada = ada_project(c_all, w_ada, b_ada.reshape(DEPTH, 6, 1, D_MODEL),
                      p_ada.reshape(DEPTH, 6, 1, D_MODEL), q_ada.reshape(DEPTH, 6, 1, D_MODEL))
    p_kv = jnp.stack([zeros[0], kv_norm_w]).reshape(1, 2, 1, D_MODEL)
    q_kv = jnp.stack([ones[0], kv_norm_w]).reshape(1, 2, 1, D_MODEL)
    ada_kv = ada_project(c_all, w_ada_kv.reshape(1, D_MODEL, 2 * D_MODEL),
                         b_ada_kv.reshape(1, 2, 1, D_MODEL), p_kv, q_kv)

    def split_mods(arr, n_chunks, lo, hi, per_token):
        out = []
        for c in range(n_chunks):
            v = arr[lo:hi, c * D_MODEL:(c + 1) * D_MODEL]
            out.append(v if per_token else v.reshape(hi - lo, 1, D_MODEL))
        return out

    def mods_for(lo, hi, per_token):
        layers = []
        for l in range(DEPTH):
            sh1, a1, g1, sh2, a2, g2 = split_mods(ada[l], 6, lo, hi, per_token)
            layers.append((a1, sh1, g1, a2, sh2, g2))
        sh_kv, a_kv = split_mods(ada_kv[0], 2, lo, hi, per_token)
        return layers, (a_kv, sh_kv)

    lb_sm = jax.nn.softmax(lb_a.astype(F32), axis=0)
    lbs = jnp.cumsum(lb_sm, axis=0) - lb_sm[0:1]

    rb = rel_bias.astype(F32)[_T5_BUCKETS]
    t_idx, j_idx = np.arange(WINDOW)[:, None], np.arange(WINDOW)[None, :]
    from_prev = j_idx > t_idx
    dist = np.where(from_prev, t_idx + WINDOW - j_idx, t_idx - j_idx)
    onehot = (jnp.asarray(dist)[:, :, None] == jnp.arange(WINDOW)[None, None, :]).astype(F32)
    tab = jnp.einsum("tjd,dh->htj", onehot, rb, precision=lax.Precision.HIGHEST) * LOG2E
    bias_tab = jnp.stack([jnp.where(from_prev, MASK_VALUE, tab),
                          tab])
    bias_row = rb[::-1].T * LOG2E

    bf = lambda w: w.astype(BF16)
    wts = (bf(w_in_a), bf(w_o_a), gnorm_a, bf(w_q_b), bf(w_o_b), sinks_b, bf(w_ffn_in),
           bf(w_ffn_out), final_norm_w, bf(w_kv)[None])
    mods_p, kv_mod_p = mods_for(0, n_p, False)
    mods_s, kv_mod_s = mods_for(n_p, n_p + n_s, True)
    y_p, st_p, k_p, v_p = _trunk(x_prompt, mods_p, kv_mod_p, False, None, None, None, wts, lbs,
                                 bias_tab, bias_row)
    y_s, st_s, k_s, v_s = _trunk(x_sample, mods_s, kv_mod_s, True, state_hgrn, cache_swa_k,
                                 cache_swa_v, wts, lbs, bias_tab, bias_row)
    return (y_p, y_s, st_p, st_s, k_p, v_p, k_s, v_s)
```

```python
import functools
import math

import numpy as np
import jax
import jax.numpy as jnp
from jax import lax
from jax.experimental import pallas as pl
from jax.experimental.pallas import tpu as pltpu

F32 = jnp.float32
BF16 = jnp.bfloat16

D_MODEL = 1024
DEPTH = 4
N_A_LAYERS = 2
A_HEADS = 8
A_DK = 128
A_DV = 128
F_MIN = 1e-30
B_HEAD_DIM = 64
B_HEADS = 16
B_KV_HEADS = 4
B_GROUPS = 4
WINDOW = 128
MASK_VALUE = -1e30
N_BUCKETS = 32
MAX_DISTANCE = 128
D_FF = 2816
EPS = 1e-6
GROUP_LANES = B_GROUPS * B_HEAD_DIM
KV_WIDTH = B_KV_HEADS * B_HEAD_DIM
HEAD_DIM_SHIFT = B_HEAD_DIM.bit_length() - 1
GROUPS_SHIFT = B_GROUPS.bit_length() - 1

CHUNK = 128
N_LEVELS = 7
ROW_TILE = 1024
COL_TILE = 1024
FF_TILE = 256
STEP_TOKENS = 4
ATTN_TOKENS = 8
VMEM_LIMIT_BYTES = 58 * 1024 * 1024


def _params(sem):
    return pltpu.CompilerParams(dimension_semantics=sem, vmem_limit_bytes=VMEM_LIMIT_BYTES)


def _row_tile(m, per_token, rows_per_batch):
    return min(m, ROW_TILE) if per_token else min(m, ROW_TILE, rows_per_batch)


def _sigmoid(x):
    return 0.5 * jnp.tanh(0.5 * x) + 0.5


def _silu(x):
    h = 0.5 * x
    return h * jnp.tanh(h) + h


def _norm_mod(x, a, s):
    y = x * lax.rsqrt(jnp.mean(x * x, axis=-1, keepdims=True) + EPS)
    return (y * a + s).astype(BF16)


def _resident(shape, layer):
    return pl.BlockSpec((None,) + shape, lambda i: (layer,) + (0,) * len(shape),
                        pipeline_mode=pl.Buffered(1))


def _mod_operand(mod, tm, per_token, rows_per_batch):
    arr, layer, chunk, row0 = mod
    if per_token:
        return arr, pl.BlockSpec((None, tm, D_MODEL), lambda i: (layer, i, chunk))
    n_l, r, w = arr.shape
    spec = pl.BlockSpec((None, 1, 1, D_MODEL),
                        lambda i: (layer, row0 + (i * tm) // rows_per_batch, 0, chunk))
    return arr.reshape(n_l, r, 1, w), spec


def _row_value(ref, per_token):
    return ref[...] if per_token else ref[0]


def _norm_proj_body(*refs, per_token, epilogue, out_scale):
    x_ref, a_ref, s_ref, w_ref = refs[:4]
    xn = _norm_mod(x_ref[...], _row_value(a_ref, per_token), _row_value(s_ref, per_token))

    tn = min(COL_TILE, w_ref.shape[1])

    def proj(c):
        return jnp.dot(xn, w_ref[:, c:c + tn], preferred_element_type=F32)

    if epilogue == "plain":
        (o_ref,) = refs[4:]
        for c in range(0, o_ref.shape[1], tn):
            acc = proj(c)
            if out_scale is not None:
                acc = acc * out_scale
            o_ref[:, c:c + tn] = acc.astype(o_ref.dtype)
    elif epilogue == "kv_rep":
        (o_ref,) = refs[4:]
        acc = proj(0)
        low_half = lax.broadcasted_iota(jnp.int32, (acc.shape[0], 128), 1) < B_HEAD_DIM
        for c in range(acc.shape[1] // 128):
            x = acc[:, c * 128:(c + 1) * 128]
            swapped = pltpu.roll(x, B_HEAD_DIM, 1)
            for half, rep in enumerate((jnp.where(low_half, x, swapped),
                                        jnp.where(low_half, swapped, x))):
                rep = rep.astype(o_ref.dtype)
                base = (2 * c + half) * GROUP_LANES
                o_ref[:, base:base + 128] = rep
                o_ref[:, base + 128:base + 256] = rep
    elif epilogue == "hgrn":
        lb_ref, o_ref, lf_ref = refs[4:]
        d = D_MODEL
        lb = lb_ref[...]
        o_ref[:, 0:d] = _silu(proj(0)).astype(o_ref.dtype)
        f = lb + (1.0 - lb) * _sigmoid(proj(d))
        o_ref[:, d:2 * d] = (1.0 - f).astype(o_ref.dtype)
        lf_ref[...] = jnp.log2(jnp.maximum(f, F_MIN))
        o_ref[:, 3 * d:4 * d] = _silu(proj(3 * d)).astype(o_ref.dtype)
        o_ref[:, 2 * d:3 * d] = proj(2 * d).astype(o_ref.dtype)
    else:
        raise ValueError(epilogue)


def norm_proj(x, mod, w, layer, *, per_token, rows_per_batch, epilogue="plain", lb=None,
              out_dtype=BF16, out_scale=None):
    m, k = x.shape
    n = w.shape[-1]
    tm = _row_tile(m, per_token, rows_per_batch)
    assert m % tm == 0 and (per_token or rows_per_batch % tm == 0)
    (a_arr, a_spec), (s_arr, s_spec) = [_mod_operand(v, tm, per_token, rows_per_batch) for v in mod]
    in_specs = [pl.BlockSpec((tm, k), lambda i: (i, 0)), a_spec, s_spec, _resident((k, n), layer)]
    args = [x, a_arr, s_arr, w]
    n_out = n * B_GROUPS if epilogue == "kv_rep" else n
    out_specs = [pl.BlockSpec((tm, n_out), lambda i: (i, 0))]
    out_shape = [jax.ShapeDtypeStruct((m, n_out), out_dtype)]
    if epilogue == "hgrn":
        in_specs.append(pl.BlockSpec((1, D_MODEL), lambda i: (0, 0)))
        args.append(lb)
        out_specs.append(pl.BlockSpec((tm, D_MODEL), lambda i: (i, 0)))
        out_shape.append(jax.ShapeDtypeStruct((m, D_MODEL), F32))
    outs = pl.pallas_call(
        functools.partial(_norm_proj_body, per_token=per_token, epilogue=epilogue,
                          out_scale=out_scale),
        grid=(m // tm,),
        in_specs=in_specs,
        out_specs=out_specs,
        out_shape=out_shape,
        compiler_params=_params(("parallel",)),
        name=f"norm_proj_{epilogue}_m{m}_n{n}",
    )(*args)
    return outs if epilogue == "hgrn" else outs[0]


def _post_ffn_body(*refs, per_token, final_norm):
    (mix_ref, h_ref, g1_ref, a2_ref, s2_ref, g2_ref, wo_ref, win_ref, wout_ref) = refs[:9]
    rest = refs[9:]
    fw_ref = rest[0] if final_norm else None
    o_ref, act_ref = rest[-2:]
    row = functools.partial(_row_value, per_token=per_token)
    h_mid = h_ref[...] + row(g1_ref) * jnp.dot(mix_ref[...].astype(BF16), wo_ref[...],
                                               preferred_element_type=F32)
    xn = _norm_mod(h_mid, row(a2_ref), row(s2_ref))
    for c in range(0, D_FF, FF_TILE):
        gate = jnp.dot(xn, win_ref[:, c:c + FF_TILE], preferred_element_type=F32)
        up = jnp.dot(xn, win_ref[:, D_FF + c:D_FF + c + FF_TILE], preferred_element_type=F32)
        act_ref[:, c:c + FF_TILE] = (_silu(gate) * up).astype(BF16)
    out = h_mid + row(g2_ref) * jnp.dot(act_ref[...], wout_ref[...], preferred_element_type=F32)
    if final_norm:
        out = out * lax.rsqrt(jnp.mean(out * out, axis=-1, keepdims=True) + EPS) * fw_ref[...]
    o_ref[...] = out


def post_ffn(mix, h, g1, mod2, g2, w_o, lo, w_ffn_in, w_ffn_out, lf, *, per_token,
             rows_per_batch, final_w=None):
    m, d = h.shape
    tm = _row_tile(m, per_token, rows_per_batch)
    assert m % tm == 0 and (per_token or rows_per_batch % tm == 0)
    rows = [_mod_operand(v, tm, per_token, rows_per_batch) for v in (g1, mod2[0], mod2[1], g2)]
    tile = pl.BlockSpec((tm, d), lambda i: (i, 0))
    in_specs = [tile, tile] + [spec for _, spec in rows] + [
        _resident((d, d), lo), _resident((d, 2 * D_FF), lf), _resident((D_FF, d), lf)]
    args = [mix, h] + [arr for arr, _ in rows] + [w_o, w_ffn_in, w_ffn_out]
    if final_w is not None:
        in_specs.append(pl.BlockSpec((1, d), lambda i: (0, 0)))
        args.append(final_w.reshape(1, d))
    return pl.pallas_call(
        functools.partial(_post_ffn_body, per_token=per_token, final_norm=final_w is not None),
        grid=(m // tm,),
        in_specs=in_specs,
        out_specs=tile,
        out_shape=jax.ShapeDtypeStruct((m, d), F32),
        scratch_shapes=[pltpu.VMEM((tm, D_FF), BF16)],
        compiler_params=_params(("parallel",)),
        name=f"post_ffn_m{m}",
    )(*args)


def _ada_body(c_ref, w_ref, b_ref, p_ref, q_ref, o_ref):
    c = _silu(c_ref[...]).astype(BF16)
    acc = jnp.dot(c, w_ref[...].astype(BF16), preferred_element_type=F32)
    o_ref[...] = p_ref[...] + q_ref[...] * (acc + b_ref[...])


def ada_project(c_all, w, b, p, q):
    n_l, _, n_cols = w.shape
    n_c = n_cols // D_MODEL
    rows = c_all.shape[0]
    vec_spec = pl.BlockSpec((None, None, 1, D_MODEL), lambda l, j: (l, j, 0, 0))
    return pl.pallas_call(
        _ada_body,
        grid=(n_l, n_c),
        in_specs=[
            pl.BlockSpec((rows, D_MODEL), lambda l, j: (0, 0)),
            pl.BlockSpec((None, D_MODEL, D_MODEL), lambda l, j: (l, 0, j)),
            vec_spec, vec_spec, vec_spec,
        ],
        out_specs=pl.BlockSpec((None, rows, D_MODEL), lambda l, j: (l, 0, j)),
        out_shape=jax.ShapeDtypeStruct((n_l, rows, n_cols), F32),
        compiler_params=_params(("parallel", "parallel")),
        name="ada_project",
    )(c_all, w, b, p, q)


MATMUL_LEVELS = (5, 6)


def _scan_tables():
    c = CHUNK
    t = np.arange(c)[:, None]
    u = np.arange(c)[None, :]
    blocks = [(u <= t)]
    for level in MATMUL_LEVELS:
        p = N_LEVELS - level
        odd = ((t >> p) & 1) == 1
        start = (t >> p) << p
        end = (((t >> p) + 1) << p) - 1
        blocks.append(np.where(odd, (u >= start) & (u <= t), (u > t) & (u <= end)))
    sums = np.concatenate(blocks, axis=0).astype(np.float32)
    x = np.arange(c)[:, None] ^ np.arange(c)[None, :]
    msb = np.floor(np.log2(np.maximum(x, 1))).astype(np.int32)
    lvl = np.where(x == 0, 0, N_LEVELS - msb)
    lvl = np.where(np.arange(c)[None, :] > np.arange(c)[:, None], -1, lvl).astype(np.int32)
    return sums, lvl


_SCAN_SUMS, _SCAN_LEVELS = _scan_tables()

_TN = (((1,), (1,)), ((), ()))
_TM = (((0,), (0,)), ((), ()))


HEAD_GROUP = 8
STEP_CHUNKS = 4


def _level_exponent(ex, log2_f, odd_rows, cols, level):
    m = CHUNK >> level
    if level in MATMUL_LEVELS:
        i = 1 + MATMUL_LEVELS.index(level)
        return ex[i * CHUNK:(i + 1) * CHUNK, cols]
    if m == 1:
        return jnp.where(odd_rows, log2_f[:, cols], 0.0)
    parts = []
    for i in range(CHUNK // m):
        blk = ex[i * m:(i + 1) * m, cols]
        if i & 1:
            parts.append(blk - ex[i * m - 1:i * m, cols])
        else:
            parts.append(ex[(i + 1) * m - 1:(i + 1) * m, cols] - blk)
    return jnp.concatenate(parts, axis=0)


def _level_mix(q, kk, odd_rows, level):
    m = CHUNK >> level
    if m >= 8:
        parts = [(q if (i & 1) else kk)[i * m:(i + 1) * m] for i in range(CHUNK // m)]
        return jnp.concatenate(parts, axis=0)
    return jnp.where(odd_rows, q, kk)


def _odd_blocks(x, m):
    return jnp.concatenate([x[i * m:(i + 1) * m] for i in range(1, CHUNK // m, 2)], axis=0)


def _level_update(att, mix, lvl, level):
    m = CHUNK >> level
    if m < 8:
        a = lax.dot_general(mix, mix, _TN, preferred_element_type=F32)
        return jnp.where(lvl == level, a, att)
    lhs = _odd_blocks(mix, m) if m >= 16 else mix
    a = lax.dot_general(lhs, mix, _TN, preferred_element_type=F32)
    if m < 16:
        a = _odd_blocks(a, m)
    tiles = []
    for i in range(CHUNK // m):
        rows = slice(i * m, (i + 1) * m)
        if i & 1:
            a_rows = a[(i // 2) * m:(i // 2 + 1) * m]
            tiles.append(jnp.where(lvl[rows] == level, a_rows, att[rows]))
        else:
            tiles.append(att[rows])
    return jnp.concatenate(tiles, axis=0)


def _hgrn_scan_body(q_ref, k_ref, v_ref, g_ref, lf_ref, gn_ref, sums_ref, lvl_ref,
                    o_ref, st_ref, state_ref, ex_ref):
    n = pl.program_id(1)

    @pl.when(n == 0)
    def _():
        state_ref[...] = jnp.zeros_like(state_ref)

    lvl = lvl_ref[...]
    row = lax.broadcasted_iota(jnp.int32, (CHUNK, A_DK), 0)
    gn = gn_ref[...]
    heads = [slice(h * A_DK, (h + 1) * A_DK) for h in range(A_HEADS)]
    for c in range(STEP_CHUNKS):
        rows = slice(c * CHUNK, (c + 1) * CHUNK)
        ex = ex_ref.at[c]
        log2_f = lf_ref[rows, :]
        hi = log2_f.astype(BF16)
        rem = log2_f - hi.astype(F32)
        mid = rem.astype(BF16)
        lo = (rem - mid.astype(F32)).astype(BF16)
        ex[...] = jnp.dot(sums_ref[...], jnp.concatenate([hi, mid, lo], axis=0),
                          preferred_element_type=F32)
        b_end = ex[CHUNK - 1:CHUNK, :]
        d_end = jnp.exp2(b_end)
        for g0 in range(0, A_HEADS, HEAD_GROUP):
            group = list(range(g0, g0 + HEAD_GROUP))
            q_bf = {h: q_ref[rows, heads[h]] for h in group}
            k_bf = {h: k_ref[rows, heads[h]] for h in group}
            q = {h: q_bf[h].astype(F32) for h in group}
            kk = {h: k_bf[h].astype(F32) for h in group}
            att = {h: jnp.where(lvl == 0, lax.dot_general(q_bf[h], k_bf[h], _TN,
                                                          preferred_element_type=F32), 0.0)
                   for h in group}
            for level in range(1, N_LEVELS + 1):
                odd_rows = ((row >> (N_LEVELS - level)) & 1) == 1
                for h in group:
                    e = jnp.exp2(_level_exponent(ex, log2_f, odd_rows, heads[h], level))
                    mix = (_level_mix(q[h], kk[h], odd_rows, level) * e).astype(BF16)
                    att[h] = _level_update(att[h], mix, lvl, level)
            for h in group:
                sl = heads[h]
                b_incl = ex[0:CHUNK, sl]
                q_dec = (q[h] * jnp.exp2(b_incl)).astype(BF16)
                k_dec = (kk[h] * jnp.exp2(b_end[:, sl] - b_incl)).astype(BF16)
                v = v_ref[rows, sl]
                s_t = state_ref[h]
                o = (lax.dot_general(q_dec, s_t.astype(BF16), _TN, preferred_element_type=F32)
                     + jnp.dot(att[h].astype(BF16), v, preferred_element_type=F32))
                state_ref[h] = d_end[:, sl] * s_t + lax.dot_general(
                    v, k_dec, _TM, preferred_element_type=F32)
                y = o * lax.rsqrt(jnp.mean(o * o, axis=-1, keepdims=True) + EPS)
                o_ref[rows, sl] = (y * gn * g_ref[rows, sl].astype(F32)).astype(o_ref.dtype)

    @pl.when(n == pl.num_programs(1) - 1)
    def _():
        for h in range(A_HEADS):
            st_ref[0, h] = state_ref[h].T


def hgrn_scan_prompt(qkvg, lf, gn, bsz, t):
    tr = STEP_CHUNKS * CHUNK
    ns = t // tr
    assert t % tr == 0
    sec = lambda k: pl.BlockSpec((tr, D_MODEL), lambda b, n, k=k: (b * ns + n, k))
    n_sum = _SCAN_SUMS.shape[0]
    return pl.pallas_call(
        _hgrn_scan_body,
        grid=(bsz, ns),
        in_specs=[sec(0), sec(1), sec(2), sec(3), sec(0),
                  pl.BlockSpec((1, A_DV), lambda b, n: (0, 0)),
                  pl.BlockSpec((n_sum, 3 * CHUNK), lambda b, n: (0, 0)),
                  pl.BlockSpec((CHUNK, CHUNK), lambda b, n: (0, 0))],
        out_specs=[pl.BlockSpec((tr, D_MODEL), lambda b, n: (b * ns + n, 0)),
                   pl.BlockSpec((1, A_HEADS, A_DK, A_DV), lambda b, n: (b, 0, 0, 0))],
        out_shape=[jax.ShapeDtypeStruct((bsz * t, D_MODEL), BF16),
                   jax.ShapeDtypeStruct((bsz, A_HEADS, A_DK, A_DV), F32)],
        scratch_shapes=[pltpu.VMEM((A_HEADS, A_DV, A_DK), F32),
                        pltpu.VMEM((STEP_CHUNKS, n_sum, D_MODEL), F32)],
        compiler_params=_params(("parallel", "arbitrary")),
        name="hgrn_scan",
    )(qkvg, qkvg, qkvg, qkvg, lf, gn, jnp.asarray(np.tile(_SCAN_SUMS, (1, 3)), BF16),
      jnp.asarray(_SCAN_LEVELS))


STEP_ROWS = 16


def _step_columns(decay, kk, q, v_row):
    r = lax.broadcasted_iota(jnp.int32, (STEP_ROWS, A_DK), 0)
    hi = decay.astype(BF16).astype(F32)
    rem = decay - hi
    mid = rem.astype(BF16).astype(F32)
    lo = rem - mid
    lhs = jnp.where(r == 0, hi, jnp.where(r == 1, mid, jnp.where(r == 2, lo,
                                                                 jnp.where(r == 3, kk, 0.0))))
    parts = [jnp.where(r < 3, 1.0, 0.0), jnp.where(r == 3, v_row, 0.0)]
    if q is not None:
        lhs = jnp.where(r == 4, q, lhs)
        parts.append(jnp.where(r == 4, 1.0, 0.0))
    rhs = jnp.concatenate(parts, axis=1).astype(BF16)
    out = lax.dot_general(lhs.astype(BF16), rhs, _TM, preferred_element_type=F32)
    return [out[:, i * A_DV:(i + 1) * A_DV] for i in range(len(parts))]


def _hgrn_step_body(*refs, n_replay, write_state):
    q_ref, k_ref, v_ref, g_ref, lf_ref, gn_ref, s_ref = refs[:7]
    replay = [refs[7 + 4 * i:11 + 4 * i] for i in range(n_replay)]
    outs = refs[7 + 4 * n_replay:]
    o_ref = outs[0]
    gn = gn_ref[...]
    for b in range(STEP_TOKENS):
        decay = jnp.exp2(lf_ref[b])
        q, kk, v, gate = q_ref[b], k_ref[b], v_ref[b], g_ref[b]
        for h in range(A_HEADS):
            sl = slice(h * A_DK, (h + 1) * A_DK)
            if write_state:
                dec_m, kv_m, q_m = _step_columns(decay[:, sl], kk[:, sl], q[:, sl], v[:, sl])
                s_new = dec_m * s_ref[b, h] + kv_m
                outs[1][b, n_replay, h] = s_new
                o = jnp.sum(q_m * s_new, axis=0, keepdims=True)
            else:
                r = lax.broadcasted_iota(jnp.int32, (STEP_ROWS, A_DK), 0)
                lhs = jnp.where(r == 0, q[:, sl] * decay[:, sl], 0.0).astype(BF16)
                ones = jnp.where(r == 0, 1.0, 0.0).astype(BF16)
                qd_m = lax.dot_general(lhs, ones, _TM, preferred_element_type=F32)
                qk = jnp.sum(q[:, sl] * kk[:, sl], axis=-1, keepdims=True)
                o = jnp.sum(qd_m * s_ref[b, h], axis=0, keepdims=True) + qk * v[:, sl]
            y = o * lax.rsqrt(jnp.mean(o * o, axis=-1, keepdims=True) + EPS)
            o_ref[b, :, sl] = y * gn * gate[:, sl]
        for i, (kp_ref, vp_ref, lfp_ref, sp_ref) in enumerate(replay):
            decay_p = jnp.exp2(lfp_ref[b])
            for h in range(A_HEADS):
                sl = slice(h * A_DK, (h + 1) * A_DK)
                dec_m, kv_m = _step_columns(decay_p[:, sl], kp_ref[b][:, sl], None,
                                            vp_ref[b][:, sl])
                outs[1][b, i, h] = dec_m * sp_ref[b, h] + kv_m


def hgrn_step(qkvg, lf, gn, state, layer, earlier=None):
    bsz = qkvg.shape[0]
    nb = STEP_TOKENS
    assert bsz % nb == 0
    sec = lambda k: pl.BlockSpec((nb, 1, D_MODEL), lambda b, k=k: (b, 0, k))
    state_blk = lambda l: pl.BlockSpec((nb, None, A_HEADS, A_DK, A_DV),
                                       lambda b, l=l: (b, l, 0, 0, 0))
    as3 = lambda a: a.reshape(bsz, 1, a.shape[-1])
    in_specs = [sec(0), sec(1), sec(2), sec(3), sec(0),
                pl.BlockSpec((1, A_DV), lambda b: (0, 0)), state_blk(layer)]
    args = [as3(qkvg)] * 4 + [as3(lf), gn, state]
    write_state = earlier is not None
    for i, (qkvg_p, lf_p) in enumerate(earlier or ()):
        in_specs += [sec(1), sec(2), sec(0), state_blk(i)]
        args += [as3(qkvg_p), as3(qkvg_p), as3(lf_p), state]
    out_specs = [pl.BlockSpec((nb, 1, D_MODEL), lambda b: (b, 0, 0))]
    out_shape = [jax.ShapeDtypeStruct((bsz, 1, D_MODEL), F32)]
    if write_state:
        assert len(earlier) == layer
        n_l = layer + 1
        out_specs.append(pl.BlockSpec((nb, n_l, A_HEADS, A_DK, A_DV), lambda b: (b, 0, 0, 0, 0)))
        out_shape.append(jax.ShapeDtypeStruct((bsz, n_l, A_HEADS, A_DK, A_DV), F32))
    outs = pl.pallas_call(
        functools.partial(_hgrn_step_body, n_replay=len(earlier or ()), write_state=write_state),
        grid=(bsz // nb,),
        in_specs=in_specs,
        out_specs=out_specs,
        out_shape=out_shape,
        compiler_params=_params(("parallel",)),
        name="hgrn_step",
    )(*args)
    o = outs[0].reshape(bsz, D_MODEL)
    return (o, outs[1]) if write_state else (o, None)


def _t5_buckets():
    max_exact = N_BUCKETS // 2
    d = np.arange(WINDOW)
    large = max_exact + (np.log(np.maximum(d, 1).astype(np.float32) / max_exact)
                         / math.log(MAX_DISTANCE / max_exact)
                         * (N_BUCKETS - max_exact)).astype(np.int32)
    large = np.clip(large, 0, N_BUCKETS - 1)
    return np.where(d < max_exact, d, large).astype(np.int32)


_T5_BUCKETS = _t5_buckets()


LOG2E = math.log2(math.e)


SWA_BLOCKS = 4


def _swa_prompt_body(sink_ref, q_ref, kp_ref, kc_ref, vp_ref, vc_ref, bias0_ref, bias_ref, o_ref):
    lane_head = lax.broadcasted_iota(jnp.int32, (WINDOW, GROUP_LANES), 1) >> HEAD_DIM_SHIFT
    from_prev = (lax.broadcasted_iota(jnp.int32, (WINDOW, WINDOW), 1)
                 > lax.broadcasted_iota(jnp.int32, (WINDOW, WINDOW), 0))
    groups = [slice(g * GROUP_LANES, (g + 1) * GROUP_LANES) for g in range(B_KV_HEADS)]
    for blk in range(SWA_BLOCKS):
        rows = slice(blk * WINDOW, (blk + 1) * WINDOW)
        prev_rows = slice((blk - 1) * WINDOW, blk * WINDOW)
        table = bias0_ref if blk == 0 else bias_ref

        def keys(prev_ref, cur_ref, cs):
            prev = prev_ref[:, cs] if blk == 0 else cur_ref[prev_rows, cs]
            return jnp.concatenate([prev, cur_ref[rows, cs]], axis=0)

        all_logits = []
        for cs in groups:
            qg = q_ref[rows, cs].astype(F32)
            qstack = jnp.concatenate(
                [jnp.where(lane_head == j, qg, 0.0).astype(BF16) for j in range(B_GROUPS)], axis=0)
            all_logits.append(lax.dot_general(qstack, keys(kp_ref, kc_ref, cs), _TN,
                                              preferred_element_type=F32))
        for g, cs in enumerate(groups):
            logits = all_logits[g]
            vcat = keys(vp_ref, vc_ref, cs)
            acc = jnp.zeros((WINDOW, GROUP_LANES), F32)
            for j in range(B_GROUPS):
                head = g * B_GROUPS + j
                hr = slice(j * WINDOW, (j + 1) * WINDOW)
                lg = jnp.where(from_prev, logits[hr, :WINDOW], logits[hr, WINDOW:]) + table[head]
                sink = sink_ref[head] * LOG2E
                m = jnp.maximum(jnp.max(lg, axis=-1, keepdims=True), sink)
                p = jnp.exp2(lg - m)
                denom = jnp.sum(p, axis=-1, keepdims=True) + jnp.exp2(sink - m)
                p_prev = jnp.where(from_prev, p, 0.0)
                p_both = jnp.concatenate([p_prev, p - p_prev], axis=1).astype(BF16)
                pv = jnp.dot(p_both, vcat, preferred_element_type=F32)
                acc = jnp.where(lane_head == j, pv / denom, acc)
            o_ref[rows, cs] = acc.astype(o_ref.dtype)


def swa_prompt(q, kv_rep, sinks, bias_tabs, bsz, t):
    tr = SWA_BLOCKS * WINDOW
    ns = t // tr
    assert t % tr == 0
    cur = lambda c: pl.BlockSpec((tr, D_MODEL), lambda b, n, c=c: (b * ns + n, c))
    prev = lambda c: pl.BlockSpec(
        (WINDOW, D_MODEL),
        lambda b, n, c=c: (jnp.maximum((b * ns + n) * SWA_BLOCKS - 1, 0), c))
    table = lambda first: pl.BlockSpec(
        (None, B_HEADS, WINDOW, WINDOW),
        (lambda b, n: (jnp.minimum(n, 1), 0, 0, 0)) if first else (lambda b, n: (1, 0, 0, 0)))
    return pl.pallas_call(
        _swa_prompt_body,
        grid=(bsz, ns),
        in_specs=[pl.BlockSpec(memory_space=pltpu.SMEM),
                  cur(0), prev(0), cur(0), prev(1), cur(1), table(True), table(False)],
        out_specs=pl.BlockSpec((tr, D_MODEL), lambda b, n: (b * ns + n, 0)),
        out_shape=jax.ShapeDtypeStruct((bsz * t, D_MODEL), BF16),
        compiler_params=_params(("parallel", "arbitrary")),
        name="swa_prompt",
    )(sinks, q, kv_rep, kv_rep, kv_rep, kv_rep, bias_tabs, bias_tabs)


def _swa_step_body(*refs, shift_cache):
    sink_ref, qz_ref, k_ref, v_ref, bias_ref, fold_ref = refs[:6]
    if shift_cache:
        kn_ref, vn_ref, o_ref, ko_ref, vo_ref = refs[6:]
        last = lax.broadcasted_iota(jnp.int32, (WINDOW, KV_WIDTH), 0) == WINDOW - 1
    else:
        (o_ref,) = refs[6:]
    sink = sink_ref[...] * LOG2E
    head_kv = lax.broadcasted_iota(jnp.int32, (B_HEADS, KV_WIDTH), 0) >> GROUPS_SHIFT
    lane_kv = lax.broadcasted_iota(jnp.int32, (B_HEADS, KV_WIDTH), 1) >> HEAD_DIM_SHIFT
    for b in range(ATTN_TOKENS):
        if shift_cache:
            k = jnp.where(last, kn_ref[b], pltpu.roll(k_ref[b], WINDOW - 1, 0))
            v = jnp.where(last, vn_ref[b], pltpu.roll(v_ref[b], WINDOW - 1, 0))
            ko_ref[b] = k
            vo_ref[b] = v
        else:
            k, v = k_ref[b], v_ref[b]
        qz = qz_ref[b].astype(BF16)
        logits = lax.dot_general(qz, k.astype(BF16), _TN, preferred_element_type=F32)
        logits = logits + bias_ref[...]
        m = jnp.maximum(jnp.max(logits, axis=-1, keepdims=True), sink)
        p = jnp.exp2(logits - m)
        denom = jnp.sum(p, axis=-1, keepdims=True) + jnp.exp2(sink - m)
        pv = jnp.dot(p.astype(BF16), v.astype(BF16), preferred_element_type=F32) / denom
        own = jnp.where(head_kv == lane_kv, pv, 0.0).astype(BF16)
        o_ref[b] = jnp.dot(own, fold_ref[...], preferred_element_type=F32)


def swa_step(qz, k_cache, v_cache, sinks, bias_row, kv_new=None):
    bsz = qz.shape[0]
    nt = ATTN_TOKENS
    assert bsz % nt == 0
    fold = np.zeros((KV_WIDTH, B_HEAD_DIM), np.float32)
    fold[np.arange(KV_WIDTH), np.arange(KV_WIDTH) % B_HEAD_DIM] = 1.0
    cache_blk = pl.BlockSpec((nt, WINDOW, KV_WIDTH), lambda b: (b, 0, 0))
    in_specs = [pl.BlockSpec((B_HEADS, 1), lambda b: (0, 0)),
                pl.BlockSpec((nt, B_HEADS, KV_WIDTH), lambda b: (b, 0, 0)),
                cache_blk, cache_blk,
                pl.BlockSpec((B_HEADS, WINDOW), lambda b: (0, 0)),
                pl.BlockSpec((KV_WIDTH, B_HEAD_DIM), lambda b: (0, 0))]
    args = [sinks.reshape(B_HEADS, 1), qz, k_cache, v_cache, bias_row, jnp.asarray(fold, BF16)]
    out_specs = [pl.BlockSpec((nt, B_HEADS, B_HEAD_DIM), lambda b: (b, 0, 0))]
    out_shape = [jax.ShapeDtypeStruct((bsz, B_HEADS, B_HEAD_DIM), F32)]
    if kv_new is not None:
        row_blk = pl.BlockSpec((nt, 1, KV_WIDTH), lambda b: (b, 0, 0))
        in_specs += [row_blk, row_blk]
        args += list(kv_new)
        out_specs += [cache_blk, cache_blk]
        out_shape += [jax.ShapeDtypeStruct(k_cache.shape, F32)] * 2
    outs = pl.pallas_call(
        functools.partial(_swa_step_body, shift_cache=kv_new is not None),
        grid=(bsz // nt,),
        in_specs=in_specs,
        out_specs=out_specs,
        out_shape=out_shape,
        compiler_params=_params(("parallel",)),
        name="swa_step",
    )(*args)
    return outs if kv_new is not None else outs[0]


def _trunk(x, mods, kv_mod, per_token, hgrn_state0, k_buf, v_buf, wts, lbs, bias_tab, bias_row):
    (w_in_a, w_o_a, gnorm_a, w_q_b, w_o_b, sinks_b, w_ffn_in, w_ffn_out, final_norm_w,
     w_kv) = wts
    prompt = k_buf is None
    bsz, t, _ = x.shape
    m = bsz * t
    common = dict(per_token=per_token, rows_per_batch=t)
    h = x.reshape(m, D_MODEL)
    states = []
    step_inputs = []
    new_state = None
    kv = k_state = v_state = kv_new = None
    for l in range(DEPTH):
        a1, s1, g1, a2, s2, g2 = mods[l]
        if l < N_A_LAYERS:
            qkvg, lf = norm_proj(h, (a1, s1), w_in_a, l, epilogue="hgrn",
                                 lb=lbs[l].reshape(1, D_MODEL), **common)
            gn = gnorm_a[l].reshape(1, A_DV)
            if prompt:
                mix, s_new = hgrn_scan_prompt(qkvg, lf, gn, bsz, t)
                states.append(s_new)
            else:
                qkvg = qkvg.astype(F32)
                last_a = l == N_A_LAYERS - 1
                mix, new_state = hgrn_step(qkvg, lf, gn, hgrn_state0, l,
                                           earlier=list(step_inputs) if last_a else None)
                step_inputs.append((qkvg, lf))
            w_o, lo = w_o_a, l
        else:
            j = l - N_A_LAYERS
            scale = LOG2E / math.sqrt(B_HEAD_DIM)
            if prompt:
                q = norm_proj(h, (a1, s1), w_q_b, j, out_scale=scale, **common)
                mix = swa_prompt(q, kv, sinks_b[j], bias_tab, bsz, t)
            else:
                q = norm_proj(h, (a1, s1), w_q_b, j, out_scale=scale, out_dtype=F32, **common)
                lane_kv = (np.arange(KV_WIDTH) // B_HEAD_DIM)[None, :]
                head_kv = (np.arange(B_HEADS) // B_GROUPS)[:, None]
                qz = jnp.where((lane_kv == head_kv)[None],
                               jnp.tile(q.reshape(m, B_HEADS, B_HEAD_DIM), (1, 1, B_KV_HEADS)), 0.0)
                if j == 0:
                    mix, k_state, v_state = swa_step(qz, k_buf.reshape(m, WINDOW, KV_WIDTH),
                                                     v_buf.reshape(m, WINDOW, KV_WIDTH),
                                                     sinks_b[j], bias_row, kv_new)
                else:
                    mix = swa_step(qz, k_state, v_state, sinks_b[j], bias_row)
                mix = mix.reshape(m, D_MODEL)
            w_o, lo = w_o_b, j
        h = post_ffn(mix, h, g1, (a2, s2), g2, w_o, lo, w_ffn_in, w_ffn_out, l,
                     final_w=final_norm_w if l == DEPTH - 1 else None, **common)
        if l == N_A_LAYERS - 1:
            if prompt:
                kv = norm_proj(h, kv_mod, w_kv, 0, epilogue="kv_rep", **common)
                tail = h.reshape(bsz, t, D_MODEL)[:, -WINDOW:].reshape(bsz * WINDOW, D_MODEL)
                kv_tail = norm_proj(tail, kv_mod, w_kv, 0, per_token=False,
                                    rows_per_batch=WINDOW, out_dtype=F32)
                kv_tail = kv_tail.reshape(bsz, WINDOW, 2, B_KV_HEADS, B_HEAD_DIM)
                k_state, v_state = kv_tail[:, :, 0], kv_tail[:, :, 1]
            else:
                kv_row = norm_proj(h, kv_mod, w_kv, 0, out_dtype=F32, **common)
                kv_new = (kv_row[:, :KV_WIDTH].reshape(m, 1, KV_WIDTH),
                          kv_row[:, KV_WIDTH:].reshape(m, 1, KV_WIDTH))
    y = h.reshape(bsz, t, D_MODEL)
    if prompt:
        return y, jnp.stack(states, axis=1), k_state, v_state
    cache_shape = (m, WINDOW, B_KV_HEADS, B_HEAD_DIM)
    return y, new_state, k_state.reshape(cache_shape), v_state.reshape(cache_shape)


def kernel(x_prompt, x_sample, state_hgrn, cache_swa_k, cache_swa_v, c_prompt, c_sample,
           w_in_a, w_o_a, gnorm_a, lb_a, w_kv, w_ada_kv, b_ada_kv, kv_norm_w, w_q_b, w_o_b,
           sinks_b, rel_bias, norm_w, w_ada, b_ada, w_ffn_in, w_ffn_out, final_norm_w):
    n_p, n_s = c_prompt.shape[0], c_sample.shape[0]
    rows = -(-(n_p + n_s) // 8) * 8
    c_all = jnp.concatenate(
        [c_sample, c_prompt, jnp.zeros((rows - n_p - n_s, D_MODEL), F32)], axis=0)

    zeros, ones = jnp.zeros((DEPTH, D_MODEL), F32), jnp.ones((DEPTH, D_MODEL), F32)
    p_ada = jnp.stack([zeros, norm_w[:, 0], zeros, zeros, norm_w[:, 1], zeros], axis=1)
    q_ada = jnp.stack([ones, norm_w[:, 0], ones, ones, norm_w[:, 1], ones], axis=1)
    ada = ada_project(c_all, w_ada, b_ada.reshape(DEPTH, 6, 1, D_MODEL),
                      p_ada.reshape(DEPTH, 6, 1, D_MODEL), q_ada.reshape(DEPTH, 6, 1, D_MODEL))
    p_kv = jnp.stack([zeros[0], kv_norm_w]).reshape(1, 2, 1, D_MODEL)
    q_kv = jnp.stack([ones[0], kv_norm_w]).reshape(1, 2, 1, D_MODEL)
    ada_kv = ada_project(c_all, w_ada_kv.reshape(1, D_MODEL, 2 * D_MODEL),
                         b_ada_kv.reshape(1, 2, 1, D_MODEL), p_kv, q_kv)

    def mods_for(row0):
        layers = [tuple((ada, l, c, row0) for c in (1, 0, 2, 4, 3, 5)) for l in range(DEPTH)]
        return layers, ((ada_kv, 0, 1, row0), (ada_kv, 0, 0, row0))

    lb_sm = jax.nn.softmax(lb_a.astype(F32), axis=0)
    lbs = jnp.cumsum(lb_sm, axis=0) - lb_sm[0:1]

    rb = rel_bias.astype(F32)[_T5_BUCKETS]
    t_idx, j_idx = np.arange(WINDOW)[:, None], np.arange(WINDOW)[None, :]
    from_prev = j_idx > t_idx
    dist = np.where(from_prev, t_idx + WINDOW - j_idx, t_idx - j_idx)
    onehot = (jnp.asarray(dist)[:, :, None] == jnp.arange(WINDOW)[None, None, :]).astype(F32)
    tab = jnp.einsum("tjd,dh->htj", onehot, rb, precision=lax.Precision.HIGHEST) * LOG2E
    bias_tab = jnp.stack([jnp.where(from_prev, MASK_VALUE, tab),
                          tab])
    bias_row = rb[::-1].T * LOG2E

    bf = lambda w: w.astype(BF16)
    wts = (bf(w_in_a), bf(w_o_a), gnorm_a, bf(w_q_b), bf(w_o_b), sinks_b, bf(w_ffn_in),
           bf(w_ffn_out), final_norm_w, bf(w_kv)[None])
    mods_p, kv_mod_p = mods_for(n_s)
    mods_s, kv_mod_s = mods_for(0)
    y_p, st_p, k_p, v_p = _trunk(x_prompt, mods_p, kv_mod_p, False, None, None, None, wts, lbs,
                                 bias_tab, bias_row)
    y_s, st_s, k_s, v_s = _trunk(x_sample, mods_s, kv_mod_s, True, state_hgrn, cache_swa_k,
                                 cache_swa_v, wts, lbs, bias_tab, bias_row)
    return (y_p, y_s, st_p, st_s, k_p, v_p, k_s, v_s)
```

```python
import functools
import math

import numpy as np
import jax
import jax.numpy as jnp
from jax import lax
from jax.experimental import pallas as pl
from jax.experimental.pallas import tpu as pltpu

F32 = jnp.float32
BF16 = jnp.bfloat16

D_MODEL = 1024
DEPTH = 4
N_A_LAYERS = 2
A_HEADS = 8
A_DK = 128
A_DV = 128
F_MIN = 1e-30
B_HEAD_DIM = 64
B_HEADS = 16
B_KV_HEADS = 4
B_GROUPS = 4
WINDOW = 128
MASK_VALUE = -1e30
N_BUCKETS = 32
MAX_DISTANCE = 128
D_FF = 2816
EPS = 1e-6
GROUP_LANES = B_GROUPS * B_HEAD_DIM
KV_WIDTH = B_KV_HEADS * B_HEAD_DIM
HEAD_DIM_SHIFT = B_HEAD_DIM.bit_length() - 1
GROUPS_SHIFT = B_GROUPS.bit_length() - 1

CHUNK = 128
N_LEVELS = 7
ROW_TILE = 1024
COL_TILE = 1024
FF_TILE = 256
STEP_TOKENS = 4
ATTN_TOKENS = 8
VMEM_LIMIT_BYTES = 58 * 1024 * 1024


def _params(sem):
    return pltpu.CompilerParams(dimension_semantics=sem, vmem_limit_bytes=VMEM_LIMIT_BYTES)


def _row_tile(m, per_token, rows_per_batch):
    return min(m, ROW_TILE) if per_token else min(m, ROW_TILE, rows_per_batch)


def _sigmoid(x):
    return 0.5 * jnp.tanh(0.5 * x) + 0.5


def _silu(x):
    h = 0.5 * x
    return h * jnp.tanh(h) + h


def _norm_mod(x, a, s):
    y = x * lax.rsqrt(jnp.mean(x * x, axis=-1, keepdims=True) + EPS)
    return (y * a + s).astype(BF16)


def _resident(shape, layer):
    return pl.BlockSpec((None,) + shape, lambda i: (layer,) + (0,) * len(shape),
                        pipeline_mode=pl.Buffered(1))


BATCH_ROWS = 8


def _mod_operand(mod, tm, per_token):
    arr, layer, chunk, row0 = mod
    if per_token:
        return arr, pl.BlockSpec((None, tm, D_MODEL), lambda i: (layer, i, chunk))
    assert row0 % BATCH_ROWS == 0
    return arr, pl.BlockSpec((None, BATCH_ROWS, D_MODEL),
                             lambda i: (layer, row0 // BATCH_ROWS, chunk))


def _row_value(ref, tiles_per_batch):
    if tiles_per_batch is None:
        return ref[...]
    return ref[pl.ds(lax.div(pl.program_id(0), tiles_per_batch), 1), :]


def _tiles_per_batch(m, tm, per_token, rows_per_batch):
    if per_token:
        return None
    assert rows_per_batch % tm == 0 and m // rows_per_batch <= BATCH_ROWS
    return rows_per_batch // tm


def _norm_proj_body(*refs, tiles_per_batch, epilogue, out_scale):
    x_ref, a_ref, s_ref, w_ref = refs[:4]
    xn = _norm_mod(x_ref[...], _row_value(a_ref, tiles_per_batch),
                   _row_value(s_ref, tiles_per_batch))

    tn = min(COL_TILE, w_ref.shape[1])

    def proj(c):
        return jnp.dot(xn, w_ref[:, c:c + tn], preferred_element_type=F32)

    if epilogue == "plain":
        (o_ref,) = refs[4:]
        for c in range(0, o_ref.shape[1], tn):
            acc = proj(c)
            if out_scale is not None:
                acc = acc * out_scale
            o_ref[:, c:c + tn] = acc.astype(o_ref.dtype)
    elif epilogue == "kv_rep":
        (o_ref,) = refs[4:]
        acc = proj(0)
        low_half = lax.broadcasted_iota(jnp.int32, (acc.shape[0], 128), 1) < B_HEAD_DIM
        for c in range(acc.shape[1] // 128):
            x = acc[:, c * 128:(c + 1) * 128]
            swapped = pltpu.roll(x, B_HEAD_DIM, 1)
            for half, rep in enumerate((jnp.where(low_half, x, swapped),
                                        jnp.where(low_half, swapped, x))):
                rep = rep.astype(o_ref.dtype)
                base = (2 * c + half) * GROUP_LANES
                o_ref[:, base:base + 128] = rep
                o_ref[:, base + 128:base + 256] = rep
    elif epilogue == "hgrn":
        lb_ref, o_ref, lf_ref = refs[4:]
        d = D_MODEL
        lb = lb_ref[...]
        o_ref[:, 0:d] = _silu(proj(0)).astype(o_ref.dtype)
        f = lb + (1.0 - lb) * _sigmoid(proj(d))
        o_ref[:, d:2 * d] = (1.0 - f).astype(o_ref.dtype)
        lf_ref[...] = jnp.log2(jnp.maximum(f, F_MIN))
        o_ref[:, 3 * d:4 * d] = _silu(proj(3 * d)).astype(o_ref.dtype)
        o_ref[:, 2 * d:3 * d] = proj(2 * d).astype(o_ref.dtype)
    else:
        raise ValueError(epilogue)


def norm_proj(x, mod, w, layer, *, per_token, rows_per_batch, epilogue="plain", lb=None,
              out_dtype=BF16, out_scale=None):
    m, k = x.shape
    n = w.shape[-1]
    tm = _row_tile(m, per_token, rows_per_batch)
    assert m % tm == 0 and (per_token or rows_per_batch % tm == 0)
    (a_arr, a_spec), (s_arr, s_spec) = [_mod_operand(v, tm, per_token) for v in mod]
    in_specs = [pl.BlockSpec((tm, k), lambda i: (i, 0)), a_spec, s_spec, _resident((k, n), layer)]
    args = [x, a_arr, s_arr, w]
    n_out = n * B_GROUPS if epilogue == "kv_rep" else n
    out_specs = [pl.BlockSpec((tm, n_out), lambda i: (i, 0))]
    out_shape = [jax.ShapeDtypeStruct((m, n_out), out_dtype)]
    if epilogue == "hgrn":
        in_specs.append(pl.BlockSpec((1, D_MODEL), lambda i: (0, 0)))
        args.append(lb)
        out_specs.append(pl.BlockSpec((tm, D_MODEL), lambda i: (i, 0)))
        out_shape.append(jax.ShapeDtypeStruct((m, D_MODEL), F32))
    outs = pl.pallas_call(
        functools.partial(_norm_proj_body, epilogue=epilogue, out_scale=out_scale,
                          tiles_per_batch=_tiles_per_batch(m, tm, per_token, rows_per_batch)),
        grid=(m // tm,),
        in_specs=in_specs,
        out_specs=out_specs,
        out_shape=out_shape,
        compiler_params=_params(("parallel",)),
        name=f"norm_proj_{epilogue}_m{m}_n{n}",
    )(*args)
    return outs if epilogue == "hgrn" else outs[0]


def _post_ffn_body(*refs, tiles_per_batch, final_norm):
    (mix_ref, h_ref, g1_ref, a2_ref, s2_ref, g2_ref, wo_ref, win_ref, wout_ref) = refs[:9]
    rest = refs[9:]
    fw_ref = rest[0] if final_norm else None
    o_ref, act_ref = rest[-2:]
    row = functools.partial(_row_value, tiles_per_batch=tiles_per_batch)
    h_mid = h_ref[...] + row(g1_ref) * jnp.dot(mix_ref[...].astype(BF16), wo_ref[...],
                                               preferred_element_type=F32)
    xn = _norm_mod(h_mid, row(a2_ref), row(s2_ref))
    for c in range(0, D_FF, FF_TILE):
        gate = jnp.dot(xn, win_ref[:, c:c + FF_TILE], preferred_element_type=F32)
        up = jnp.dot(xn, win_ref[:, D_FF + c:D_FF + c + FF_TILE], preferred_element_type=F32)
        act_ref[:, c:c + FF_TILE] = (_silu(gate) * up).astype(BF16)
    out = h_mid + row(g2_ref) * jnp.dot(act_ref[...], wout_ref[...], preferred_element_type=F32)
    if final_norm:
        out = out * lax.rsqrt(jnp.mean(out * out, axis=-1, keepdims=True) + EPS) * fw_ref[...]
    o_ref[...] = out


def post_ffn(mix, h, g1, mod2, g2, w_o, lo, w_ffn_in, w_ffn_out, lf, *, per_token,
             rows_per_batch, final_w=None):
    m, d = h.shape
    tm = _row_tile(m, per_token, rows_per_batch)
    assert m % tm == 0 and (per_token or rows_per_batch % tm == 0)
    rows = [_mod_operand(v, tm, per_token) for v in (g1, mod2[0], mod2[1], g2)]
    tile = pl.BlockSpec((tm, d), lambda i: (i, 0))
    in_specs = [tile, tile] + [spec for _, spec in rows] + [
        _resident((d, d), lo), _resident((d, 2 * D_FF), lf), _resident((D_FF, d), lf)]
    args = [mix, h] + [arr for arr, _ in rows] + [w_o, w_ffn_in, w_ffn_out]
    if final_w is not None:
        in_specs.append(pl.BlockSpec((1, d), lambda i: (0, 0)))
        args.append(final_w.reshape(1, d))
    return pl.pallas_call(
        functools.partial(_post_ffn_body, final_norm=final_w is not None,
                          tiles_per_batch=_tiles_per_batch(m, tm, per_token, rows_per_batch)),
        grid=(m // tm,),
        in_specs=in_specs,
        out_specs=tile,
        out_shape=jax.ShapeDtypeStruct((m, d), F32),
        scratch_shapes=[pltpu.VMEM((tm, D_FF), BF16)],
        compiler_params=_params(("parallel",)),
        name=f"post_ffn_m{m}",
    )(*args)


def _ada_body(c_ref, w_ref, b_ref, p_ref, q_ref, o_ref):
    c = _silu(c_ref[...]).astype(BF16)
    acc = jnp.dot(c, w_ref[...].astype(BF16), preferred_element_type=F32)
    o_ref[...] = p_ref[...] + q_ref[...] * (acc + b_ref[...])


ADA_COLS = 2048


def ada_project(c_all, w, b, p, q):
    n_l, _, n_cols = w.shape
    tn = min(ADA_COLS, n_cols)
    assert n_cols % tn == 0
    rows = c_all.shape[0]
    vec_spec = pl.BlockSpec((None, 1, tn), lambda l, j: (l, 0, j))
    return pl.pallas_call(
        _ada_body,
        grid=(n_l, n_cols // tn),
        in_specs=[
            pl.BlockSpec((rows, D_MODEL), lambda l, j: (0, 0)),
            pl.BlockSpec((None, D_MODEL, tn), lambda l, j: (l, 0, j)),
            vec_spec, vec_spec, vec_spec,
        ],
        out_specs=pl.BlockSpec((None, rows, tn), lambda l, j: (l, 0, j)),
        out_shape=jax.ShapeDtypeStruct((n_l, rows, n_cols), F32),
        compiler_params=_params(("parallel", "parallel")),
        name="ada_project",
    )(c_all, w, b, p, q)


MATMUL_LEVELS = (5, 6)


def _scan_tables():
    c = CHUNK
    t = np.arange(c)[:, None]
    u = np.arange(c)[None, :]
    blocks = [(u <= t)]
    for level in MATMUL_LEVELS:
        p = N_LEVELS - level
        odd = ((t >> p) & 1) == 1
        start = (t >> p) << p
        end = (((t >> p) + 1) << p) - 1
        blocks.append(np.where(odd, (u >= start) & (u <= t), (u > t) & (u <= end)))
    sums = np.concatenate(blocks, axis=0).astype(np.float32)
    x = np.arange(c)[:, None] ^ np.arange(c)[None, :]
    msb = np.floor(np.log2(np.maximum(x, 1))).astype(np.int32)
    lvl = np.where(x == 0, 0, N_LEVELS - msb)
    lvl = np.where(np.arange(c)[None, :] > np.arange(c)[:, None], -1, lvl).astype(np.int32)
    return sums, lvl


_SCAN_SUMS, _SCAN_LEVELS = _scan_tables()

_TN = (((1,), (1,)), ((), ()))
_TM = (((0,), (0,)), ((), ()))


HEAD_GROUP = 8
STEP_CHUNKS = 4


def _level_exponent(ex, log2_f, odd_rows, cols, level):
    m = CHUNK >> level
    if level in MATMUL_LEVELS:
        i = 1 + MATMUL_LEVELS.index(level)
        return ex[i * CHUNK:(i + 1) * CHUNK, cols]
    if m == 1:
        return jnp.where(odd_rows, log2_f[:, cols], 0.0)
    parts = []
    for i in range(CHUNK // m):
        blk = ex[i * m:(i + 1) * m, cols]
        if i & 1:
            parts.append(blk - ex[i * m - 1:i * m, cols])
        else:
            parts.append(ex[(i + 1) * m - 1:(i + 1) * m, cols] - blk)
    return jnp.concatenate(parts, axis=0)


def _level_mix(q, kk, odd_rows, level):
    m = CHUNK >> level
    if m >= 8:
        parts = [(q if (i & 1) else kk)[i * m:(i + 1) * m] for i in range(CHUNK // m)]
        return jnp.concatenate(parts, axis=0)
    return jnp.where(odd_rows, q, kk)


def _odd_blocks(x, m):
    return jnp.concatenate([x[i * m:(i + 1) * m] for i in range(1, CHUNK // m, 2)], axis=0)


def _level_product(mix, level):
    m = CHUNK >> level
    if m < 8:
        return lax.dot_general(mix, mix, _TN, preferred_element_type=F32)
    lhs = _odd_blocks(mix, m) if m >= 16 else mix
    a = lax.dot_general(lhs, mix, _TN, preferred_element_type=F32)
    return a if m >= 16 else _odd_blocks(a, m)


def _level_select(att, a, lvl, level):
    m = CHUNK >> level
    if m < 8:
        return jnp.where(lvl == level, a, att)
    tiles = []
    for i in range(CHUNK // m):
        rows = slice(i * m, (i + 1) * m)
        if i & 1:
            a_rows = a[(i // 2) * m:(i // 2 + 1) * m]
            tiles.append(jnp.where(lvl[rows] == level, a_rows, att[rows]))
        else:
            tiles.append(att[rows])
    return jnp.concatenate(tiles, axis=0)


def _hgrn_scan_body(q_ref, k_ref, v_ref, g_ref, lf_ref, gn_ref, sums_ref, lvl_ref,
                    o_ref, st_ref, state_ref, ex_ref):
    n = pl.program_id(1)

    @pl.when(n == 0)
    def _():
        state_ref[...] = jnp.zeros_like(state_ref)

    lvl = lvl_ref[...]
    row = lax.broadcasted_iota(jnp.int32, (CHUNK, A_DK), 0)
    gn = gn_ref[...]
    heads = [slice(h * A_DK, (h + 1) * A_DK) for h in range(A_HEADS)]
    for c in range(STEP_CHUNKS):
        rows = slice(c * CHUNK, (c + 1) * CHUNK)
        ex = ex_ref.at[c]
        log2_f = lf_ref[rows, :]
        hi = log2_f.astype(BF16)
        rem = log2_f - hi.astype(F32)
        mid = rem.astype(BF16)
        lo = (rem - mid.astype(F32)).astype(BF16)
        ex[...] = jnp.dot(sums_ref[...], jnp.concatenate([hi, mid, lo], axis=0),
                          preferred_element_type=F32)
        b_end = ex[CHUNK - 1:CHUNK, :]
        d_end = jnp.exp2(b_end)
        for g0 in range(0, A_HEADS, HEAD_GROUP):
            group = list(range(g0, g0 + HEAD_GROUP))
            q_bf = {h: q_ref[rows, heads[h]] for h in group}
            k_bf = {h: k_ref[rows, heads[h]] for h in group}
            q = {h: q_bf[h].astype(F32) for h in group}
            kk = {h: k_bf[h].astype(F32) for h in group}
            att = {h: jnp.where(lvl == 0, lax.dot_general(q_bf[h], k_bf[h], _TN,
                                                          preferred_element_type=F32), 0.0)
                   for h in group}
            for level in range(1, N_LEVELS + 1):
                odd_rows = ((row >> (N_LEVELS - level)) & 1) == 1
                for h in group:
                    e = jnp.exp2(_level_exponent(ex, log2_f, odd_rows, heads[h], level))
                    mix = (_level_mix(q[h], kk[h], odd_rows, level) * e).astype(BF16)
                    att[h] = _level_select(att[h], _level_product(mix, level), lvl, level)
            for h in group:
                sl = heads[h]
                b_incl = ex[0:CHUNK, sl]
                q_dec = (q[h] * jnp.exp2(b_incl)).astype(BF16)
                k_dec = (kk[h] * jnp.exp2(b_end[:, sl] - b_incl)).astype(BF16)
                v = v_ref[rows, sl]
                s_t = state_ref[h]
                o = (lax.dot_general(q_dec, s_t.astype(BF16), _TN, preferred_element_type=F32)
                     + jnp.dot(att[h].astype(BF16), v, preferred_element_type=F32))
                state_ref[h] = d_end[:, sl] * s_t + lax.dot_general(
                    v, k_dec, _TM, preferred_element_type=F32)
                y = o * lax.rsqrt(jnp.mean(o * o, axis=-1, keepdims=True) + EPS)
                o_ref[rows, sl] = (y * gn * g_ref[rows, sl].astype(F32)).astype(o_ref.dtype)

    @pl.when(n == pl.num_programs(1) - 1)
    def _():
        for h in range(A_HEADS):
            st_ref[0, h] = state_ref[h].T


def hgrn_scan_prompt(qkvg, lf, gn, bsz, t):
    tr = STEP_CHUNKS * CHUNK
    ns = t // tr
    assert t % tr == 0
    sec = lambda k: pl.BlockSpec((tr, D_MODEL), lambda b, n, k=k: (b * ns + n, k))
    n_sum = _SCAN_SUMS.shape[0]
    return pl.pallas_call(
        _hgrn_scan_body,
        grid=(bsz, ns),
        in_specs=[sec(0), sec(1), sec(2), sec(3), sec(0),
                  pl.BlockSpec((1, A_DV), lambda b, n: (0, 0)),
                  pl.BlockSpec((n_sum, 3 * CHUNK), lambda b, n: (0, 0)),
                  pl.BlockSpec((CHUNK, CHUNK), lambda b, n: (0, 0))],
        out_specs=[pl.BlockSpec((tr, D_MODEL), lambda b, n: (b * ns + n, 0)),
                   pl.BlockSpec((1, A_HEADS, A_DK, A_DV), lambda b, n: (b, 0, 0, 0))],
        out_shape=[jax.ShapeDtypeStruct((bsz * t, D_MODEL), BF16),
                   jax.ShapeDtypeStruct((bsz, A_HEADS, A_DK, A_DV), F32)],
        scratch_shapes=[pltpu.VMEM((A_HEADS, A_DV, A_DK), F32),
                        pltpu.VMEM((STEP_CHUNKS, n_sum, D_MODEL), F32)],
        compiler_params=_params(("parallel", "arbitrary")),
        name="hgrn_scan",
    )(qkvg, qkvg, qkvg, qkvg, lf, gn, jnp.asarray(np.tile(_SCAN_SUMS, (1, 3)), BF16),
      jnp.asarray(_SCAN_LEVELS))


STEP_ROWS = 16


def _step_columns(decay, kk, q, v_row):
    r = lax.broadcasted_iota(jnp.int32, (STEP_ROWS, A_DK), 0)
    hi = decay.astype(BF16).astype(F32)
    rem = decay - hi
    mid = rem.astype(BF16).astype(F32)
    lo = rem - mid
    lhs = jnp.where(r == 0, hi, jnp.where(r == 1, mid, jnp.where(r == 2, lo,
                                                                 jnp.where(r == 3, kk, 0.0))))
    parts = [jnp.where(r < 3, 1.0, 0.0), jnp.where(r == 3, v_row, 0.0)]
    if q is not None:
        lhs = jnp.where(r == 4, q, lhs)
        parts.append(jnp.where(r == 4, 1.0, 0.0))
    rhs = jnp.concatenate(parts, axis=1).astype(BF16)
    out = lax.dot_general(lhs.astype(BF16), rhs, _TM, preferred_element_type=F32)
    return [out[:, i * A_DV:(i + 1) * A_DV] for i in range(len(parts))]


def _hgrn_step_body(*refs, n_replay, write_state):
    q_ref, k_ref, v_ref, g_ref, lf_ref, gn_ref, s_ref = refs[:7]
    replay = [refs[7 + 4 * i:11 + 4 * i] for i in range(n_replay)]
    outs = refs[7 + 4 * n_replay:]
    o_ref = outs[0]
    gn = gn_ref[...]
    for b in range(STEP_TOKENS):
        decay = jnp.exp2(lf_ref[b])
        q, kk, v, gate = q_ref[b], k_ref[b], v_ref[b], g_ref[b]
        for h in range(A_HEADS):
            sl = slice(h * A_DK, (h + 1) * A_DK)
            if write_state:
                dec_m, kv_m, q_m = _step_columns(decay[:, sl], kk[:, sl], q[:, sl], v[:, sl])
                s_new = dec_m * s_ref[b, h] + kv_m
                outs[1][b, n_replay, h] = s_new
                o = jnp.sum(q_m * s_new, axis=0, keepdims=True)
            else:
                r = lax.broadcasted_iota(jnp.int32, (STEP_ROWS, A_DK), 0)
                lhs = jnp.where(r == 0, q[:, sl] * decay[:, sl], 0.0).astype(BF16)
                ones = jnp.where(r == 0, 1.0, 0.0).astype(BF16)
                qd_m = lax.dot_general(lhs, ones, _TM, preferred_element_type=F32)
                qk = jnp.sum(q[:, sl] * kk[:, sl], axis=-1, keepdims=True)
                o = jnp.sum(qd_m * s_ref[b, h], axis=0, keepdims=True) + qk * v[:, sl]
            y = o * lax.rsqrt(jnp.mean(o * o, axis=-1, keepdims=True) + EPS)
            o_ref[b, :, sl] = y * gn * gate[:, sl]
        for i, (kp_ref, vp_ref, lfp_ref, sp_ref) in enumerate(replay):
            decay_p = jnp.exp2(lfp_ref[b])
            for h in range(A_HEADS):
                sl = slice(h * A_DK, (h + 1) * A_DK)
                dec_m, kv_m = _step_columns(decay_p[:, sl], kp_ref[b][:, sl], None,
                                            vp_ref[b][:, sl])
                outs[1][b, i, h] = dec_m * sp_ref[b, h] + kv_m


def hgrn_step(qkvg, lf, gn, state, layer, earlier=None):
    bsz = qkvg.shape[0]
    nb = STEP_TOKENS
    assert bsz % nb == 0
    sec = lambda k: pl.BlockSpec((nb, 1, D_MODEL), lambda b, k=k: (b, 0, k))
    state_blk = lambda l: pl.BlockSpec((nb, None, A_HEADS, A_DK, A_DV),
                                       lambda b, l=l: (b, l, 0, 0, 0))
    as3 = lambda a: a.reshape(bsz, 1, a.shape[-1])
    in_specs = [sec(0), sec(1), sec(2), sec(3), sec(0),
                pl.BlockSpec((1, A_DV), lambda b: (0, 0)), state_blk(layer)]
    args = [as3(qkvg)] * 4 + [as3(lf), gn, state]
    write_state = earlier is not None
    for i, (qkvg_p, lf_p) in enumerate(earlier or ()):
        in_specs += [sec(1), sec(2), sec(0), state_blk(i)]
        args += [as3(qkvg_p), as3(qkvg_p), as3(lf_p), state]
    out_specs = [pl.BlockSpec((nb, 1, D_MODEL), lambda b: (b, 0, 0))]
    out_shape = [jax.ShapeDtypeStruct((bsz, 1, D_MODEL), F32)]
    if write_state:
        assert len(earlier) == layer
        n_l = layer + 1
        out_specs.append(pl.BlockSpec((nb, n_l, A_HEADS, A_DK, A_DV), lambda b: (b, 0, 0, 0, 0)))
        out_shape.append(jax.ShapeDtypeStruct((bsz, n_l, A_HEADS, A_DK, A_DV), F32))
    outs = pl.pallas_call(
        functools.partial(_hgrn_step_body, n_replay=len(earlier or ()), write_state=write_state),
        grid=(bsz // nb,),
        in_specs=in_specs,
        out_specs=out_specs,
        out_shape=out_shape,
        compiler_params=_params(("parallel",)),
        name="hgrn_step",
    )(*args)
    o = outs[0].reshape(bsz, D_MODEL)
    return (o, outs[1]) if write_state else (o, None)


def _t5_buckets():
    max_exact = N_BUCKETS // 2
    d = np.arange(WINDOW)
    large = max_exact + (np.log(np.maximum(d, 1).astype(np.float32) / max_exact)
                         / math.log(MAX_DISTANCE / max_exact)
                         * (N_BUCKETS - max_exact)).astype(np.int32)
    large = np.clip(large, 0, N_BUCKETS - 1)
    return np.where(d < max_exact, d, large).astype(np.int32)


_T5_BUCKETS = _t5_buckets()


LOG2E = math.log2(math.e)


SWA_BLOCKS = 4


def _swa_prompt_body(sink_ref, q_ref, kp_ref, kc_ref, vp_ref, vc_ref, bias0_ref, bias_ref, o_ref):
    lane_head = lax.broadcasted_iota(jnp.int32, (WINDOW, GROUP_LANES), 1) >> HEAD_DIM_SHIFT
    from_prev = (lax.broadcasted_iota(jnp.int32, (WINDOW, WINDOW), 1)
                 > lax.broadcasted_iota(jnp.int32, (WINDOW, WINDOW), 0))
    groups = [slice(g * GROUP_LANES, (g + 1) * GROUP_LANES) for g in range(B_KV_HEADS)]
    for blk in range(SWA_BLOCKS):
        rows = slice(blk * WINDOW, (blk + 1) * WINDOW)
        prev_rows = slice((blk - 1) * WINDOW, blk * WINDOW)
        table = bias0_ref if blk == 0 else bias_ref

        def keys(prev_ref, cur_ref, cs):
            prev = prev_ref[:, cs] if blk == 0 else cur_ref[prev_rows, cs]
            return jnp.concatenate([prev, cur_ref[rows, cs]], axis=0)

        all_logits = []
        for cs in groups:
            qg = q_ref[rows, cs].astype(F32)
            qstack = jnp.concatenate(
                [jnp.where(lane_head == j, qg, 0.0).astype(BF16) for j in range(B_GROUPS)], axis=0)
            all_logits.append(lax.dot_general(qstack, keys(kp_ref, kc_ref, cs), _TN,
                                              preferred_element_type=F32))
        for g, cs in enumerate(groups):
            logits = all_logits[g]
            vcat = keys(vp_ref, vc_ref, cs)
            acc = jnp.zeros((WINDOW, GROUP_LANES), F32)
            for j in range(B_GROUPS):
                head = g * B_GROUPS + j
                hr = slice(j * WINDOW, (j + 1) * WINDOW)
                lg = jnp.where(from_prev, logits[hr, :WINDOW], logits[hr, WINDOW:]) + table[head]
                sink = sink_ref[head] * LOG2E
                m = jnp.maximum(jnp.max(lg, axis=-1, keepdims=True), sink)
                p = jnp.exp2(lg - m)
                denom = jnp.sum(p, axis=-1, keepdims=True) + jnp.exp2(sink - m)
                p_prev = jnp.where(from_prev, p, 0.0)
                p_both = jnp.concatenate([p_prev, p - p_prev], axis=1).astype(BF16)
                pv = jnp.dot(p_both, vcat, preferred_element_type=F32)
                acc = jnp.where(lane_head == j, pv / denom, acc)
            o_ref[rows, cs] = acc.astype(o_ref.dtype)


def swa_prompt(q, kv_rep, sinks, bias_tabs, bsz, t):
    tr = SWA_BLOCKS * WINDOW
    ns = t // tr
    assert t % tr == 0
    cur = lambda c: pl.BlockSpec((tr, D_MODEL), lambda b, n, c=c: (b * ns + n, c))
    prev = lambda c: pl.BlockSpec(
        (WINDOW, D_MODEL),
        lambda b, n, c=c: (jnp.maximum((b * ns + n) * SWA_BLOCKS - 1, 0), c))
    table = lambda first: pl.BlockSpec(
        (None, B_HEADS, WINDOW, WINDOW),
        (lambda b, n: (jnp.minimum(n, 1), 0, 0, 0)) if first else (lambda b, n: (1, 0, 0, 0)))
    return pl.pallas_call(
        _swa_prompt_body,
        grid=(bsz, ns),
        in_specs=[pl.BlockSpec(memory_space=pltpu.SMEM),
                  cur(0), prev(0), cur(0), prev(1), cur(1), table(True), table(False)],
        out_specs=pl.BlockSpec((tr, D_MODEL), lambda b, n: (b * ns + n, 0)),
        out_shape=jax.ShapeDtypeStruct((bsz * t, D_MODEL), BF16),
        compiler_params=_params(("parallel", "arbitrary")),
        name="swa_prompt",
    )(sinks, q, kv_rep, kv_rep, kv_rep, kv_rep, bias_tabs, bias_tabs)


def _swa_step_body(*refs, shift_cache):
    sink_ref, qz_ref, k_ref, v_ref, bias_ref, fold_ref = refs[:6]
    if shift_cache:
        kn_ref, vn_ref, o_ref, ko_ref, vo_ref = refs[6:]
        last = lax.broadcasted_iota(jnp.int32, (WINDOW, KV_WIDTH), 0) == WINDOW - 1
    else:
        (o_ref,) = refs[6:]
    sink = sink_ref[...] * LOG2E
    head_kv = lax.broadcasted_iota(jnp.int32, (B_HEADS, KV_WIDTH), 0) >> GROUPS_SHIFT
    lane_kv = lax.broadcasted_iota(jnp.int32, (B_HEADS, KV_WIDTH), 1) >> HEAD_DIM_SHIFT
    for b in range(ATTN_TOKENS):
        if shift_cache:
            k = jnp.where(last, kn_ref[b], pltpu.roll(k_ref[b], WINDOW - 1, 0))
            v = jnp.where(last, vn_ref[b], pltpu.roll(v_ref[b], WINDOW - 1, 0))
            ko_ref[b] = k
            vo_ref[b] = v
        else:
            k, v = k_ref[b], v_ref[b]
        qz = qz_ref[b].astype(BF16)
        logits = lax.dot_general(qz, k.astype(BF16), _TN, preferred_element_type=F32)
        logits = logits + bias_ref[...]
        m = jnp.maximum(jnp.max(logits, axis=-1, keepdims=True), sink)
        p = jnp.exp2(logits - m)
        denom = jnp.sum(p, axis=-1, keepdims=True) + jnp.exp2(sink - m)
        pv = jnp.dot(p.astype(BF16), v.astype(BF16), preferred_element_type=F32) / denom
        own = jnp.where(head_kv == lane_kv, pv, 0.0).astype(BF16)
        o_ref[b] = jnp.dot(own, fold_ref[...], preferred_element_type=F32)


def swa_step(qz, k_cache, v_cache, sinks, bias_row, kv_new=None):
    bsz = qz.shape[0]
    nt = ATTN_TOKENS
    assert bsz % nt == 0
    fold = np.zeros((KV_WIDTH, B_HEAD_DIM), np.float32)
    fold[np.arange(KV_WIDTH), np.arange(KV_WIDTH) % B_HEAD_DIM] = 1.0
    cache_blk = pl.BlockSpec((nt, WINDOW, KV_WIDTH), lambda b: (b, 0, 0))
    in_specs = [pl.BlockSpec((B_HEADS, 1), lambda b: (0, 0)),
                pl.BlockSpec((nt, B_HEADS, KV_WIDTH), lambda b: (b, 0, 0)),
                cache_blk, cache_blk,
                pl.BlockSpec((B_HEADS, WINDOW), lambda b: (0, 0)),
                pl.BlockSpec((KV_WIDTH, B_HEAD_DIM), lambda b: (0, 0))]
    args = [sinks.reshape(B_HEADS, 1), qz, k_cache, v_cache, bias_row, jnp.asarray(fold, BF16)]
    out_specs = [pl.BlockSpec((nt, B_HEADS, B_HEAD_DIM), lambda b: (b, 0, 0))]
    out_shape = [jax.ShapeDtypeStruct((bsz, B_HEADS, B_HEAD_DIM), F32)]
    if kv_new is not None:
        row_blk = pl.BlockSpec((nt, 1, KV_WIDTH), lambda b: (b, 0, 0))
        in_specs += [row_blk, row_blk]
        args += list(kv_new)
        out_specs += [cache_blk, cache_blk]
        out_shape += [jax.ShapeDtypeStruct(k_cache.shape, F32)] * 2
    outs = pl.pallas_call(
        functools.partial(_swa_step_body, shift_cache=kv_new is not None),
        grid=(bsz // nt,),
        in_specs=in_specs,
        out_specs=out_specs,
        out_shape=out_shape,
        compiler_params=_params(("parallel",)),
        name="swa_step",
    )(*args)
    return outs if kv_new is not None else outs[0]


def _trunk(x, mods, kv_mod, per_token, hgrn_state0, k_buf, v_buf, wts, lbs, bias_tab, bias_row):
    (w_in_a, w_o_a, gnorm_a, w_q_b, w_o_b, sinks_b, w_ffn_in, w_ffn_out, final_norm_w,
     w_kv) = wts
    prompt = k_buf is None
    bsz, t, _ = x.shape
    m = bsz * t
    common = dict(per_token=per_token, rows_per_batch=t)
    h = x.reshape(m, D_MODEL)
    states = []
    step_inputs = []
    new_state = None
    kv = k_state = v_state = kv_new = None
    for l in range(DEPTH):
        a1, s1, g1, a2, s2, g2 = mods[l]
        if l < N_A_LAYERS:
            qkvg, lf = norm_proj(h, (a1, s1), w_in_a, l, epilogue="hgrn",
                                 lb=lbs[l].reshape(1, D_MODEL), **common)
            gn = gnorm_a[l].reshape(1, A_DV)
            if prompt:
                mix, s_new = hgrn_scan_prompt(qkvg, lf, gn, bsz, t)
                states.append(s_new)
            else:
                qkvg = qkvg.astype(F32)
                last_a = l == N_A_LAYERS - 1
                mix, new_state = hgrn_step(qkvg, lf, gn, hgrn_state0, l,
                                           earlier=list(step_inputs) if last_a else None)
                step_inputs.append((qkvg, lf))
            w_o, lo = w_o_a, l
        else:
            j = l - N_A_LAYERS
            scale = LOG2E / math.sqrt(B_HEAD_DIM)
            if prompt:
                q = norm_proj(h, (a1, s1), w_q_b, j, out_scale=scale, **common)
                mix = swa_prompt(q, kv, sinks_b[j], bias_tab, bsz, t)
            else:
                q = norm_proj(h, (a1, s1), w_q_b, j, out_scale=scale, out_dtype=F32, **common)
                lane_kv = (np.arange(KV_WIDTH) // B_HEAD_DIM)[None, :]
                head_kv = (np.arange(B_HEADS) // B_GROUPS)[:, None]
                qz = jnp.where((lane_kv == head_kv)[None],
                               jnp.tile(q.reshape(m, B_HEADS, B_HEAD_DIM), (1, 1, B_KV_HEADS)), 0.0)
                if j == 0:
                    mix, k_state, v_state = swa_step(qz, k_buf.reshape(m, WINDOW, KV_WIDTH),
                                                     v_buf.reshape(m, WINDOW, KV_WIDTH),
                                                     sinks_b[j], bias_row, kv_new)
                else:
                    mix = swa_step(qz, k_state, v_state, sinks_b[j], bias_row)
                mix = mix.reshape(m, D_MODEL)
            w_o, lo = w_o_b, j
        h = post_ffn(mix, h, g1, (a2, s2), g2, w_o, lo, w_ffn_in, w_ffn_out, l,
                     final_w=final_norm_w if l == DEPTH - 1 else None, **common)
        if l == N_A_LAYERS - 1:
            if prompt:
                kv = norm_proj(h, kv_mod, w_kv, 0, epilogue="kv_rep", **common)
                tail = h.reshape(bsz, t, D_MODEL)[:, -WINDOW:].reshape(bsz * WINDOW, D_MODEL)
                kv_tail = norm_proj(tail, kv_mod, w_kv, 0, per_token=False,
                                    rows_per_batch=WINDOW, out_dtype=F32)
                kv_tail = kv_tail.reshape(bsz, WINDOW, 2, B_KV_HEADS, B_HEAD_DIM)
                k_state, v_state = kv_tail[:, :, 0], kv_tail[:, :, 1]
            else:
                kv_row = norm_proj(h, kv_mod, w_kv, 0, out_dtype=F32, **common)
                kv_new = (kv_row[:, :KV_WIDTH].reshape(m, 1, KV_WIDTH),
                          kv_row[:, KV_WIDTH:].reshape(m, 1, KV_WIDTH))
    y = h.reshape(bsz, t, D_MODEL)
    if prompt:
        return y, jnp.stack(states, axis=1), k_state, v_state
    cache_shape = (m, WINDOW, B_KV_HEADS, B_HEAD_DIM)
    return y, new_state, k_state.reshape(cache_shape), v_state.reshape(cache_shape)


def kernel(x_prompt, x_sample, state_hgrn, cache_swa_k, cache_swa_v, c_prompt, c_sample,
           w_in_a, w_o_a, gnorm_a, lb_a, w_kv, w_ada_kv, b_ada_kv, kv_norm_w, w_q_b, w_o_b,
           sinks_b, rel_bias, norm_w, w_ada, b_ada, w_ffn_in, w_ffn_out, final_norm_w):
    n_p, n_s = c_prompt.shape[0], c_sample.shape[0]
    rows = -(-(n_p + n_s) // 8) * 8
    c_all = jnp.concatenate(
        [c_sample, c_prompt, jnp.zeros((rows - n_p - n_s, D_MODEL), F32)], axis=0)

    zeros, ones = jnp.zeros((DEPTH, D_MODEL), F32), jnp.ones((DEPTH, D_MODEL), F32)
    p_ada = jnp.stack([zeros, norm_w[:, 0], zeros, zeros, norm_w[:, 1], zeros], axis=1)
    q_ada = jnp.stack([ones, norm_w[:, 0], ones, ones, norm_w[:, 1], ones], axis=1)
    ada = ada_project(c_all, w_ada, b_ada.reshape(DEPTH, 1, 6 * D_MODEL),
                      p_ada.reshape(DEPTH, 1, 6 * D_MODEL), q_ada.reshape(DEPTH, 1, 6 * D_MODEL))
    p_kv = jnp.concatenate([zeros[0], kv_norm_w]).reshape(1, 1, 2 * D_MODEL)
    q_kv = jnp.concatenate([ones[0], kv_norm_w]).reshape(1, 1, 2 * D_MODEL)
    ada_kv = ada_project(c_all, w_ada_kv.reshape(1, D_MODEL, 2 * D_MODEL),
                         b_ada_kv.reshape(1, 1, 2 * D_MODEL), p_kv, q_kv)

    def mods_for(row0):
        layers = [tuple((ada, l, c, row0) for c in (1, 0, 2, 4, 3, 5)) for l in range(DEPTH)]
        return layers, ((ada_kv, 0, 1, row0), (ada_kv, 0, 0, row0))

    lb_sm = jax.nn.softmax(lb_a.astype(F32), axis=0)
    lbs = jnp.cumsum(lb_sm, axis=0) - lb_sm[0:1]

    rb = rel_bias.astype(F32)[_T5_BUCKETS]
    t_idx, j_idx = np.arange(WINDOW)[:, None], np.arange(WINDOW)[None, :]
    from_prev = j_idx > t_idx
    dist = np.where(from_prev, t_idx + WINDOW - j_idx, t_idx - j_idx)
    onehot = (jnp.asarray(dist)[:, :, None] == jnp.arange(WINDOW)[None, None, :]).astype(F32)
    tab = jnp.einsum("tjd,dh->htj", onehot, rb, precision=lax.Precision.HIGHEST) * LOG2E
    bias_tab = jnp.stack([jnp.where(from_prev, MASK_VALUE, tab),
                          tab])
    bias_row = rb[::-1].T * LOG2E

    bf = lambda w: w.astype(BF16)
    wts = (bf(w_in_a), bf(w_o_a), gnorm_a, bf(w_q_b), bf(w_o_b), sinks_b, bf(w_ffn_in),
           bf(w_ffn_out), final_norm_w, bf(w_kv)[None])
    mods_p, kv_mod_p = mods_for(n_s)
    mods_s, kv_mod_s = mods_for(0)
    y_p, st_p, k_p, v_p = _trunk(x_prompt, mods_p, kv_mod_p, False, None, None, None, wts, lbs,
                                 bias_tab, bias_row)
    y_s, st_s, k_s, v_s = _trunk(x_sample, mods_s, kv_mod_s, True, state_hgrn, cache_swa_k,
                                 cache_swa_v, wts, lbs, bias_tab, bias_row)
    return (y_p, y_s, st_p, st_s, k_p, v_p, k_s, v_s)
```

```python
import functools
import math

import numpy as np
import jax
import jax.numpy as jnp
from jax import lax
from jax.experimental import pallas as pl
from jax.experimental.pallas import tpu as pltpu

F32 = jnp.float32
BF16 = jnp.bfloat16

D_MODEL = 1024
DEPTH = 4
N_A_LAYERS = 2
A_HEADS = 8
A_DK = 128
A_DV = 128
F_MIN = 1e-30
B_HEAD_DIM = 64
B_HEADS = 16
B_KV_HEADS = 4
B_GROUPS = 4
WINDOW = 128
MASK_VALUE = -1e30
N_BUCKETS = 32
MAX_DISTANCE = 128
D_FF = 2816
EPS = 1e-6
GROUP_LANES = B_GROUPS * B_HEAD_DIM
KV_WIDTH = B_KV_HEADS * B_HEAD_DIM
HEAD_DIM_SHIFT = B_HEAD_DIM.bit_length() - 1
GROUPS_SHIFT = B_GROUPS.bit_length() - 1

LOG2E = math.log2(math.e)

SUBLANES = 8
LANES = 128
BF16_SUBLANES = 16

CHUNK = 128
N_LEVELS = 7
MATMUL_LEVELS = (5, 6)
STEP_CHUNKS = 4
ROW_TILE = 1024
COL_TILE = 1024
FF_TILE = 256
ADA_COLS = 2048
SWA_BLOCKS = 4
STEP_TOKENS = 4
ATTN_TOKENS = 16
STEP_ROWS = BF16_SUBLANES
BATCH_ROWS = SUBLANES
VMEM_LIMIT_BYTES = 58 * 1024 * 1024


def _params(sem):
    return pltpu.CompilerParams(dimension_semantics=sem, vmem_limit_bytes=VMEM_LIMIT_BYTES)


def _row_tile(m, per_token, rows_per_batch):
    return min(m, ROW_TILE) if per_token else min(m, ROW_TILE, rows_per_batch)


def _sigmoid(x):
    return 0.5 * jnp.tanh(0.5 * x) + 0.5


def _silu(x):
    h = 0.5 * x
    return h * jnp.tanh(h) + h


def _norm_mod(x, a, s):
    y = x * lax.rsqrt(jnp.mean(x * x, axis=-1, keepdims=True) + EPS)
    return (y * a + s).astype(BF16)


def _resident(shape, layer):
    return pl.BlockSpec((None,) + shape, lambda i: (layer,) + (0,) * len(shape),
                        pipeline_mode=pl.Buffered(1))


def _mod_operand(mod, tm, per_token):
    arr, layer, chunk, row0 = mod
    if per_token:
        return arr, pl.BlockSpec((None, tm, D_MODEL), lambda i: (layer, i, chunk))
    assert row0 % BATCH_ROWS == 0
    return arr, pl.BlockSpec((None, BATCH_ROWS, D_MODEL),
                             lambda i: (layer, row0 // BATCH_ROWS, chunk))


def _row_value(ref, tiles_per_batch):
    if tiles_per_batch is None:
        return ref[...]
    return ref[pl.ds(lax.div(pl.program_id(0), tiles_per_batch), 1), :]


def _tiles_per_batch(m, tm, per_token, rows_per_batch):
    if per_token:
        return None
    assert rows_per_batch % tm == 0 and m // rows_per_batch <= BATCH_ROWS
    return rows_per_batch // tm


def _norm_proj_body(*refs, tiles_per_batch, epilogue, out_scale):
    x_ref, a_ref, s_ref, w_ref = refs[:4]
    xn = _norm_mod(x_ref[...], _row_value(a_ref, tiles_per_batch),
                   _row_value(s_ref, tiles_per_batch))

    tn = min(COL_TILE, w_ref.shape[1])

    def proj(c):
        return jnp.dot(xn, w_ref[:, c:c + tn], preferred_element_type=F32)

    if epilogue == "plain":
        (o_ref,) = refs[4:]
        for c in range(0, o_ref.shape[1], tn):
            acc = proj(c)
            if out_scale is not None:
                acc = acc * out_scale
            o_ref[:, c:c + tn] = acc.astype(o_ref.dtype)
    elif epilogue == "kv_rep":
        (o_ref,) = refs[4:]
        acc = proj(0)
        low_half = lax.broadcasted_iota(jnp.int32, (acc.shape[0], LANES), 1) < B_HEAD_DIM
        for c in range(acc.shape[1] // LANES):
            x = acc[:, c * LANES:(c + 1) * LANES]
            swapped = pltpu.roll(x, B_HEAD_DIM, 1)
            for half, rep in enumerate((jnp.where(low_half, x, swapped),
                                        jnp.where(low_half, swapped, x))):
                rep = rep.astype(o_ref.dtype)
                base = (2 * c + half) * GROUP_LANES
                o_ref[:, base:base + LANES] = rep
                o_ref[:, base + LANES:base + GROUP_LANES] = rep
    elif epilogue == "hgrn":
        lb_ref, o_ref, lf_ref = refs[4:]
        d = D_MODEL
        lb = lb_ref[...]
        o_ref[:, 0:d] = _silu(proj(0)).astype(o_ref.dtype)
        f = lb + (1.0 - lb) * _sigmoid(proj(d))
        o_ref[:, d:2 * d] = (1.0 - f).astype(o_ref.dtype)
        lf_ref[...] = jnp.log2(jnp.maximum(f, F_MIN))
        o_ref[:, 2 * d:3 * d] = proj(2 * d).astype(o_ref.dtype)
        o_ref[:, 3 * d:4 * d] = _silu(proj(3 * d)).astype(o_ref.dtype)
    else:
        raise ValueError(epilogue)


def norm_proj(x, mod, w, layer, *, per_token, rows_per_batch, epilogue="plain", lb=None,
              out_dtype=BF16, out_scale=None):
    m, k = x.shape
    n = w.shape[-1]
    tm = _row_tile(m, per_token, rows_per_batch)
    assert m % tm == 0 and (per_token or rows_per_batch % tm == 0)
    (a_arr, a_spec), (s_arr, s_spec) = [_mod_operand(v, tm, per_token) for v in mod]
    in_specs = [pl.BlockSpec((tm, k), lambda i: (i, 0)), a_spec, s_spec, _resident((k, n), layer)]
    args = [x, a_arr, s_arr, w]
    n_out = n * B_GROUPS if epilogue == "kv_rep" else n
    out_specs = [pl.BlockSpec((tm, n_out), lambda i: (i, 0))]
    out_shape = [jax.ShapeDtypeStruct((m, n_out), out_dtype)]
    if epilogue == "hgrn":
        in_specs.append(pl.BlockSpec((1, D_MODEL), lambda i: (0, 0)))
        args.append(lb)
        out_specs.append(pl.BlockSpec((tm, D_MODEL), lambda i: (i, 0)))
        out_shape.append(jax.ShapeDtypeStruct((m, D_MODEL), F32))
    outs = pl.pallas_call(
        functools.partial(_norm_proj_body, epilogue=epilogue, out_scale=out_scale,
                          tiles_per_batch=_tiles_per_batch(m, tm, per_token, rows_per_batch)),
        grid=(m // tm,),
        in_specs=in_specs,
        out_specs=out_specs,
        out_shape=out_shape,
        compiler_params=_params(("parallel",)),
        name=f"norm_proj_{epilogue}_m{m}_n{n}",
    )(*args)
    return outs if epilogue == "hgrn" else outs[0]


def _post_ffn_body(*refs, tiles_per_batch, final_norm):
    (mix_ref, h_ref, g1_ref, a2_ref, s2_ref, g2_ref, wo_ref, win_ref, wout_ref) = refs[:9]
    rest = refs[9:]
    fw_ref = rest[0] if final_norm else None
    o_ref, act_ref = rest[-2:]
    row = functools.partial(_row_value, tiles_per_batch=tiles_per_batch)
    h_mid = h_ref[...] + row(g1_ref) * jnp.dot(mix_ref[...].astype(BF16), wo_ref[...],
                                               preferred_element_type=F32)
    xn = _norm_mod(h_mid, row(a2_ref), row(s2_ref))
    for c in range(0, D_FF, FF_TILE):
        gate = jnp.dot(xn, win_ref[:, c:c + FF_TILE], preferred_element_type=F32)
        up = jnp.dot(xn, win_ref[:, D_FF + c:D_FF + c + FF_TILE], preferred_element_type=F32)
        act_ref[:, c:c + FF_TILE] = (_silu(gate) * up).astype(BF16)
    out = h_mid + row(g2_ref) * jnp.dot(act_ref[...], wout_ref[...], preferred_element_type=F32)
    if final_norm:
        out = out * lax.rsqrt(jnp.mean(out * out, axis=-1, keepdims=True) + EPS) * fw_ref[...]
    o_ref[...] = out


def post_ffn(mix, h, g1, mod2, g2, w_o, lo, w_ffn_in, w_ffn_out, lf, *, per_token,
             rows_per_batch, final_w=None):
    m, d = h.shape
    tm = _row_tile(m, per_token, rows_per_batch)
    assert m % tm == 0 and (per_token or rows_per_batch % tm == 0)
    rows = [_mod_operand(v, tm, per_token) for v in (g1, mod2[0], mod2[1], g2)]
    tile = pl.BlockSpec((tm, d), lambda i: (i, 0))
    in_specs = [tile, tile] + [spec for _, spec in rows] + [
        _resident((d, d), lo), _resident((d, 2 * D_FF), lf), _resident((D_FF, d), lf)]
    args = [mix, h] + [arr for arr, _ in rows] + [w_o, w_ffn_in, w_ffn_out]
    if final_w is not None:
        in_specs.append(pl.BlockSpec((1, d), lambda i: (0, 0)))
        args.append(final_w.reshape(1, d))
    return pl.pallas_call(
        functools.partial(_post_ffn_body, final_norm=final_w is not None,
                          tiles_per_batch=_tiles_per_batch(m, tm, per_token, rows_per_batch)),
        grid=(m // tm,),
        in_specs=in_specs,
        out_specs=tile,
        out_shape=jax.ShapeDtypeStruct((m, d), F32),
        scratch_shapes=[pltpu.VMEM((tm, D_FF), BF16)],
        compiler_params=_params(("parallel",)),
        name=f"post_ffn_m{m}",
    )(*args)


def _ada_body(c_ref, w_ref, b_ref, p_ref, q_ref, o_ref):
    c = _silu(c_ref[...]).astype(BF16)
    acc = jnp.dot(c, w_ref[...].astype(BF16), preferred_element_type=F32)
    o_ref[...] = p_ref[...] + q_ref[...] * (acc + b_ref[...])


def ada_project(c_all, w, b, p, q):
    n_l, _, n_cols = w.shape
    tn = min(ADA_COLS, n_cols)
    assert n_cols % tn == 0
    rows = c_all.shape[0]
    vec_spec = pl.BlockSpec((None, 1, tn), lambda l, j: (l, 0, j))
    return pl.pallas_call(
        _ada_body,
        grid=(n_l, n_cols // tn),
        in_specs=[
            pl.BlockSpec((rows, D_MODEL), lambda l, j: (0, 0)),
            pl.BlockSpec((None, D_MODEL, tn), lambda l, j: (l, 0, j)),
            vec_spec, vec_spec, vec_spec,
        ],
        out_specs=pl.BlockSpec((None, rows, tn), lambda l, j: (l, 0, j)),
        out_shape=jax.ShapeDtypeStruct((n_l, rows, n_cols), F32),
        compiler_params=_params(("parallel", "parallel")),
        name="ada_project",
    )(c_all, w, b, p, q)


def _scan_tables():
    c = CHUNK
    t = np.arange(c)[:, None]
    u = np.arange(c)[None, :]
    blocks = [(u <= t)]
    for level in MATMUL_LEVELS:
        p = N_LEVELS - level
        odd = ((t >> p) & 1) == 1
        start = (t >> p) << p
        end = (((t >> p) + 1) << p) - 1
        blocks.append(np.where(odd, (u >= start) & (u <= t), (u > t) & (u <= end)))
    sums = np.concatenate(blocks, axis=0).astype(np.float32)
    x = np.arange(c)[:, None] ^ np.arange(c)[None, :]
    msb = np.floor(np.log2(np.maximum(x, 1))).astype(np.int32)
    lvl = np.where(x == 0, 0, N_LEVELS - msb)
    lvl = np.where(np.arange(c)[None, :] > np.arange(c)[:, None], -1, lvl).astype(np.int32)
    return sums, lvl


_SCAN_SUMS, _SCAN_LEVELS = _scan_tables()

_TN = (((1,), (1,)), ((), ()))
_TM = (((0,), (0,)), ((), ()))


def _level_exponent(ex, log2_f, odd_rows, cols, level):
    m = CHUNK >> level
    if level in MATMUL_LEVELS:
        i = 1 + MATMUL_LEVELS.index(level)
        return ex[i * CHUNK:(i + 1) * CHUNK, cols]
    if m == 1:
        return jnp.where(odd_rows, log2_f[:, cols], 0.0)
    parts = []
    for i in range(CHUNK // m):
        blk = ex[i * m:(i + 1) * m, cols]
        if i & 1:
            parts.append(blk - ex[i * m - 1:i * m, cols])
        else:
            parts.append(ex[(i + 1) * m - 1:(i + 1) * m, cols] - blk)
    return jnp.concatenate(parts, axis=0)


def _level_mix(q, kk, odd_rows, level):
    m = CHUNK >> level
    if m >= 8:
        parts = [(q if (i & 1) else kk)[i * m:(i + 1) * m] for i in range(CHUNK // m)]
        return jnp.concatenate(parts, axis=0)
    return jnp.where(odd_rows, q, kk)


def _odd_blocks(x, m):
    return jnp.concatenate([x[i * m:(i + 1) * m] for i in range(1, CHUNK // m, 2)], axis=0)


def _level_product(mix, level):
    m = CHUNK >> level
    if m < 8:
        return lax.dot_general(mix, mix, _TN, preferred_element_type=F32)
    lhs = _odd_blocks(mix, m) if m >= 16 else mix
    a = lax.dot_general(lhs, mix, _TN, preferred_element_type=F32)
    return a if m >= 16 else _odd_blocks(a, m)


def _level_select(att, a, lvl, level):
    m = CHUNK >> level
    if m < 8:
        return jnp.where(lvl == level, a, att)
    tiles = []
    for i in range(CHUNK // m):
        rows = slice(i * m, (i + 1) * m)
        if i & 1:
            a_rows = a[(i // 2) * m:(i // 2 + 1) * m]
            tiles.append(jnp.where(lvl[rows] == level, a_rows, att[rows]))
        else:
            tiles.append(att[rows])
    return jnp.concatenate(tiles, axis=0)


def _hgrn_scan_body(q_ref, k_ref, v_ref, g_ref, lf_ref, gn_ref, sums_ref, lvl_ref,
                    o_ref, st_ref, state_ref, ex_ref):
    n = pl.program_id(1)

    @pl.when(n == 0)
    def _():
        state_ref[...] = jnp.zeros_like(state_ref)

    lvl = lvl_ref[...]
    row = lax.broadcasted_iota(jnp.int32, (CHUNK, A_DK), 0)
    gn = gn_ref[...]
    heads = [slice(h * A_DK, (h + 1) * A_DK) for h in range(A_HEADS)]
    for c in range(STEP_CHUNKS):
        rows = slice(c * CHUNK, (c + 1) * CHUNK)
        ex = ex_ref.at[c]
        log2_f = lf_ref[rows, :]
        hi = log2_f.astype(BF16)
        rem = log2_f - hi.astype(F32)
        mid = rem.astype(BF16)
        lo = (rem - mid.astype(F32)).astype(BF16)
        ex[...] = jnp.dot(sums_ref[...], jnp.concatenate([hi, mid, lo], axis=0),
                          preferred_element_type=F32)
        b_end = ex[CHUNK - 1:CHUNK, :]
        d_end = jnp.exp2(b_end)
        group = range(A_HEADS)
        q_bf = {h: q_ref[rows, heads[h]] for h in group}
        k_bf = {h: k_ref[rows, heads[h]] for h in group}
        q = {h: q_bf[h].astype(F32) for h in group}
        kk = {h: k_bf[h].astype(F32) for h in group}
        att = {h: jnp.where(lvl == 0, lax.dot_general(q_bf[h], k_bf[h], _TN,
                                                      preferred_element_type=F32), 0.0)
               for h in group}
        for level in range(1, N_LEVELS + 1):
            odd_rows = ((row >> (N_LEVELS - level)) & 1) == 1
            for h in group:
                e = jnp.exp2(_level_exponent(ex, log2_f, odd_rows, heads[h], level))
                mix = (_level_mix(q[h], kk[h], odd_rows, level) * e).astype(BF16)
                att[h] = _level_select(att[h], _level_product(mix, level), lvl, level)
        for h in group:
            sl = heads[h]
            b_incl = ex[0:CHUNK, sl]
            q_dec = (q[h] * jnp.exp2(b_incl)).astype(BF16)
            k_dec = (kk[h] * jnp.exp2(b_end[:, sl] - b_incl)).astype(BF16)
            v = v_ref[rows, sl]
            s_t = state_ref[h]
            o = (lax.dot_general(q_dec, s_t.astype(BF16), _TN, preferred_element_type=F32)
                 + jnp.dot(att[h].astype(BF16), v, preferred_element_type=F32))
            state_ref[h] = d_end[:, sl] * s_t + lax.dot_general(
                v, k_dec, _TM, preferred_element_type=F32)
            y = o * lax.rsqrt(jnp.mean(o * o, axis=-1, keepdims=True) + EPS)
            o_ref[rows, sl] = (y * gn * g_ref[rows, sl].astype(F32)).astype(o_ref.dtype)

    @pl.when(n == pl.num_programs(1) - 1)
    def _():
        for h in range(A_HEADS):
            st_ref[0, h] = state_ref[h].T


def hgrn_scan_prompt(qkvg, lf, gn, bsz, t):
    tr = STEP_CHUNKS * CHUNK
    ns = t // tr
    assert t % tr == 0
    sec = lambda k: pl.BlockSpec((tr, D_MODEL), lambda b, n, k=k: (b * ns + n, k))
    n_sum = _SCAN_SUMS.shape[0]
    return pl.pallas_call(
        _hgrn_scan_body,
        grid=(bsz, ns),
        in_specs=[sec(0), sec(1), sec(2), sec(3), sec(0),
                  pl.BlockSpec((1, A_DV), lambda b, n: (0, 0)),
                  pl.BlockSpec((n_sum, 3 * CHUNK), lambda b, n: (0, 0)),
                  pl.BlockSpec((CHUNK, CHUNK), lambda b, n: (0, 0))],
        out_specs=[pl.BlockSpec((tr, D_MODEL), lambda b, n: (b * ns + n, 0)),
                   pl.BlockSpec((1, A_HEADS, A_DK, A_DV), lambda b, n: (b, 0, 0, 0))],
        out_shape=[jax.ShapeDtypeStruct((bsz * t, D_MODEL), BF16),
                   jax.ShapeDtypeStruct((bsz, A_HEADS, A_DK, A_DV), F32)],
        scratch_shapes=[pltpu.VMEM((A_HEADS, A_DV, A_DK), F32),
                        pltpu.VMEM((STEP_CHUNKS, n_sum, D_MODEL), F32)],
        compiler_params=_params(("parallel", "arbitrary")),
        name="hgrn_scan",
    )(qkvg, qkvg, qkvg, qkvg, lf, gn, jnp.asarray(np.tile(_SCAN_SUMS, (1, 3)), BF16),
      jnp.asarray(_SCAN_LEVELS))


def _step_columns(decay, kk, q, v_row):
    r = lax.broadcasted_iota(jnp.int32, (STEP_ROWS, A_DK), 0)
    hi = decay.astype(BF16).astype(F32)
    rem = decay - hi
    mid = rem.astype(BF16).astype(F32)
    lo = rem - mid
    lhs = jnp.where(r == 0, hi, jnp.where(r == 1, mid, jnp.where(r == 2, lo,
                                                                 jnp.where(r == 3, kk, 0.0))))
    parts = [jnp.where(r < 3, 1.0, 0.0), jnp.where(r == 3, v_row, 0.0)]
    if q is not None:
        lhs = jnp.where(r == 4, q, lhs)
        parts.append(jnp.where(r == 4, 1.0, 0.0))
    rhs = jnp.concatenate(parts, axis=1).astype(BF16)
    out = lax.dot_general(lhs.astype(BF16), rhs, _TM, preferred_element_type=F32)
    return [out[:, i * A_DV:(i + 1) * A_DV] for i in range(len(parts))]


def _hgrn_step_body(*refs, n_replay, write_state):
    q_ref, k_ref, v_ref, g_ref, lf_ref, gn_ref, s_ref = refs[:7]
    replay = [refs[7 + 4 * i:11 + 4 * i] for i in range(n_replay)]
    outs = refs[7 + 4 * n_replay:]
    o_ref = outs[0]
    gn = gn_ref[...]
    for b in range(STEP_TOKENS):
        decay = jnp.exp2(lf_ref[b])
        q, kk, v, gate = q_ref[b], k_ref[b], v_ref[b], g_ref[b]
        for h in range(A_HEADS):
            sl = slice(h * A_DK, (h + 1) * A_DK)
            if write_state:
                dec_m, kv_m, q_m = _step_columns(decay[:, sl], kk[:, sl], q[:, sl], v[:, sl])
                s_new = dec_m * s_ref[b, h] + kv_m
                outs[1][b, n_replay, h] = s_new
                o = jnp.sum(q_m * s_new, axis=0, keepdims=True)
            else:
                r = lax.broadcasted_iota(jnp.int32, (STEP_ROWS, A_DK), 0)
                lhs = jnp.where(r == 0, q[:, sl] * decay[:, sl], 0.0).astype(BF16)
                ones = jnp.where(r == 0, 1.0, 0.0).astype(BF16)
                qd_m = lax.dot_general(lhs, ones, _TM, preferred_element_type=F32)
                qk = jnp.sum(q[:, sl] * kk[:, sl], axis=-1, keepdims=True)
                o = jnp.sum(qd_m * s_ref[b, h], axis=0, keepdims=True) + qk * v[:, sl]
            y = o * lax.rsqrt(jnp.mean(o * o, axis=-1, keepdims=True) + EPS)
            o_ref[b, :, sl] = y * gn * gate[:, sl]
        for i, (kp_ref, vp_ref, lfp_ref, sp_ref) in enumerate(replay):
            decay_p = jnp.exp2(lfp_ref[b])
            for h in range(A_HEADS):
                sl = slice(h * A_DK, (h + 1) * A_DK)
                dec_m, kv_m = _step_columns(decay_p[:, sl], kp_ref[b][:, sl], None,
                                            vp_ref[b][:, sl])
                outs[1][b, i, h] = dec_m * sp_ref[b, h] + kv_m


def hgrn_step(qkvg, lf, gn, state, layer, earlier=None):
    bsz = qkvg.shape[0]
    nb = STEP_TOKENS
    assert bsz % nb == 0
    sec = lambda k: pl.BlockSpec((nb, 1, D_MODEL), lambda b, k=k: (b, 0, k))
    state_blk = lambda l: pl.BlockSpec((nb, None, A_HEADS, A_DK, A_DV),
                                       lambda b, l=l: (b, l, 0, 0, 0))
    as3 = lambda a: a.reshape(bsz, 1, a.shape[-1])
    in_specs = [sec(0), sec(1), sec(2), sec(3), sec(0),
                pl.BlockSpec((1, A_DV), lambda b: (0, 0)), state_blk(layer)]
    args = [as3(qkvg)] * 4 + [as3(lf), gn, state]
    write_state = earlier is not None
    for i, (qkvg_p, lf_p) in enumerate(earlier or ()):
        in_specs += [sec(1), sec(2), sec(0), state_blk(i)]
        args += [as3(qkvg_p), as3(qkvg_p), as3(lf_p), state]
    out_specs = [pl.BlockSpec((nb, 1, D_MODEL), lambda b: (b, 0, 0))]
    out_shape = [jax.ShapeDtypeStruct((bsz, 1, D_MODEL), F32)]
    if write_state:
        assert len(earlier) == layer
        n_l = layer + 1
        out_specs.append(pl.BlockSpec((nb, n_l, A_HEADS, A_DK, A_DV), lambda b: (b, 0, 0, 0, 0)))
        out_shape.append(jax.ShapeDtypeStruct((bsz, n_l, A_HEADS, A_DK, A_DV), F32))
    outs = pl.pallas_call(
        functools.partial(_hgrn_step_body, n_replay=len(earlier or ()), write_state=write_state),
        grid=(bsz // nb,),
        in_specs=in_specs,
        out_specs=out_specs,
        out_shape=out_shape,
        compiler_params=_params(("parallel",)),
        name="hgrn_step",
    )(*args)
    o = outs[0].reshape(bsz, D_MODEL)
    return (o, outs[1]) if write_state else (o, None)


def _t5_buckets():
    max_exact = N_BUCKETS // 2
    d = np.arange(WINDOW)
    large = max_exact + (np.log(np.maximum(d, 1).astype(np.float32) / max_exact)
                         / math.log(MAX_DISTANCE / max_exact)
                         * (N_BUCKETS - max_exact)).astype(np.int32)
    large = np.clip(large, 0, N_BUCKETS - 1)
    return np.where(d < max_exact, d, large).astype(np.int32)


_T5_BUCKETS = _t5_buckets()


def _swa_prompt_body(sink_ref, q_ref, kp_ref, kc_ref, vp_ref, vc_ref, bias0_ref, bias_ref, o_ref):
    lane_head = lax.broadcasted_iota(jnp.int32, (WINDOW, GROUP_LANES), 1) >> HEAD_DIM_SHIFT
    from_prev = (lax.broadcasted_iota(jnp.int32, (WINDOW, WINDOW), 1)
                 > lax.broadcasted_iota(jnp.int32, (WINDOW, WINDOW), 0))
    groups = [slice(g * GROUP_LANES, (g + 1) * GROUP_LANES) for g in range(B_KV_HEADS)]
    for blk in range(SWA_BLOCKS):
        rows = slice(blk * WINDOW, (blk + 1) * WINDOW)
        prev_rows = slice((blk - 1) * WINDOW, blk * WINDOW)
        table = bias0_ref if blk == 0 else bias_ref

        def keys(prev_ref, cur_ref, cs):
            prev = prev_ref[:, cs] if blk == 0 else cur_ref[prev_rows, cs]
            return jnp.concatenate([prev, cur_ref[rows, cs]], axis=0)

        all_logits = []
        for cs in groups:
            qg = q_ref[rows, cs].astype(F32)
            qstack = jnp.concatenate(
                [jnp.where(lane_head == j, qg, 0.0).astype(BF16) for j in range(B_GROUPS)], axis=0)
            all_logits.append(lax.dot_general(qstack, keys(kp_ref, kc_ref, cs), _TN,
                                              preferred_element_type=F32))
        for g, cs in enumerate(groups):
            logits = all_logits[g]
            vcat = keys(vp_ref, vc_ref, cs)
            acc = jnp.zeros((WINDOW, GROUP_LANES), F32)
            for j in range(B_GROUPS):
                head = g * B_GROUPS + j
                hr = slice(j * WINDOW, (j + 1) * WINDOW)
                lg = jnp.where(from_prev, logits[hr, :WINDOW], logits[hr, WINDOW:]) + table[head]
                sink = sink_ref[head] * LOG2E
                m = jnp.maximum(jnp.max(lg, axis=-1, keepdims=True), sink)
                p = jnp.exp2(lg - m)
                denom = jnp.sum(p, axis=-1, keepdims=True) + jnp.exp2(sink - m)
                p_prev = jnp.where(from_prev, p, 0.0)
                p_both = jnp.concatenate([p_prev, p - p_prev], axis=1).astype(BF16)
                pv = jnp.dot(p_both, vcat, preferred_element_type=F32)
                acc = jnp.where(lane_head == j, pv / denom, acc)
            o_ref[rows, cs] = acc.astype(o_ref.dtype)


def swa_prompt(q, kv_rep, sinks, bias_tabs, bsz, t):
    tr = SWA_BLOCKS * WINDOW
    ns = t // tr
    assert t % tr == 0
    cur = lambda c: pl.BlockSpec((tr, D_MODEL), lambda b, n, c=c: (b * ns + n, c))
    prev = lambda c: pl.BlockSpec(
        (WINDOW, D_MODEL),
        lambda b, n, c=c: (jnp.maximum((b * ns + n) * SWA_BLOCKS - 1, 0), c))
    table = lambda first: pl.BlockSpec(
        (None, B_HEADS, WINDOW, WINDOW),
        (lambda b, n: (jnp.minimum(n, 1), 0, 0, 0)) if first else (lambda b, n: (1, 0, 0, 0)))
    return pl.pallas_call(
        _swa_prompt_body,
        grid=(bsz, ns),
        in_specs=[pl.BlockSpec(memory_space=pltpu.SMEM),
                  cur(0), prev(0), cur(0), prev(1), cur(1), table(True), table(False)],
        out_specs=pl.BlockSpec((tr, D_MODEL), lambda b, n: (b * ns + n, 0)),
        out_shape=jax.ShapeDtypeStruct((bsz * t, D_MODEL), BF16),
        compiler_params=_params(("parallel", "arbitrary")),
        name="swa_prompt",
    )(sinks, q, kv_rep, kv_rep, kv_rep, kv_rep, bias_tabs, bias_tabs)


def _swa_step_body(*refs, shift_cache):
    sink_ref, qz_ref, k_ref, v_ref, bias_ref, fold_ref = refs[:6]
    if shift_cache:
        kn_ref, vn_ref, o_ref, ko_ref, vo_ref = refs[6:]
        last = lax.broadcasted_iota(jnp.int32, (WINDOW, KV_WIDTH), 0) == WINDOW - 1
    else:
        (o_ref,) = refs[6:]
    sink = sink_ref[...] * LOG2E
    head_kv = lax.broadcasted_iota(jnp.int32, (B_HEADS, KV_WIDTH), 0) >> GROUPS_SHIFT
    lane_kv = lax.broadcasted_iota(jnp.int32, (B_HEADS, KV_WIDTH), 1) >> HEAD_DIM_SHIFT
    for b in range(ATTN_TOKENS):
        if shift_cache:
            k = jnp.where(last, kn_ref[b], pltpu.roll(k_ref[b], WINDOW - 1, 0))
            v = jnp.where(last, vn_ref[b], pltpu.roll(v_ref[b], WINDOW - 1, 0))
            ko_ref[b] = k
            vo_ref[b] = v
        else:
            k, v = k_ref[b], v_ref[b]
        qz = qz_ref[b].astype(BF16)
        logits = lax.dot_general(qz, k.astype(BF16), _TN, preferred_element_type=F32)
        logits = logits + bias_ref[...]
        m = jnp.maximum(jnp.max(logits, axis=-1, keepdims=True), sink)
        p = jnp.exp2(logits - m)
        denom = jnp.sum(p, axis=-1, keepdims=True) + jnp.exp2(sink - m)
        pv = jnp.dot(p.astype(BF16), v.astype(BF16), preferred_element_type=F32) / denom
        own = jnp.where(head_kv == lane_kv, pv, 0.0).astype(BF16)
        o_ref[b] = jnp.dot(own, fold_ref[...], preferred_element_type=F32)


def swa_step(qz, k_cache, v_cache, sinks, bias_row, kv_new=None):
    bsz = qz.shape[0]
    nt = ATTN_TOKENS
    assert bsz % nt == 0
    fold = np.zeros((KV_WIDTH, B_HEAD_DIM), np.float32)
    fold[np.arange(KV_WIDTH), np.arange(KV_WIDTH) % B_HEAD_DIM] = 1.0
    cache_blk = pl.BlockSpec((nt, WINDOW, KV_WIDTH), lambda b: (b, 0, 0))
    in_specs = [pl.BlockSpec((B_HEADS, 1), lambda b: (0, 0)),
                pl.BlockSpec((nt, B_HEADS, KV_WIDTH), lambda b: (b, 0, 0)),
                cache_blk, cache_blk,
                pl.BlockSpec((B_HEADS, WINDOW), lambda b: (0, 0)),
                pl.BlockSpec((KV_WIDTH, B_HEAD_DIM), lambda b: (0, 0))]
    args = [sinks.reshape(B_HEADS, 1), qz, k_cache, v_cache, bias_row, jnp.asarray(fold, BF16)]
    out_specs = [pl.BlockSpec((nt, B_HEADS, B_HEAD_DIM), lambda b: (b, 0, 0))]
    out_shape = [jax.ShapeDtypeStruct((bsz, B_HEADS, B_HEAD_DIM), F32)]
    if kv_new is not None:
        row_blk = pl.BlockSpec((nt, 1, KV_WIDTH), lambda b: (b, 0, 0))
        in_specs += [row_blk, row_blk]
        args += list(kv_new)
        out_specs += [cache_blk, cache_blk]
        out_shape += [jax.ShapeDtypeStruct(k_cache.shape, F32)] * 2
    outs = pl.pallas_call(
        functools.partial(_swa_step_body, shift_cache=kv_new is not None),
        grid=(bsz // nt,),
        in_specs=in_specs,
        out_specs=out_specs,
        out_shape=out_shape,
        compiler_params=_params(("parallel",)),
        name="swa_step",
    )(*args)
    return outs if kv_new is not None else outs[0]


def _trunk(x, mods, kv_mod, per_token, hgrn_state0, k_buf, v_buf, wts, lbs, bias_tab, bias_row):
    (w_in_a, w_o_a, gnorm_a, w_q_b, w_o_b, sinks_b, w_ffn_in, w_ffn_out, final_norm_w,
     w_kv) = wts
    prompt = k_buf is None
    bsz, t, _ = x.shape
    m = bsz * t
    common = dict(per_token=per_token, rows_per_batch=t)
    h = x.reshape(m, D_MODEL)
    states = []
    step_inputs = []
    new_state = None
    kv = k_state = v_state = kv_new = None
    for l in range(DEPTH):
        a1, s1, g1, a2, s2, g2 = mods[l]
        if l < N_A_LAYERS:
            qkvg, lf = norm_proj(h, (a1, s1), w_in_a, l, epilogue="hgrn",
                                 lb=lbs[l].reshape(1, D_MODEL), **common)
            gn = gnorm_a[l].reshape(1, A_DV)
            if prompt:
                mix, s_new = hgrn_scan_prompt(qkvg, lf, gn, bsz, t)
                states.append(s_new)
            else:
                qkvg = qkvg.astype(F32)
                last_a = l == N_A_LAYERS - 1
                mix, new_state = hgrn_step(qkvg, lf, gn, hgrn_state0, l,
                                           earlier=list(step_inputs) if last_a else None)
                step_inputs.append((qkvg, lf))
            w_o, lo = w_o_a, l
        else:
            j = l - N_A_LAYERS
            scale = LOG2E / math.sqrt(B_HEAD_DIM)
            if prompt:
                q = norm_proj(h, (a1, s1), w_q_b, j, out_scale=scale, **common)
                mix = swa_prompt(q, kv, sinks_b[j], bias_tab, bsz, t)
            else:
                q = norm_proj(h, (a1, s1), w_q_b, j, out_scale=scale, out_dtype=F32, **common)
                lane_kv = (np.arange(KV_WIDTH) // B_HEAD_DIM)[None, :]
                head_kv = (np.arange(B_HEADS) // B_GROUPS)[:, None]
                qz = jnp.where((lane_kv == head_kv)[None],
                               jnp.tile(q.reshape(m, B_HEADS, B_HEAD_DIM), (1, 1, B_KV_HEADS)), 0.0)
                if j == 0:
                    mix, k_state, v_state = swa_step(qz, k_buf.reshape(m, WINDOW, KV_WIDTH),
                                                     v_buf.reshape(m, WINDOW, KV_WIDTH),
                                                     sinks_b[j], bias_row, kv_new)
                else:
                    mix = swa_step(qz, k_state, v_state, sinks_b[j], bias_row)
                mix = mix.reshape(m, D_MODEL)
            w_o, lo = w_o_b, j
        h = post_ffn(mix, h, g1, (a2, s2), g2, w_o, lo, w_ffn_in, w_ffn_out, l,
                     final_w=final_norm_w if l == DEPTH - 1 else None, **common)
        if l == N_A_LAYERS - 1:
            if prompt:
                kv = norm_proj(h, kv_mod, w_kv, 0, epilogue="kv_rep", **common)
                tail = h.reshape(bsz, t, D_MODEL)[:, -WINDOW:].reshape(bsz * WINDOW, D_MODEL)
                kv_tail = norm_proj(tail, kv_mod, w_kv, 0, per_token=False,
                                    rows_per_batch=WINDOW, out_dtype=F32)
                kv_tail = kv_tail.reshape(bsz, WINDOW, 2, B_KV_HEADS, B_HEAD_DIM)
                k_state, v_state = kv_tail[:, :, 0], kv_tail[:, :, 1]
            else:
                kv_row = norm_proj(h, kv_mod, w_kv, 0, out_dtype=F32, **common)
                kv_new = (kv_row[:, :KV_WIDTH].reshape(m, 1, KV_WIDTH),
                          kv_row[:, KV_WIDTH:].reshape(m, 1, KV_WIDTH))
    y = h.reshape(bsz, t, D_MODEL)
    if prompt:
        return y, jnp.stack(states, axis=1), k_state, v_state
    cache_shape = (m, WINDOW, B_KV_HEADS, B_HEAD_DIM)
    return y, new_state, k_state.reshape(cache_shape), v_state.reshape(cache_shape)


def kernel(x_prompt, x_sample, state_hgrn, cache_swa_k, cache_swa_v, c_prompt, c_sample,
           w_in_a, w_o_a, gnorm_a, lb_a, w_kv, w_ada_kv, b_ada_kv, kv_norm_w, w_q_b, w_o_b,
           sinks_b, rel_bias, norm_w, w_ada, b_ada, w_ffn_in, w_ffn_out, final_norm_w):
    n_p, n_s = c_prompt.shape[0], c_sample.shape[0]
    rows = -(-(n_p + n_s) // SUBLANES) * SUBLANES
    c_all = jnp.concatenate(
        [c_sample, c_prompt, jnp.zeros((rows - n_p - n_s, D_MODEL), F32)], axis=0)

    zeros, ones = jnp.zeros((DEPTH, D_MODEL), F32), jnp.ones((DEPTH, D_MODEL), F32)
    p_ada = jnp.stack([zeros, norm_w[:, 0], zeros, zeros, norm_w[:, 1], zeros], axis=1)
    q_ada = jnp.stack([ones, norm_w[:, 0], ones, ones, norm_w[:, 1], ones], axis=1)
    ada = ada_project(c_all, w_ada, b_ada.reshape(DEPTH, 1, 6 * D_MODEL),
                      p_ada.reshape(DEPTH, 1, 6 * D_MODEL), q_ada.reshape(DEPTH, 1, 6 * D_MODEL))
    p_kv = jnp.concatenate([zeros[0], kv_norm_w]).reshape(1, 1, 2 * D_MODEL)
    q_kv = jnp.concatenate([ones[0], kv_norm_w]).reshape(1, 1, 2 * D_MODEL)
    ada_kv = ada_project(c_all, w_ada_kv.reshape(1, D_MODEL, 2 * D_MODEL),
                         b_ada_kv.reshape(1, 1, 2 * D_MODEL), p_kv, q_kv)

    def mods_for(row0):
        layers = [tuple((ada, l, c, row0) for c in (1, 0, 2, 4, 3, 5)) for l in range(DEPTH)]
        return layers, ((ada_kv, 0, 1, row0), (ada_kv, 0, 0, row0))

    lb_sm = jax.nn.softmax(lb_a.astype(F32), axis=0)
    lbs = jnp.cumsum(lb_sm, axis=0) - lb_sm[0:1]

    rb = rel_bias.astype(F32)[_T5_BUCKETS]
    t_idx, j_idx = np.arange(WINDOW)[:, None], np.arange(WINDOW)[None, :]
    from_prev = j_idx > t_idx
    dist = np.where(from_prev, t_idx + WINDOW - j_idx, t_idx - j_idx)
    onehot = (jnp.asarray(dist)[:, :, None] == jnp.arange(WINDOW)[None, None, :]).astype(F32)
    tab = jnp.einsum("tjd,dh->htj", onehot, rb, precision=lax.Precision.HIGHEST) * LOG2E
    bias_tab = jnp.stack([jnp.where(from_prev, MASK_VALUE, tab),
                          tab])
    bias_row = rb[::-1].T * LOG2E

    bf = lambda w: w.astype(BF16)
    wts = (bf(w_in_a), bf(w_o_a), gnorm_a, bf(w_q_b), bf(w_o_b), sinks_b, bf(w_ffn_in),
           bf(w_ffn_out), final_norm_w, bf(w_kv)[None])
    mods_p, kv_mod_p = mods_for(n_s)
    mods_s, kv_mod_s = mods_for(0)
    y_p, st_p, k_p, v_p = _trunk(x_prompt, mods_p, kv_mod_p, False, None, None, None, wts, lbs,
                                 bias_tab, bias_row)
    y_s, st_s, k_s, v_s = _trunk(x_sample, mods_s, kv_mod_s, True, state_hgrn, cache_swa_k,
                                 cache_swa_v, wts, lbs, bias_tab, bias_row)
    return (y_p, y_s, st_p, st_s, k_p, v_p, k_s, v_s)
```

```python
import functools
import math

import numpy as np
import jax
import jax.numpy as jnp
from jax import lax
from jax.experimental import pallas as pl
from jax.experimental.pallas import tpu as pltpu

F32 = jnp.float32
BF16 = jnp.bfloat16

D_MODEL = 1024
DEPTH = 4
N_A_LAYERS = 2
A_HEADS = 8
A_DK = 128
A_DV = 128
F_MIN = 1e-30
B_HEAD_DIM = 64
B_HEADS = 16
B_KV_HEADS = 4
B_GROUPS = 4
WINDOW = 128
MASK_VALUE = -1e30
N_BUCKETS = 32
MAX_DISTANCE = 128
D_FF = 2816
EPS = 1e-6
GROUP_LANES = B_GROUPS * B_HEAD_DIM
KV_WIDTH = B_KV_HEADS * B_HEAD_DIM
HEAD_DIM_SHIFT = B_HEAD_DIM.bit_length() - 1
GROUPS_SHIFT = B_GROUPS.bit_length() - 1

LOG2E = math.log2(math.e)

SUBLANES = 8
LANES = 128
BF16_SUBLANES = 16

CHUNK = 128
N_LEVELS = 7
MATMUL_LEVELS = (5, 6)
STEP_CHUNKS = 8
ROW_TILE = 1024
COL_TILE = 1024
FF_TILE = 256
ADA_COLS = 2048
SWA_BLOCKS = 8
STEP_TOKENS = 4
ATTN_TOKENS = 16
STEP_ROWS = BF16_SUBLANES
BATCH_ROWS = SUBLANES
VMEM_LIMIT_BYTES = 58 * 1024 * 1024


def _params(sem):
    return pltpu.CompilerParams(dimension_semantics=sem, vmem_limit_bytes=VMEM_LIMIT_BYTES)


def _row_tile(m, per_token, rows_per_batch):
    return min(m, ROW_TILE) if per_token else min(m, ROW_TILE, rows_per_batch)


def _sigmoid(x):
    return 0.5 * jnp.tanh(0.5 * x) + 0.5


def _silu(x):
    h = 0.5 * x
    return h * jnp.tanh(h) + h


def _norm_mod(x, a, s):
    y = x * lax.rsqrt(jnp.mean(x * x, axis=-1, keepdims=True) + EPS)
    return (y * a + s).astype(BF16)


def _resident(shape, layer):
    return pl.BlockSpec((None,) + shape, lambda i: (layer,) + (0,) * len(shape),
                        pipeline_mode=pl.Buffered(1))


def _mod_operand(mod, tm, per_token):
    arr, layer, chunk, row0 = mod
    if per_token:
        return arr, pl.BlockSpec((None, tm, D_MODEL), lambda i: (layer, i, chunk))
    assert row0 % BATCH_ROWS == 0
    return arr, pl.BlockSpec((None, BATCH_ROWS, D_MODEL),
                             lambda i: (layer, row0 // BATCH_ROWS, chunk))


def _row_value(ref, tiles_per_batch):
    if tiles_per_batch is None:
        return ref[...]
    return ref[pl.ds(lax.div(pl.program_id(0), tiles_per_batch), 1), :]


def _tiles_per_batch(m, tm, per_token, rows_per_batch):
    if per_token:
        return None
    assert rows_per_batch % tm == 0 and m // rows_per_batch <= BATCH_ROWS
    return rows_per_batch // tm


def _norm_proj_body(*refs, tiles_per_batch, epilogue, out_scale):
    x_ref, a_ref, s_ref, w_ref = refs[:4]
    xn = _norm_mod(x_ref[...], _row_value(a_ref, tiles_per_batch),
                   _row_value(s_ref, tiles_per_batch))

    tn = min(COL_TILE, w_ref.shape[1])

    def proj(c):
        return jnp.dot(xn, w_ref[:, c:c + tn], preferred_element_type=F32)

    if epilogue == "plain":
        (o_ref,) = refs[4:]
        for c in range(0, o_ref.shape[1], tn):
            acc = proj(c)
            if out_scale is not None:
                acc = acc * out_scale
            o_ref[:, c:c + tn] = acc.astype(o_ref.dtype)
    elif epilogue == "kv_rep":
        (o_ref,) = refs[4:]
        acc = proj(0)
        low_half = lax.broadcasted_iota(jnp.int32, (acc.shape[0], LANES), 1) < B_HEAD_DIM
        for c in range(acc.shape[1] // LANES):
            x = acc[:, c * LANES:(c + 1) * LANES]
            swapped = pltpu.roll(x, B_HEAD_DIM, 1)
            for half, rep in enumerate((jnp.where(low_half, x, swapped),
                                        jnp.where(low_half, swapped, x))):
                rep = rep.astype(o_ref.dtype)
                base = (2 * c + half) * GROUP_LANES
                o_ref[:, base:base + LANES] = rep
                o_ref[:, base + LANES:base + GROUP_LANES] = rep
    elif epilogue == "hgrn":
        lb_ref, o_ref, lf_ref = refs[4:]
        d = D_MODEL
        lb = lb_ref[...]
        o_ref[:, 0:d] = _silu(proj(0)).astype(o_ref.dtype)
        f = lb + (1.0 - lb) * _sigmoid(proj(d))
        o_ref[:, d:2 * d] = (1.0 - f).astype(o_ref.dtype)
        lf_ref[...] = jnp.log2(jnp.maximum(f, F_MIN))
        o_ref[:, 2 * d:3 * d] = proj(2 * d).astype(o_ref.dtype)
        o_ref[:, 3 * d:4 * d] = _silu(proj(3 * d)).astype(o_ref.dtype)
    else:
        raise ValueError(epilogue)


def norm_proj(x, mod, w, layer, *, per_token, rows_per_batch, epilogue="plain", lb=None,
              out_dtype=BF16, out_scale=None):
    m, k = x.shape
    n = w.shape[-1]
    tm = _row_tile(m, per_token, rows_per_batch)
    assert m % tm == 0 and (per_token or rows_per_batch % tm == 0)
    (a_arr, a_spec), (s_arr, s_spec) = [_mod_operand(v, tm, per_token) for v in mod]
    in_specs = [pl.BlockSpec((tm, k), lambda i: (i, 0)), a_spec, s_spec, _resident((k, n), layer)]
    args = [x, a_arr, s_arr, w]
    n_out = n * B_GROUPS if epilogue == "kv_rep" else n
    out_specs = [pl.BlockSpec((tm, n_out), lambda i: (i, 0))]
    out_shape = [jax.ShapeDtypeStruct((m, n_out), out_dtype)]
    if epilogue == "hgrn":
        in_specs.append(pl.BlockSpec((1, D_MODEL), lambda i: (0, 0)))
        args.append(lb)
        out_specs.append(pl.BlockSpec((tm, D_MODEL), lambda i: (i, 0)))
        out_shape.append(jax.ShapeDtypeStruct((m, D_MODEL), F32))
    outs = pl.pallas_call(
        functools.partial(_norm_proj_body, epilogue=epilogue, out_scale=out_scale,
                          tiles_per_batch=_tiles_per_batch(m, tm, per_token, rows_per_batch)),
        grid=(m // tm,),
        in_specs=in_specs,
        out_specs=out_specs,
        out_shape=out_shape,
        compiler_params=_params(("parallel",)),
        name=f"norm_proj_{epilogue}_m{m}_n{n}",
    )(*args)
    return outs if epilogue == "hgrn" else outs[0]


def _post_ffn_body(*refs, tiles_per_batch, final_norm):
    (mix_ref, h_ref, g1_ref, a2_ref, s2_ref, g2_ref, wo_ref, win_ref, wout_ref) = refs[:9]
    rest = refs[9:]
    fw_ref = rest[0] if final_norm else None
    o_ref, act_ref = rest[-2:]
    row = functools.partial(_row_value, tiles_per_batch=tiles_per_batch)
    h_mid = h_ref[...] + row(g1_ref) * jnp.dot(mix_ref[...].astype(BF16), wo_ref[...],
                                               preferred_element_type=F32)
    xn = _norm_mod(h_mid, row(a2_ref), row(s2_ref))
    for c in range(0, D_FF, FF_TILE):
        gate = jnp.dot(xn, win_ref[:, c:c + FF_TILE], preferred_element_type=F32)
        up = jnp.dot(xn, win_ref[:, D_FF + c:D_FF + c + FF_TILE], preferred_element_type=F32)
        act_ref[:, c:c + FF_TILE] = (_silu(gate) * up).astype(BF16)
    out = h_mid + row(g2_ref) * jnp.dot(act_ref[...], wout_ref[...], preferred_element_type=F32)
    if final_norm:
        out = out * lax.rsqrt(jnp.mean(out * out, axis=-1, keepdims=True) + EPS) * fw_ref[...]
    o_ref[...] = out


def post_ffn(mix, h, g1, mod2, g2, w_o, lo, w_ffn_in, w_ffn_out, lf, *, per_token,
             rows_per_batch, final_w=None):
    m, d = h.shape
    tm = _row_tile(m, per_token, rows_per_batch)
    assert m % tm == 0 and (per_token or rows_per_batch % tm == 0)
    rows = [_mod_operand(v, tm, per_token) for v in (g1, mod2[0], mod2[1], g2)]
    tile = pl.BlockSpec((tm, d), lambda i: (i, 0))
    in_specs = [tile, tile] + [spec for _, spec in rows] + [
        _resident((d, d), lo), _resident((d, 2 * D_FF), lf), _resident((D_FF, d), lf)]
    args = [mix, h] + [arr for arr, _ in rows] + [w_o, w_ffn_in, w_ffn_out]
    if final_w is not None:
        in_specs.append(pl.BlockSpec((1, d), lambda i: (0, 0)))
        args.append(final_w.reshape(1, d))
    return pl.pallas_call(
        functools.partial(_post_ffn_body, final_norm=final_w is not None,
                          tiles_per_batch=_tiles_per_batch(m, tm, per_token, rows_per_batch)),
        grid=(m // tm,),
        in_specs=in_specs,
        out_specs=tile,
        out_shape=jax.ShapeDtypeStruct((m, d), F32),
        scratch_shapes=[pltpu.VMEM((tm, D_FF), BF16)],
        compiler_params=_params(("parallel",)),
        name=f"post_ffn_m{m}",
    )(*args)


def _ada_body(c_ref, w_ref, b_ref, p_ref, q_ref, o_ref):
    c = _silu(c_ref[...]).astype(BF16)
    acc = jnp.dot(c, w_ref[...].astype(BF16), preferred_element_type=F32)
    o_ref[...] = p_ref[...] + q_ref[...] * (acc + b_ref[...])


def ada_project(c_all, w, b, p, q):
    n_l, _, n_cols = w.shape
    tn = min(ADA_COLS, n_cols)
    assert n_cols % tn == 0
    rows = c_all.shape[0]
    vec_spec = pl.BlockSpec((None, 1, tn), lambda l, j: (l, 0, j))
    return pl.pallas_call(
        _ada_body,
        grid=(n_l, n_cols // tn),
        in_specs=[
            pl.BlockSpec((rows, D_MODEL), lambda l, j: (0, 0)),
            pl.BlockSpec((None, D_MODEL, tn), lambda l, j: (l, 0, j)),
            vec_spec, vec_spec, vec_spec,
        ],
        out_specs=pl.BlockSpec((None, rows, tn), lambda l, j: (l, 0, j)),
        out_shape=jax.ShapeDtypeStruct((n_l, rows, n_cols), F32),
        compiler_params=_params(("parallel", "parallel")),
        name="ada_project",
    )(c_all, w, b, p, q)


def _scan_tables():
    c = CHUNK
    t = np.arange(c)[:, None]
    u = np.arange(c)[None, :]
    blocks = [(u <= t)]
    for level in MATMUL_LEVELS:
        p = N_LEVELS - level
        odd = ((t >> p) & 1) == 1
        start = (t >> p) << p
        end = (((t >> p) + 1) << p) - 1
        blocks.append(np.where(odd, (u >= start) & (u <= t), (u > t) & (u <= end)))
    sums = np.concatenate(blocks, axis=0).astype(np.float32)
    x = np.arange(c)[:, None] ^ np.arange(c)[None, :]
    msb = np.floor(np.log2(np.maximum(x, 1))).astype(np.int32)
    lvl = np.where(x == 0, 0, N_LEVELS - msb)
    lvl = np.where(np.arange(c)[None, :] > np.arange(c)[:, None], -1, lvl).astype(np.int32)
    return sums, lvl


_SCAN_SUMS, _SCAN_LEVELS = _scan_tables()

_TN = (((1,), (1,)), ((), ()))
_TM = (((0,), (0,)), ((), ()))


def _level_exponent(ex, log2_f, odd_rows, cols, level):
    m = CHUNK >> level
    if level in MATMUL_LEVELS:
        i = 1 + MATMUL_LEVELS.index(level)
        return ex[i * CHUNK:(i + 1) * CHUNK, cols]
    if m == 1:
        return jnp.where(odd_rows, log2_f[:, cols], 0.0)
    parts = []
    for i in range(CHUNK // m):
        blk = ex[i * m:(i + 1) * m, cols]
        if i & 1:
            parts.append(blk - ex[i * m - 1:i * m, cols])
        else:
            parts.append(ex[(i + 1) * m - 1:(i + 1) * m, cols] - blk)
    return jnp.concatenate(parts, axis=0)


def _level_mix(q, kk, odd_rows, level):
    m = CHUNK >> level
    if m >= 8:
        parts = [(q if (i & 1) else kk)[i * m:(i + 1) * m] for i in range(CHUNK // m)]
        return jnp.concatenate(parts, axis=0)
    return jnp.where(odd_rows, q, kk)


def _odd_blocks(x, m):
    return jnp.concatenate([x[i * m:(i + 1) * m] for i in range(1, CHUNK // m, 2)], axis=0)


def _level_product(mix, level):
    m = CHUNK >> level
    if m < 8:
        return lax.dot_general(mix, mix, _TN, preferred_element_type=F32)
    lhs = _odd_blocks(mix, m) if m >= 16 else mix
    a = lax.dot_general(lhs, mix, _TN, preferred_element_type=F32)
    return a if m >= 16 else _odd_blocks(a, m)


def _level_select(att, a, lvl, level):
    m = CHUNK >> level
    if m < 8:
        return jnp.where(lvl == level, a, att)
    tiles = []
    for i in range(CHUNK // m):
        rows = slice(i * m, (i + 1) * m)
        if i & 1:
            a_rows = a[(i // 2) * m:(i // 2 + 1) * m]
            tiles.append(jnp.where(lvl[rows] == level, a_rows, att[rows]))
        else:
            tiles.append(att[rows])
    return jnp.concatenate(tiles, axis=0)


def _hgrn_scan_body(q_ref, k_ref, v_ref, g_ref, lf_ref, gn_ref, sums_ref, lvl_ref,
                    o_ref, st_ref, state_ref, ex_ref):
    n = pl.program_id(1)

    @pl.when(n == 0)
    def _():
        state_ref[...] = jnp.zeros_like(state_ref)

    lvl = lvl_ref[...]
    row = lax.broadcasted_iota(jnp.int32, (CHUNK, A_DK), 0)
    gn = gn_ref[...]
    heads = [slice(h * A_DK, (h + 1) * A_DK) for h in range(A_HEADS)]
    for c in range(STEP_CHUNKS):
        rows = slice(c * CHUNK, (c + 1) * CHUNK)
        ex = ex_ref.at[c]
        log2_f = lf_ref[rows, :]
        hi = log2_f.astype(BF16)
        rem = log2_f - hi.astype(F32)
        mid = rem.astype(BF16)
        lo = (rem - mid.astype(F32)).astype(BF16)
        ex[...] = jnp.dot(sums_ref[...], jnp.concatenate([hi, mid, lo], axis=0),
                          preferred_element_type=F32)
        b_end = ex[CHUNK - 1:CHUNK, :]
        d_end = jnp.exp2(b_end)
        group = range(A_HEADS)
        q_bf = {h: q_ref[rows, heads[h]] for h in group}
        k_bf = {h: k_ref[rows, heads[h]] for h in group}
        q = {h: q_bf[h].astype(F32) for h in group}
        kk = {h: k_bf[h].astype(F32) for h in group}
        att = {h: jnp.where(lvl == 0, lax.dot_general(q_bf[h], k_bf[h], _TN,
                                                      preferred_element_type=F32), 0.0)
               for h in group}
        for level in range(1, N_LEVELS + 1):
            odd_rows = ((row >> (N_LEVELS - level)) & 1) == 1
            for h in group:
                e = jnp.exp2(_level_exponent(ex, log2_f, odd_rows, heads[h], level))
                mix = (_level_mix(q[h], kk[h], odd_rows, level) * e).astype(BF16)
                att[h] = _level_select(att[h], _level_product(mix, level), lvl, level)
        for h in group:
            sl = heads[h]
            b_incl = ex[0:CHUNK, sl]
            q_dec = (q[h] * jnp.exp2(b_incl)).astype(BF16)
            k_dec = (kk[h] * jnp.exp2(b_end[:, sl] - b_incl)).astype(BF16)
            v = v_ref[rows, sl]
            s_t = state_ref[h]
            o = (lax.dot_general(q_dec, s_t.astype(BF16), _TN, preferred_element_type=F32)
                 + jnp.dot(att[h].astype(BF16), v, preferred_element_type=F32))
            state_ref[h] = d_end[:, sl] * s_t + lax.dot_general(
                v, k_dec, _TM, preferred_element_type=F32)
            y = o * lax.rsqrt(jnp.mean(o * o, axis=-1, keepdims=True) + EPS)
            o_ref[rows, sl] = (y * gn * g_ref[rows, sl].astype(F32)).astype(o_ref.dtype)

    @pl.when(n == pl.num_programs(1) - 1)
    def _():
        for h in range(A_HEADS):
            st_ref[0, h] = state_ref[h].T


def hgrn_scan_prompt(qkvg, lf, gn, bsz, t):
    tr = STEP_CHUNKS * CHUNK
    ns = t // tr
    assert t % tr == 0
    sec = lambda k: pl.BlockSpec((tr, D_MODEL), lambda b, n, k=k: (b * ns + n, k))
    n_sum = _SCAN_SUMS.shape[0]
    return pl.pallas_call(
        _hgrn_scan_body,
        grid=(bsz, ns),
        in_specs=[sec(0), sec(1), sec(2), sec(3), sec(0),
                  pl.BlockSpec((1, A_DV), lambda b, n: (0, 0)),
                  pl.BlockSpec((n_sum, 3 * CHUNK), lambda b, n: (0, 0)),
                  pl.BlockSpec((CHUNK, CHUNK), lambda b, n: (0, 0))],
        out_specs=[pl.BlockSpec((tr, D_MODEL), lambda b, n: (b * ns + n, 0)),
                   pl.BlockSpec((1, A_HEADS, A_DK, A_DV), lambda b, n: (b, 0, 0, 0))],
        out_shape=[jax.ShapeDtypeStruct((bsz * t, D_MODEL), BF16),
                   jax.ShapeDtypeStruct((bsz, A_HEADS, A_DK, A_DV), F32)],
        scratch_shapes=[pltpu.VMEM((A_HEADS, A_DV, A_DK), F32),
                        pltpu.VMEM((STEP_CHUNKS, n_sum, D_MODEL), F32)],
        compiler_params=_params(("parallel", "arbitrary")),
        name="hgrn_scan",
    )(qkvg, qkvg, qkvg, qkvg, lf, gn, jnp.asarray(np.tile(_SCAN_SUMS, (1, 3)), BF16),
      jnp.asarray(_SCAN_LEVELS))


def _step_columns(decay, kk, q, v_row):
    r = lax.broadcasted_iota(jnp.int32, (STEP_ROWS, A_DK), 0)
    hi = decay.astype(BF16).astype(F32)
    rem = decay - hi
    mid = rem.astype(BF16).astype(F32)
    lo = rem - mid
    lhs = jnp.where(r == 0, hi, jnp.where(r == 1, mid, jnp.where(r == 2, lo,
                                                                 jnp.where(r == 3, kk, 0.0))))
    parts = [jnp.where(r < 3, 1.0, 0.0), jnp.where(r == 3, v_row, 0.0)]
    if q is not None:
        lhs = jnp.where(r == 4, q, lhs)
        parts.append(jnp.where(r == 4, 1.0, 0.0))
    rhs = jnp.concatenate(parts, axis=1).astype(BF16)
    out = lax.dot_general(lhs.astype(BF16), rhs, _TM, preferred_element_type=F32)
    return [out[:, i * A_DV:(i + 1) * A_DV] for i in range(len(parts))]


def _hgrn_step_body(*refs, n_replay, write_state):
    q_ref, k_ref, v_ref, g_ref, lf_ref, gn_ref, s_ref = refs[:7]
    replay = [refs[7 + 4 * i:11 + 4 * i] for i in range(n_replay)]
    outs = refs[7 + 4 * n_replay:]
    o_ref = outs[0]
    gn = gn_ref[...]
    for b in range(STEP_TOKENS):
        decay = jnp.exp2(lf_ref[b])
        q, kk, v, gate = q_ref[b], k_ref[b], v_ref[b], g_ref[b]
        for h in range(A_HEADS):
            sl = slice(h * A_DK, (h + 1) * A_DK)
            if write_state:
                dec_m, kv_m, q_m = _step_columns(decay[:, sl], kk[:, sl], q[:, sl], v[:, sl])
                s_new = dec_m * s_ref[b, h] + kv_m
                outs[1][b, n_replay, h] = s_new
                o = jnp.sum(q_m * s_new, axis=0, keepdims=True)
            else:
                r = lax.broadcasted_iota(jnp.int32, (STEP_ROWS, A_DK), 0)
                lhs = jnp.where(r == 0, q[:, sl] * decay[:, sl], 0.0).astype(BF16)
                ones = jnp.where(r == 0, 1.0, 0.0).astype(BF16)
                qd_m = lax.dot_general(lhs, ones, _TM, preferred_element_type=F32)
                qk = jnp.sum(q[:, sl] * kk[:, sl], axis=-1, keepdims=True)
                o = jnp.sum(qd_m * s_ref[b, h], axis=0, keepdims=True) + qk * v[:, sl]
            y = o * lax.rsqrt(jnp.mean(o * o, axis=-1, keepdims=True) + EPS)
            o_ref[b, :, sl] = y * gn * gate[:, sl]
        for i, (kp_ref, vp_ref, lfp_ref, sp_ref) in enumerate(replay):
            decay_p = jnp.exp2(lfp_ref[b])
            for h in range(A_HEADS):
                sl = slice(h * A_DK, (h + 1) * A_DK)
                dec_m, kv_m = _step_columns(decay_p[:, sl], kp_ref[b][:, sl], None,
                                            vp_ref[b][:, sl])
                outs[1][b, i, h] = dec_m * sp_ref[b, h] + kv_m


def hgrn_step(qkvg, lf, gn, state, layer, earlier=None):
    bsz = qkvg.shape[0]
    nb = STEP_TOKENS
    assert bsz % nb == 0
    sec = lambda k: pl.BlockSpec((nb, 1, D_MODEL), lambda b, k=k: (b, 0, k))
    state_blk = lambda l: pl.BlockSpec((nb, None, A_HEADS, A_DK, A_DV),
                                       lambda b, l=l: (b, l, 0, 0, 0))
    as3 = lambda a: a.reshape(bsz, 1, a.shape[-1])
    in_specs = [sec(0), sec(1), sec(2), sec(3), sec(0),
                pl.BlockSpec((1, A_DV), lambda b: (0, 0)), state_blk(layer)]
    args = [as3(qkvg)] * 4 + [as3(lf), gn, state]
    write_state = earlier is not None
    for i, (qkvg_p, lf_p) in enumerate(earlier or ()):
        in_specs += [sec(1), sec(2), sec(0), state_blk(i)]
        args += [as3(qkvg_p), as3(qkvg_p), as3(lf_p), state]
    out_specs = [pl.BlockSpec((nb, 1, D_MODEL), lambda b: (b, 0, 0))]
    out_shape = [jax.ShapeDtypeStruct((bsz, 1, D_MODEL), F32)]
    if write_state:
        assert len(earlier) == layer
        n_l = layer + 1
        out_specs.append(pl.BlockSpec((nb, n_l, A_HEADS, A_DK, A_DV), lambda b: (b, 0, 0, 0, 0)))
        out_shape.append(jax.ShapeDtypeStruct((bsz, n_l, A_HEADS, A_DK, A_DV), F32))
    outs = pl.pallas_call(
        functools.partial(_hgrn_step_body, n_replay=len(earlier or ()), write_state=write_state),
        grid=(bsz // nb,),
        in_specs=in_specs,
        out_specs=out_specs,
        out_shape=out_shape,
        compiler_params=_params(("parallel",)),
        name="hgrn_step",
    )(*args)
    o = outs[0].reshape(bsz, D_MODEL)
    return (o, outs[1]) if write_state else (o, None)


def _t5_buckets():
    max_exact = N_BUCKETS // 2
    d = np.arange(WINDOW)
    large = max_exact + (np.log(np.maximum(d, 1).astype(np.float32) / max_exact)
                         / math.log(MAX_DISTANCE / max_exact)
                         * (N_BUCKETS - max_exact)).astype(np.int32)
    large = np.clip(large, 0, N_BUCKETS - 1)
    return np.where(d < max_exact, d, large).astype(np.int32)


_T5_BUCKETS = _t5_buckets()


def _swa_prompt_body(sink_ref, q_ref, kp_ref, kc_ref, vp_ref, vc_ref, bias0_ref, bias_ref, o_ref):
    lane_head = lax.broadcasted_iota(jnp.int32, (WINDOW, GROUP_LANES), 1) >> HEAD_DIM_SHIFT
    from_prev = (lax.broadcasted_iota(jnp.int32, (WINDOW, WINDOW), 1)
                 > lax.broadcasted_iota(jnp.int32, (WINDOW, WINDOW), 0))
    groups = [slice(g * GROUP_LANES, (g + 1) * GROUP_LANES) for g in range(B_KV_HEADS)]
    for blk in range(SWA_BLOCKS):
        rows = slice(blk * WINDOW, (blk + 1) * WINDOW)
        prev_rows = slice((blk - 1) * WINDOW, blk * WINDOW)
        table = bias0_ref if blk == 0 else bias_ref

        def keys(prev_ref, cur_ref, cs):
            prev = prev_ref[:, cs] if blk == 0 else cur_ref[prev_rows, cs]
            return jnp.concatenate([prev, cur_ref[rows, cs]], axis=0)

        all_logits = []
        for cs in groups:
            qg = q_ref[rows, cs].astype(F32)
            qstack = jnp.concatenate(
                [jnp.where(lane_head == j, qg, 0.0).astype(BF16) for j in range(B_GROUPS)], axis=0)
            all_logits.append(lax.dot_general(qstack, keys(kp_ref, kc_ref, cs), _TN,
                                              preferred_element_type=F32))
        for g, cs in enumerate(groups):
            logits = all_logits[g]
            vcat = keys(vp_ref, vc_ref, cs)
            acc = jnp.zeros((WINDOW, GROUP_LANES), F32)
            for j in range(B_GROUPS):
                head = g * B_GROUPS + j
                hr = slice(j * WINDOW, (j + 1) * WINDOW)
                lg = jnp.where(from_prev, logits[hr, :WINDOW], logits[hr, WINDOW:]) + table[head]
                sink = sink_ref[head] * LOG2E
                m = jnp.maximum(jnp.max(lg, axis=-1, keepdims=True), sink)
                p = jnp.exp2(lg - m)
                denom = jnp.sum(p, axis=-1, keepdims=True) + jnp.exp2(sink - m)
                p_prev = jnp.where(from_prev, p, 0.0)
                p_both = jnp.concatenate([p_prev, p - p_prev], axis=1).astype(BF16)
                pv = jnp.dot(p_both, vcat, preferred_element_type=F32)
                acc = jnp.where(lane_head == j, pv / denom, acc)
            o_ref[rows, cs] = acc.astype(o_ref.dtype)


def swa_prompt(q, kv_rep, sinks, bias_tabs, bsz, t):
    tr = SWA_BLOCKS * WINDOW
    ns = t // tr
    assert t % tr == 0
    cur = lambda c: pl.BlockSpec((tr, D_MODEL), lambda b, n, c=c: (b * ns + n, c))
    prev = lambda c: pl.BlockSpec(
        (WINDOW, D_MODEL),
        lambda b, n, c=c: (jnp.maximum((b * ns + n) * SWA_BLOCKS - 1, 0), c))
    table = lambda first: pl.BlockSpec(
        (None, B_HEADS, WINDOW, WINDOW),
        (lambda b, n: (jnp.minimum(n, 1), 0, 0, 0)) if first else (lambda b, n: (1, 0, 0, 0)))
    return pl.pallas_call(
        _swa_prompt_body,
        grid=(bsz, ns),
        in_specs=[pl.BlockSpec(memory_space=pltpu.SMEM),
                  cur(0), prev(0), cur(0), prev(1), cur(1), table(True), table(False)],
        out_specs=pl.BlockSpec((tr, D_MODEL), lambda b, n: (b * ns + n, 0)),
        out_shape=jax.ShapeDtypeStruct((bsz * t, D_MODEL), BF16),
        compiler_params=_params(("parallel", "arbitrary")),
        name="swa_prompt",
    )(sinks, q, kv_rep, kv_rep, kv_rep, kv_rep, bias_tabs, bias_tabs)


def _swa_step_body(*refs, shift_cache):
    sink_ref, qz_ref, k_ref, v_ref, bias_ref, fold_ref = refs[:6]
    if shift_cache:
        kn_ref, vn_ref, o_ref, ko_ref, vo_ref = refs[6:]
        last = lax.broadcasted_iota(jnp.int32, (WINDOW, KV_WIDTH), 0) == WINDOW - 1
    else:
        (o_ref,) = refs[6:]
    sink = sink_ref[...] * LOG2E
    head_kv = lax.broadcasted_iota(jnp.int32, (B_HEADS, KV_WIDTH), 0) >> GROUPS_SHIFT
    lane_kv = lax.broadcasted_iota(jnp.int32, (B_HEADS, KV_WIDTH), 1) >> HEAD_DIM_SHIFT
    for b in range(ATTN_TOKENS):
        if shift_cache:
            k = jnp.where(last, kn_ref[b], pltpu.roll(k_ref[b], WINDOW - 1, 0))
            v = jnp.where(last, vn_ref[b], pltpu.roll(v_ref[b], WINDOW - 1, 0))
            ko_ref[b] = k
            vo_ref[b] = v
        else:
            k, v = k_ref[b], v_ref[b]
        qz = qz_ref[b].astype(BF16)
        logits = lax.dot_general(qz, k.astype(BF16), _TN, preferred_element_type=F32)
        logits = logits + bias_ref[...]
        m = jnp.maximum(jnp.max(logits, axis=-1, keepdims=True), sink)
        p = jnp.exp2(logits - m)
        denom = jnp.sum(p, axis=-1, keepdims=True) + jnp.exp2(sink - m)
        pv = jnp.dot(p.astype(BF16), v.astype(BF16), preferred_element_type=F32) / denom
        own = jnp.where(head_kv == lane_kv, pv, 0.0).astype(BF16)
        o_ref[b] = jnp.dot(own, fold_ref[...], preferred_element_type=F32)


def swa_step(qz, k_cache, v_cache, sinks, bias_row, kv_new=None):
    bsz = qz.shape[0]
    nt = ATTN_TOKENS
    assert bsz % nt == 0
    fold = np.zeros((KV_WIDTH, B_HEAD_DIM), np.float32)
    fold[np.arange(KV_WIDTH), np.arange(KV_WIDTH) % B_HEAD_DIM] = 1.0
    cache_blk = pl.BlockSpec((nt, WINDOW, KV_WIDTH), lambda b: (b, 0, 0))
    in_specs = [pl.BlockSpec((B_HEADS, 1), lambda b: (0, 0)),
                pl.BlockSpec((nt, B_HEADS, KV_WIDTH), lambda b: (b, 0, 0)),
                cache_blk, cache_blk,
                pl.BlockSpec((B_HEADS, WINDOW), lambda b: (0, 0)),
                pl.BlockSpec((KV_WIDTH, B_HEAD_DIM), lambda b: (0, 0))]
    args = [sinks.reshape(B_HEADS, 1), qz, k_cache, v_cache, bias_row, jnp.asarray(fold, BF16)]
    out_specs = [pl.BlockSpec((nt, B_HEADS, B_HEAD_DIM), lambda b: (b, 0, 0))]
    out_shape = [jax.ShapeDtypeStruct((bsz, B_HEADS, B_HEAD_DIM), F32)]
    if kv_new is not None:
        row_blk = pl.BlockSpec((nt, 1, KV_WIDTH), lambda b: (b, 0, 0))
        in_specs += [row_blk, row_blk]
        args += list(kv_new)
        out_specs += [cache_blk, cache_blk]
        out_shape += [jax.ShapeDtypeStruct(k_cache.shape, F32)] * 2
    outs = pl.pallas_call(
        functools.partial(_swa_step_body, shift_cache=kv_new is not None),
        grid=(bsz // nt,),
        in_specs=in_specs,
        out_specs=out_specs,
        out_shape=out_shape,
        compiler_params=_params(("parallel",)),
        name="swa_step",
    )(*args)
    return outs if kv_new is not None else outs[0]


def _trunk(x, mods, kv_mod, per_token, hgrn_state0, k_buf, v_buf, wts, lbs, bias_tab, bias_row):
    (w_in_a, w_o_a, gnorm_a, w_q_b, w_o_b, sinks_b, w_ffn_in, w_ffn_out, final_norm_w,
     w_kv) = wts
    prompt = k_buf is None
    bsz, t, _ = x.shape
    m = bsz * t
    common = dict(per_token=per_token, rows_per_batch=t)
    h = x.reshape(m, D_MODEL)
    states = []
    step_inputs = []
    new_state = None
    kv = k_state = v_state = kv_new = None
    for l in range(DEPTH):
        a1, s1, g1, a2, s2, g2 = mods[l]
        if l < N_A_LAYERS:
            qkvg, lf = norm_proj(h, (a1, s1), w_in_a, l, epilogue="hgrn",
                                 lb=lbs[l].reshape(1, D_MODEL), **common)
            gn = gnorm_a[l].reshape(1, A_DV)
            if prompt:
                mix, s_new = hgrn_scan_prompt(qkvg, lf, gn, bsz, t)
                states.append(s_new)
            else:
                qkvg = qkvg.astype(F32)
                last_a = l == N_A_LAYERS - 1
                mix, new_state = hgrn_step(qkvg, lf, gn, hgrn_state0, l,
                                           earlier=list(step_inputs) if last_a else None)
                step_inputs.append((qkvg, lf))
            w_o, lo = w_o_a, l
        else:
            j = l - N_A_LAYERS
            scale = LOG2E / math.sqrt(B_HEAD_DIM)
            if prompt:
                q = norm_proj(h, (a1, s1), w_q_b, j, out_scale=scale, **common)
                mix = swa_prompt(q, kv, sinks_b[j], bias_tab, bsz, t)
            else:
                q = norm_proj(h, (a1, s1), w_q_b, j, out_scale=scale, out_dtype=F32, **common)
                lane_kv = (np.arange(KV_WIDTH) // B_HEAD_DIM)[None, :]
                head_kv = (np.arange(B_HEADS) // B_GROUPS)[:, None]
                qz = jnp.where((lane_kv == head_kv)[None],
                               jnp.tile(q.reshape(m, B_HEADS, B_HEAD_DIM), (1, 1, B_KV_HEADS)), 0.0)
                if j == 0:
                    mix, k_state, v_state = swa_step(qz, k_buf.reshape(m, WINDOW, KV_WIDTH),
                                                     v_buf.reshape(m, WINDOW, KV_WIDTH),
                                                     sinks_b[j], bias_row, kv_new)
                else:
                    mix = swa_step(qz, k_state, v_state, sinks_b[j], bias_row)
                mix = mix.reshape(m, D_MODEL)
            w_o, lo = w_o_b, j
        h = post_ffn(mix, h, g1, (a2, s2), g2, w_o, lo, w_ffn_in, w_ffn_out, l,
                     final_w=final_norm_w if l == DEPTH - 1 else None, **common)
        if l == N_A_LAYERS - 1:
            if prompt:
                kv = norm_proj(h, kv_mod, w_kv, 0, epilogue="kv_rep", **common)
                tail = h.reshape(bsz, t, D_MODEL)[:, -WINDOW:].reshape(bsz * WINDOW, D_MODEL)
                kv_tail = norm_proj(tail, kv_mod, w_kv, 0, per_token=False,
                                    rows_per_batch=WINDOW, out_dtype=F32)
                kv_tail = kv_tail.reshape(bsz, WINDOW, 2, B_KV_HEADS, B_HEAD_DIM)
                k_state, v_state = kv_tail[:, :, 0], kv_tail[:, :, 1]
            else:
                kv_row = norm_proj(h, kv_mod, w_kv, 0, out_dtype=F32, **common)
                kv_new = (kv_row[:, :KV_WIDTH].reshape(m, 1, KV_WIDTH),
                          kv_row[:, KV_WIDTH:].reshape(m, 1, KV_WIDTH))
    y = h.reshape(bsz, t, D_MODEL)
    if prompt:
        return y, jnp.stack(states, axis=1), k_state, v_state
    cache_shape = (m, WINDOW, B_KV_HEADS, B_HEAD_DIM)
    return y, new_state, k_state.reshape(cache_shape), v_state.reshape(cache_shape)


def kernel(x_prompt, x_sample, state_hgrn, cache_swa_k, cache_swa_v, c_prompt, c_sample,
           w_in_a, w_o_a, gnorm_a, lb_a, w_kv, w_ada_kv, b_ada_kv, kv_norm_w, w_q_b, w_o_b,
           sinks_b, rel_bias, norm_w, w_ada, b_ada, w_ffn_in, w_ffn_out, final_norm_w):
    n_p, n_s = c_prompt.shape[0], c_sample.shape[0]
    rows = -(-(n_p + n_s) // SUBLANES) * SUBLANES
    c_all = jnp.concatenate(
        [c_sample, c_prompt, jnp.zeros((rows - n_p - n_s, D_MODEL), F32)], axis=0)

    zeros, ones = jnp.zeros((DEPTH, D_MODEL), F32), jnp.ones((DEPTH, D_MODEL), F32)
    p_ada = jnp.stack([zeros, norm_w[:, 0], zeros, zeros, norm_w[:, 1], zeros], axis=1)
    q_ada = jnp.stack([ones, norm_w[:, 0], ones, ones, norm_w[:, 1], ones], axis=1)
    ada = ada_project(c_all, w_ada, b_ada.reshape(DEPTH, 1, 6 * D_MODEL),
                      p_ada.reshape(DEPTH, 1, 6 * D_MODEL), q_ada.reshape(DEPTH, 1, 6 * D_MODEL))
    p_kv = jnp.concatenate([zeros[0], kv_norm_w]).reshape(1, 1, 2 * D_MODEL)
    q_kv = jnp.concatenate([ones[0], kv_norm_w]).reshape(1, 1, 2 * D_MODEL)
    ada_kv = ada_project(c_all, w_ada_kv.reshape(1, D_MODEL, 2 * D_MODEL),
                         b_ada_kv.reshape(1, 1, 2 * D_MODEL), p_kv, q_kv)

    def mods_for(row0):
        layers = [tuple((ada, l, c, row0) for c in (1, 0, 2, 4, 3, 5)) for l in range(DEPTH)]
        return layers, ((ada_kv, 0, 1, row0), (ada_kv, 0, 0, row0))

    lb_sm = jax.nn.softmax(lb_a.astype(F32), axis=0)
    lbs = jnp.cumsum(lb_sm, axis=0) - lb_sm[0:1]

    rb = rel_bias.astype(F32)[_T5_BUCKETS]
    t_idx, j_idx = np.arange(WINDOW)[:, None], np.arange(WINDOW)[None, :]
    from_prev = j_idx > t_idx
    dist = np.where(from_prev, t_idx + WINDOW - j_idx, t_idx - j_idx)
    onehot = (jnp.asarray(dist)[:, :, None] == jnp.arange(WINDOW)[None, None, :]).astype(F32)
    tab = jnp.einsum("tjd,dh->htj", onehot, rb, precision=lax.Precision.HIGHEST) * LOG2E
    bias_tab = jnp.stack([jnp.where(from_prev, MASK_VALUE, tab),
                          tab])
    bias_row = rb[::-1].T * LOG2E

    bf = lambda w: w.astype(BF16)
    wts = (bf(w_in_a), bf(w_o_a), gnorm_a, bf(w_q_b), bf(w_o_b), sinks_b, bf(w_ffn_in),
           bf(w_ffn_out), final_norm_w, bf(w_kv)[None])
    mods_p, kv_mod_p = mods_for(n_s)
    mods_s, kv_mod_s = mods_for(0)
    y_p, st_p, k_p, v_p = _trunk(x_prompt, mods_p, kv_mod_p, False, None, None, None, wts, lbs,
                                 bias_tab, bias_row)
    y_s, st_s, k_s, v_s = _trunk(x_sample, mods_s, kv_mod_s, True, state_hgrn, cache_swa_k,
                                 cache_swa_v, wts, lbs, bias_tab, bias_row)
    return (y_p, y_s, st_p, st_s, k_p, v_p, k_s, v_s)
```

```python
import functools
import math

import numpy as np
import jax
import jax.numpy as jnp
from jax import lax
from jax.experimental import pallas as pl
from jax.experimental.pallas import tpu as pltpu

F32 = jnp.float32
BF16 = jnp.bfloat16

D_MODEL = 1024
DEPTH = 4
N_A_LAYERS = 2
A_HEADS = 8
A_DK = 128
A_DV = 128
F_MIN = 1e-30
B_HEAD_DIM = 64
B_HEADS = 16
B_KV_HEADS = 4
B_GROUPS = 4
WINDOW = 128
MASK_VALUE = -1e30
N_BUCKETS = 32
MAX_DISTANCE = 128
D_FF = 2816
EPS = 1e-6
GROUP_LANES = B_GROUPS * B_HEAD_DIM
KV_WIDTH = B_KV_HEADS * B_HEAD_DIM
HEAD_DIM_SHIFT = B_HEAD_DIM.bit_length() - 1
GROUPS_SHIFT = B_GROUPS.bit_length() - 1

LOG2E = math.log2(math.e)

SUBLANES = 8
LANES = 128
BF16_SUBLANES = 16

CHUNK = 128
N_LEVELS = 7
MATMUL_LEVELS = (5, 6)
STEP_CHUNKS = 8
ROW_TILE = 1024
COL_TILE = 1024
FF_TILE = 256
ADA_COLS = 2048
SWA_BLOCKS = 8
STEP_TOKENS = 4
ATTN_TOKENS = 32
STEP_ROWS = BF16_SUBLANES
BATCH_ROWS = SUBLANES
VMEM_LIMIT_BYTES = 58 * 1024 * 1024


def _params(sem):
    return pltpu.CompilerParams(dimension_semantics=sem, vmem_limit_bytes=VMEM_LIMIT_BYTES)


def _row_tile(m, per_token, rows_per_batch):
    return min(m, ROW_TILE) if per_token else min(m, ROW_TILE, rows_per_batch)


def _sigmoid(x):
    return 0.5 * jnp.tanh(0.5 * x) + 0.5


def _silu(x):
    h = 0.5 * x
    return h * jnp.tanh(h) + h


def _norm_mod(x, a, s):
    y = x * lax.rsqrt(jnp.mean(x * x, axis=-1, keepdims=True) + EPS)
    return (y * a + s).astype(BF16)


def _resident(shape, layer):
    return pl.BlockSpec((None,) + shape, lambda i: (layer,) + (0,) * len(shape),
                        pipeline_mode=pl.Buffered(1))


def _mod_operand(mod, tm, per_token):
    arr, layer, chunk, row0 = mod
    if per_token:
        return arr, pl.BlockSpec((None, tm, D_MODEL), lambda i: (layer, i, chunk))
    assert row0 % BATCH_ROWS == 0
    return arr, pl.BlockSpec((None, BATCH_ROWS, D_MODEL),
                             lambda i: (layer, row0 // BATCH_ROWS, chunk))


def _row_value(ref, tiles_per_batch):
    if tiles_per_batch is None:
        return ref[...]
    return ref[pl.ds(lax.div(pl.program_id(0), tiles_per_batch), 1), :]


def _tiles_per_batch(m, tm, per_token, rows_per_batch):
    if per_token:
        return None
    assert rows_per_batch % tm == 0 and m // rows_per_batch <= BATCH_ROWS
    return rows_per_batch // tm


def _norm_proj_body(*refs, tiles_per_batch, epilogue, out_scale):
    x_ref, a_ref, s_ref, w_ref = refs[:4]
    xn = _norm_mod(x_ref[...], _row_value(a_ref, tiles_per_batch),
                   _row_value(s_ref, tiles_per_batch))

    tn = min(COL_TILE, w_ref.shape[1])

    def proj(c):
        return jnp.dot(xn, w_ref[:, c:c + tn], preferred_element_type=F32)

    if epilogue == "plain":
        (o_ref,) = refs[4:]
        for c in range(0, o_ref.shape[1], tn):
            acc = proj(c)
            if out_scale is not None:
                acc = acc * out_scale
            o_ref[:, c:c + tn] = acc.astype(o_ref.dtype)
    elif epilogue == "kv_rep":
        (o_ref,) = refs[4:]
        acc = proj(0)
        low_half = lax.broadcasted_iota(jnp.int32, (acc.shape[0], LANES), 1) < B_HEAD_DIM
        for c in range(acc.shape[1] // LANES):
            x = acc[:, c * LANES:(c + 1) * LANES]
            swapped = pltpu.roll(x, B_HEAD_DIM, 1)
            for half, rep in enumerate((jnp.where(low_half, x, swapped),
                                        jnp.where(low_half, swapped, x))):
                rep = rep.astype(o_ref.dtype)
                base = (2 * c + half) * GROUP_LANES
                o_ref[:, base:base + LANES] = rep
                o_ref[:, base + LANES:base + GROUP_LANES] = rep
    elif epilogue == "hgrn":
        lb_ref, o_ref, lf_ref = refs[4:]
        d = D_MODEL
        lb = lb_ref[...]
        o_ref[:, 0:d] = _silu(proj(0)).astype(o_ref.dtype)
        f = lb + (1.0 - lb) * _sigmoid(proj(d))
        o_ref[:, d:2 * d] = (1.0 - f).astype(o_ref.dtype)
        lf_ref[...] = jnp.log2(jnp.maximum(f, F_MIN))
        o_ref[:, 2 * d:3 * d] = proj(2 * d).astype(o_ref.dtype)
        o_ref[:, 3 * d:4 * d] = _silu(proj(3 * d)).astype(o_ref.dtype)
    else:
        raise ValueError(epilogue)


def norm_proj(x, mod, w, layer, *, per_token, rows_per_batch, epilogue="plain", lb=None,
              out_dtype=BF16, out_scale=None):
    m, k = x.shape
    n = w.shape[-1]
    tm = _row_tile(m, per_token, rows_per_batch)
    assert m % tm == 0 and (per_token or rows_per_batch % tm == 0)
    (a_arr, a_spec), (s_arr, s_spec) = [_mod_operand(v, tm, per_token) for v in mod]
    in_specs = [pl.BlockSpec((tm, k), lambda i: (i, 0)), a_spec, s_spec, _resident((k, n), layer)]
    args = [x, a_arr, s_arr, w]
    n_out = n * B_GROUPS if epilogue == "kv_rep" else n
    out_specs = [pl.BlockSpec((tm, n_out), lambda i: (i, 0))]
    out_shape = [jax.ShapeDtypeStruct((m, n_out), out_dtype)]
    if epilogue == "hgrn":
        in_specs.append(pl.BlockSpec((1, D_MODEL), lambda i: (0, 0)))
        args.append(lb)
        out_specs.append(pl.BlockSpec((tm, D_MODEL), lambda i: (i, 0)))
        out_shape.append(jax.ShapeDtypeStruct((m, D_MODEL), F32))
    outs = pl.pallas_call(
        functools.partial(_norm_proj_body, epilogue=epilogue, out_scale=out_scale,
                          tiles_per_batch=_tiles_per_batch(m, tm, per_token, rows_per_batch)),
        grid=(m // tm,),
        in_specs=in_specs,
        out_specs=out_specs,
        out_shape=out_shape,
        compiler_params=_params(("parallel",)),
        name=f"norm_proj_{epilogue}_m{m}_n{n}",
    )(*args)
    return outs if epilogue == "hgrn" else outs[0]


def _post_ffn_body(*refs, tiles_per_batch, final_norm):
    (mix_ref, h_ref, g1_ref, a2_ref, s2_ref, g2_ref, wo_ref, win_ref, wout_ref) = refs[:9]
    rest = refs[9:]
    fw_ref = rest[0] if final_norm else None
    o_ref, act_ref = rest[-2:]
    row = functools.partial(_row_value, tiles_per_batch=tiles_per_batch)
    h_mid = h_ref[...] + row(g1_ref) * jnp.dot(mix_ref[...].astype(BF16), wo_ref[...],
                                               preferred_element_type=F32)
    xn = _norm_mod(h_mid, row(a2_ref), row(s2_ref))
    for c in range(0, D_FF, FF_TILE):
        gate = jnp.dot(xn, win_ref[:, c:c + FF_TILE], preferred_element_type=F32)
        up = jnp.dot(xn, win_ref[:, D_FF + c:D_FF + c + FF_TILE], preferred_element_type=F32)
        act_ref[:, c:c + FF_TILE] = (_silu(gate) * up).astype(BF16)
    out = h_mid + row(g2_ref) * jnp.dot(act_ref[...], wout_ref[...], preferred_element_type=F32)
    if final_norm:
        out = out * lax.rsqrt(jnp.mean(out * out, axis=-1, keepdims=True) + EPS) * fw_ref[...]
    o_ref[...] = out


def post_ffn(mix, h, g1, mod2, g2, w_o, lo, w_ffn_in, w_ffn_out, lf, *, per_token,
             rows_per_batch, final_w=None):
    m, d = h.shape
    tm = _row_tile(m, per_token, rows_per_batch)
    assert m % tm == 0 and (per_token or rows_per_batch % tm == 0)
    rows = [_mod_operand(v, tm, per_token) for v in (g1, mod2[0], mod2[1], g2)]
    tile = pl.BlockSpec((tm, d), lambda i: (i, 0))
    in_specs = [tile, tile] + [spec for _, spec in rows] + [
        _resident((d, d), lo), _resident((d, 2 * D_FF), lf), _resident((D_FF, d), lf)]
    args = [mix, h] + [arr for arr, _ in rows] + [w_o, w_ffn_in, w_ffn_out]
    if final_w is not None:
        in_specs.append(pl.BlockSpec((1, d), lambda i: (0, 0)))
        args.append(final_w.reshape(1, d))
    return pl.pallas_call(
        functools.partial(_post_ffn_body, final_norm=final_w is not None,
                          tiles_per_batch=_tiles_per_batch(m, tm, per_token, rows_per_batch)),
        grid=(m // tm,),
        in_specs=in_specs,
        out_specs=tile,
        out_shape=jax.ShapeDtypeStruct((m, d), F32),
        scratch_shapes=[pltpu.VMEM((tm, D_FF), BF16)],
        compiler_params=_params(("parallel",)),
        name=f"post_ffn_m{m}",
    )(*args)


def _ada_body(c_ref, w_ref, b_ref, p_ref, q_ref, o_ref):
    c = _silu(c_ref[...]).astype(BF16)
    acc = jnp.dot(c, w_ref[...].astype(BF16), preferred_element_type=F32)
    o_ref[...] = p_ref[...] + q_ref[...] * (acc + b_ref[...])


def ada_project(c_all, w, b, p, q):
    n_l, _, n_cols = w.shape
    tn = min(ADA_COLS, n_cols)
    assert n_cols % tn == 0
    rows = c_all.shape[0]
    vec_spec = pl.BlockSpec((None, 1, tn), lambda l, j: (l, 0, j))
    return pl.pallas_call(
        _ada_body,
        grid=(n_l, n_cols // tn),
        in_specs=[
            pl.BlockSpec((rows, D_MODEL), lambda l, j: (0, 0)),
            pl.BlockSpec((None, D_MODEL, tn), lambda l, j: (l, 0, j)),
            vec_spec, vec_spec, vec_spec,
        ],
        out_specs=pl.BlockSpec((None, rows, tn), lambda l, j: (l, 0, j)),
        out_shape=jax.ShapeDtypeStruct((n_l, rows, n_cols), F32),
        compiler_params=_params(("parallel", "parallel")),
        name="ada_project",
    )(c_all, w, b, p, q)


def _scan_tables():
    c = CHUNK
    t = np.arange(c)[:, None]
    u = np.arange(c)[None, :]
    blocks = [(u <= t)]
    for level in MATMUL_LEVELS:
        p = N_LEVELS - level
        odd = ((t >> p) & 1) == 1
        start = (t >> p) << p
        end = (((t >> p) + 1) << p) - 1
        blocks.append(np.where(odd, (u >= start) & (u <= t), (u > t) & (u <= end)))
    sums = np.concatenate(blocks, axis=0).astype(np.float32)
    x = np.arange(c)[:, None] ^ np.arange(c)[None, :]
    msb = np.floor(np.log2(np.maximum(x, 1))).astype(np.int32)
    lvl = np.where(x == 0, 0, N_LEVELS - msb)
    lvl = np.where(np.arange(c)[None, :] > np.arange(c)[:, None], -1, lvl).astype(np.int32)
    return sums, lvl


_SCAN_SUMS, _SCAN_LEVELS = _scan_tables()

_TN = (((1,), (1,)), ((), ()))
_TM = (((0,), (0,)), ((), ()))


def _level_exponent(ex, log2_f, odd_rows, cols, level):
    m = CHUNK >> level
    if level in MATMUL_LEVELS:
        i = 1 + MATMUL_LEVELS.index(level)
        return ex[i * CHUNK:(i + 1) * CHUNK, cols]
    if m == 1:
        return jnp.where(odd_rows, log2_f[:, cols], 0.0)
    parts = []
    for i in range(CHUNK // m):
        blk = ex[i * m:(i + 1) * m, cols]
        if i & 1:
            parts.append(blk - ex[i * m - 1:i * m, cols])
        else:
            parts.append(ex[(i + 1) * m - 1:(i + 1) * m, cols] - blk)
    return jnp.concatenate(parts, axis=0)


def _level_mix(q, kk, odd_rows, level):
    m = CHUNK >> level
    if m >= 8:
        parts = [(q if (i & 1) else kk)[i * m:(i + 1) * m] for i in range(CHUNK // m)]
        return jnp.concatenate(parts, axis=0)
    return jnp.where(odd_rows, q, kk)


def _odd_blocks(x, m):
    return jnp.concatenate([x[i * m:(i + 1) * m] for i in range(1, CHUNK // m, 2)], axis=0)


def _level_product(mix, level):
    m = CHUNK >> level
    if m < 8:
        return lax.dot_general(mix, mix, _TN, preferred_element_type=F32)
    lhs = _odd_blocks(mix, m) if m >= 16 else mix
    a = lax.dot_general(lhs, mix, _TN, preferred_element_type=F32)
    return a if m >= 16 else _odd_blocks(a, m)


def _level_select(att, a, lvl, level):
    m = CHUNK >> level
    if m < 8:
        return jnp.where(lvl == level, a, att)
    tiles = []
    for i in range(CHUNK // m):
        rows = slice(i * m, (i + 1) * m)
        if i & 1:
            a_rows = a[(i // 2) * m:(i // 2 + 1) * m]
            tiles.append(jnp.where(lvl[rows] == level, a_rows, att[rows]))
        else:
            tiles.append(att[rows])
    return jnp.concatenate(tiles, axis=0)


def _hgrn_scan_body(q_ref, k_ref, v_ref, g_ref, lf_ref, gn_ref, sums_ref, lvl_ref,
                    o_ref, st_ref, state_ref, ex_ref):
    n = pl.program_id(1)

    @pl.when(n == 0)
    def _():
        state_ref[...] = jnp.zeros_like(state_ref)

    lvl = lvl_ref[...]
    row = lax.broadcasted_iota(jnp.int32, (CHUNK, A_DK), 0)
    gn = gn_ref[...]
    heads = [slice(h * A_DK, (h + 1) * A_DK) for h in range(A_HEADS)]
    for c in range(STEP_CHUNKS):
        rows = slice(c * CHUNK, (c + 1) * CHUNK)
        ex = ex_ref.at[c]
        log2_f = lf_ref[rows, :]
        hi = log2_f.astype(BF16)
        rem = log2_f - hi.astype(F32)
        mid = rem.astype(BF16)
        lo = (rem - mid.astype(F32)).astype(BF16)
        ex[...] = jnp.dot(sums_ref[...], jnp.concatenate([hi, mid, lo], axis=0),
                          preferred_element_type=F32)
        b_end = ex[CHUNK - 1:CHUNK, :]
        d_end = jnp.exp2(b_end)
        group = range(A_HEADS)
        q_bf = {h: q_ref[rows, heads[h]] for h in group}
        k_bf = {h: k_ref[rows, heads[h]] for h in group}
        q = {h: q_bf[h].astype(F32) for h in group}
        kk = {h: k_bf[h].astype(F32) for h in group}
        att = {h: jnp.where(lvl == 0, lax.dot_general(q_bf[h], k_bf[h], _TN,
                                                      preferred_element_type=F32), 0.0)
               for h in group}
        for level in range(1, N_LEVELS + 1):
            odd_rows = ((row >> (N_LEVELS - level)) & 1) == 1
            for h in group:
                e = jnp.exp2(_level_exponent(ex, log2_f, odd_rows, heads[h], level))
                mix = (_level_mix(q[h], kk[h], odd_rows, level) * e).astype(BF16)
                att[h] = _level_select(att[h], _level_product(mix, level), lvl, level)
        for h in group:
            sl = heads[h]
            b_incl = ex[0:CHUNK, sl]
            q_dec = (q[h] * jnp.exp2(b_incl)).astype(BF16)
            k_dec = (kk[h] * jnp.exp2(b_end[:, sl] - b_incl)).astype(BF16)
            v = v_ref[rows, sl]
            s_t = state_ref[h]
            o = (lax.dot_general(q_dec, s_t.astype(BF16), _TN, preferred_element_type=F32)
                 + jnp.dot(att[h].astype(BF16), v, preferred_element_type=F32))
            state_ref[h] = d_end[:, sl] * s_t + lax.dot_general(
                v, k_dec, _TM, preferred_element_type=F32)
            y = o * lax.rsqrt(jnp.mean(o * o, axis=-1, keepdims=True) + EPS)
            o_ref[rows, sl] = (y * gn * g_ref[rows, sl].astype(F32)).astype(o_ref.dtype)

    @pl.when(n == pl.num_programs(1) - 1)
    def _():
        for h in range(A_HEADS):
            st_ref[0, h] = state_ref[h].T


def hgrn_scan_prompt(qkvg, lf, gn, bsz, t):
    tr = STEP_CHUNKS * CHUNK
    ns = t // tr
    assert t % tr == 0
    sec = lambda k: pl.BlockSpec((tr, D_MODEL), lambda b, n, k=k: (b * ns + n, k))
    n_sum = _SCAN_SUMS.shape[0]
    return pl.pallas_call(
        _hgrn_scan_body,
        grid=(bsz, ns),
        in_specs=[sec(0), sec(1), sec(2), sec(3), sec(0),
                  pl.BlockSpec((1, A_DV), lambda b, n: (0, 0)),
                  pl.BlockSpec((n_sum, 3 * CHUNK), lambda b, n: (0, 0)),
                  pl.BlockSpec((CHUNK, CHUNK), lambda b, n: (0, 0))],
        out_specs=[pl.BlockSpec((tr, D_MODEL), lambda b, n: (b * ns + n, 0)),
                   pl.BlockSpec((1, A_HEADS, A_DK, A_DV), lambda b, n: (b, 0, 0, 0))],
        out_shape=[jax.ShapeDtypeStruct((bsz * t, D_MODEL), BF16),
                   jax.ShapeDtypeStruct((bsz, A_HEADS, A_DK, A_DV), F32)],
        scratch_shapes=[pltpu.VMEM((A_HEADS, A_DV, A_DK), F32),
                        pltpu.VMEM((STEP_CHUNKS, n_sum, D_MODEL), F32)],
        compiler_params=_params(("parallel", "arbitrary")),
        name="hgrn_scan",
    )(qkvg, qkvg, qkvg, qkvg, lf, gn, jnp.asarray(np.tile(_SCAN_SUMS, (1, 3)), BF16),
      jnp.asarray(_SCAN_LEVELS))


def _step_columns(decay, kk, q, v_row):
    r = lax.broadcasted_iota(jnp.int32, (STEP_ROWS, A_DK), 0)
    hi = decay.astype(BF16).astype(F32)
    rem = decay - hi
    mid = rem.astype(BF16).astype(F32)
    lo = rem - mid
    lhs = jnp.where(r == 0, hi, jnp.where(r == 1, mid, jnp.where(r == 2, lo,
                                                                 jnp.where(r == 3, kk, 0.0))))
    parts = [jnp.where(r < 3, 1.0, 0.0), jnp.where(r == 3, v_row, 0.0)]
    if q is not None:
        lhs = jnp.where(r == 4, q, lhs)
        parts.append(jnp.where(r == 4, 1.0, 0.0))
    rhs = jnp.concatenate(parts, axis=1).astype(BF16)
    out = lax.dot_general(lhs.astype(BF16), rhs, _TM, preferred_element_type=F32)
    return [out[:, i * A_DV:(i + 1) * A_DV] for i in range(len(parts))]


def _hgrn_step_body(*refs, n_replay, write_state):
    q_ref, k_ref, v_ref, g_ref, lf_ref, gn_ref, s_ref = refs[:7]
    replay = [refs[7 + 4 * i:11 + 4 * i] for i in range(n_replay)]
    outs = refs[7 + 4 * n_replay:]
    o_ref = outs[0]
    gn = gn_ref[...]
    for b in range(STEP_TOKENS):
        decay = jnp.exp2(lf_ref[b])
        q, kk, v, gate = q_ref[b], k_ref[b], v_ref[b], g_ref[b]
        for h in range(A_HEADS):
            sl = slice(h * A_DK, (h + 1) * A_DK)
            if write_state:
                dec_m, kv_m, q_m = _step_columns(decay[:, sl], kk[:, sl], q[:, sl], v[:, sl])
                s_new = dec_m * s_ref[b, h] + kv_m
                outs[1][b, n_replay, h] = s_new
                o = jnp.sum(q_m * s_new, axis=0, keepdims=True)
            else:
                r = lax.broadcasted_iota(jnp.int32, (STEP_ROWS, A_DK), 0)
                lhs = jnp.where(r == 0, q[:, sl] * decay[:, sl], 0.0).astype(BF16)
                ones = jnp.where(r == 0, 1.0, 0.0).astype(BF16)
                qd_m = lax.dot_general(lhs, ones, _TM, preferred_element_type=F32)
                qk = jnp.sum(q[:, sl] * kk[:, sl], axis=-1, keepdims=True)
                o = jnp.sum(qd_m * s_ref[b, h], axis=0, keepdims=True) + qk * v[:, sl]
            y = o * lax.rsqrt(jnp.mean(o * o, axis=-1, keepdims=True) + EPS)
            o_ref[b, :, sl] = y * gn * gate[:, sl]
        for i, (kp_ref, vp_ref, lfp_ref, sp_ref) in enumerate(replay):
            decay_p = jnp.exp2(lfp_ref[b])
            for h in range(A_HEADS):
                sl = slice(h * A_DK, (h + 1) * A_DK)
                dec_m, kv_m = _step_columns(decay_p[:, sl], kp_ref[b][:, sl], None,
                                            vp_ref[b][:, sl])
                outs[1][b, i, h] = dec_m * sp_ref[b, h] + kv_m


def hgrn_step(qkvg, lf, gn, state, layer, earlier=None):
    bsz = qkvg.shape[0]
    nb = STEP_TOKENS
    assert bsz % nb == 0
    sec = lambda k: pl.BlockSpec((nb, 1, D_MODEL), lambda b, k=k: (b, 0, k))
    state_blk = lambda l: pl.BlockSpec((nb, None, A_HEADS, A_DK, A_DV),
                                       lambda b, l=l: (b, l, 0, 0, 0))
    as3 = lambda a: a.reshape(bsz, 1, a.shape[-1])
    in_specs = [sec(0), sec(1), sec(2), sec(3), sec(0),
                pl.BlockSpec((1, A_DV), lambda b: (0, 0)), state_blk(layer)]
    args = [as3(qkvg)] * 4 + [as3(lf), gn, state]
    write_state = earlier is not None
    for i, (qkvg_p, lf_p) in enumerate(earlier or ()):
        in_specs += [sec(1), sec(2), sec(0), state_blk(i)]
        args += [as3(qkvg_p), as3(qkvg_p), as3(lf_p), state]
    out_specs = [pl.BlockSpec((nb, 1, D_MODEL), lambda b: (b, 0, 0))]
    out_shape = [jax.ShapeDtypeStruct((bsz, 1, D_MODEL), F32)]
    if write_state:
        assert len(earlier) == layer
        n_l = layer + 1
        out_specs.append(pl.BlockSpec((nb, n_l, A_HEADS, A_DK, A_DV), lambda b: (b, 0, 0, 0, 0)))
        out_shape.append(jax.ShapeDtypeStruct((bsz, n_l, A_HEADS, A_DK, A_DV), F32))
    outs = pl.pallas_call(
        functools.partial(_hgrn_step_body, n_replay=len(earlier or ()), write_state=write_state),
        grid=(bsz // nb,),
        in_specs=in_specs,
        out_specs=out_specs,
        out_shape=out_shape,
        compiler_params=_params(("parallel",)),
        name="hgrn_step",
    )(*args)
    o = outs[0].reshape(bsz, D_MODEL)
    return (o, outs[1]) if write_state else (o, None)


def _t5_buckets():
    max_exact = N_BUCKETS // 2
    d = np.arange(WINDOW)
    large = max_exact + (np.log(np.maximum(d, 1).astype(np.float32) / max_exact)
                         / math.log(MAX_DISTANCE / max_exact)
                         * (N_BUCKETS - max_exact)).astype(np.int32)
    large = np.clip(large, 0, N_BUCKETS - 1)
    return np.where(d < max_exact, d, large).astype(np.int32)


_T5_BUCKETS = _t5_buckets()


def _swa_prompt_body(sink_ref, q_ref, kp_ref, kc_ref, vp_ref, vc_ref, bias0_ref, bias_ref, o_ref):
    lane_head = lax.broadcasted_iota(jnp.int32, (WINDOW, GROUP_LANES), 1) >> HEAD_DIM_SHIFT
    from_prev = (lax.broadcasted_iota(jnp.int32, (WINDOW, WINDOW), 1)
                 > lax.broadcasted_iota(jnp.int32, (WINDOW, WINDOW), 0))
    groups = [slice(g * GROUP_LANES, (g + 1) * GROUP_LANES) for g in range(B_KV_HEADS)]
    for blk in range(SWA_BLOCKS):
        rows = slice(blk * WINDOW, (blk + 1) * WINDOW)
        prev_rows = slice((blk - 1) * WINDOW, blk * WINDOW)
        table = bias0_ref if blk == 0 else bias_ref

        def keys(prev_ref, cur_ref, cs):
            prev = prev_ref[:, cs] if blk == 0 else cur_ref[prev_rows, cs]
            return jnp.concatenate([prev, cur_ref[rows, cs]], axis=0)

        all_logits = []
        for cs in groups:
            qg = q_ref[rows, cs].astype(F32)
            qstack = jnp.concatenate(
                [jnp.where(lane_head == j, qg, 0.0).astype(BF16) for j in range(B_GROUPS)], axis=0)
            all_logits.append(lax.dot_general(qstack, keys(kp_ref, kc_ref, cs), _TN,
                                              preferred_element_type=F32))
        for g, cs in enumerate(groups):
            logits = all_logits[g]
            vcat = keys(vp_ref, vc_ref, cs)
            acc = jnp.zeros((WINDOW, GROUP_LANES), F32)
            for j in range(B_GROUPS):
                head = g * B_GROUPS + j
                hr = slice(j * WINDOW, (j + 1) * WINDOW)
                lg = jnp.where(from_prev, logits[hr, :WINDOW], logits[hr, WINDOW:]) + table[head]
                sink = sink_ref[head] * LOG2E
                m = jnp.maximum(jnp.max(lg, axis=-1, keepdims=True), sink)
                p = jnp.exp2(lg - m)
                denom = jnp.sum(p, axis=-1, keepdims=True) + jnp.exp2(sink - m)
                p_prev = jnp.where(from_prev, p, 0.0)
                p_both = jnp.concatenate([p_prev, p - p_prev], axis=1).astype(BF16)
                pv = jnp.dot(p_both, vcat, preferred_element_type=F32)
                acc = jnp.where(lane_head == j, pv / denom, acc)
            o_ref[rows, cs] = acc.astype(o_ref.dtype)


def swa_prompt(q, kv_rep, sinks, bias_tabs, bsz, t):
    tr = SWA_BLOCKS * WINDOW
    ns = t // tr
    assert t % tr == 0
    cur = lambda c: pl.BlockSpec((tr, D_MODEL), lambda b, n, c=c: (b * ns + n, c))
    prev = lambda c: pl.BlockSpec(
        (WINDOW, D_MODEL),
        lambda b, n, c=c: (jnp.maximum((b * ns + n) * SWA_BLOCKS - 1, 0), c))
    table = lambda first: pl.BlockSpec(
        (None, B_HEADS, WINDOW, WINDOW),
        (lambda b, n: (jnp.minimum(n, 1), 0, 0, 0)) if first else (lambda b, n: (1, 0, 0, 0)))
    return pl.pallas_call(
        _swa_prompt_body,
        grid=(bsz, ns),
        in_specs=[pl.BlockSpec(memory_space=pltpu.SMEM),
                  cur(0), prev(0), cur(0), prev(1), cur(1), table(True), table(False)],
        out_specs=pl.BlockSpec((tr, D_MODEL), lambda b, n: (b * ns + n, 0)),
        out_shape=jax.ShapeDtypeStruct((bsz * t, D_MODEL), BF16),
        compiler_params=_params(("parallel", "arbitrary")),
        name="swa_prompt",
    )(sinks, q, kv_rep, kv_rep, kv_rep, kv_rep, bias_tabs, bias_tabs)


def _swa_step_body(*refs, shift_cache):
    sink_ref, qz_ref, k_ref, v_ref, bias_ref, fold_ref = refs[:6]
    if shift_cache:
        kn_ref, vn_ref, o_ref, ko_ref, vo_ref = refs[6:]
        last = lax.broadcasted_iota(jnp.int32, (WINDOW, KV_WIDTH), 0) == WINDOW - 1
    else:
        (o_ref,) = refs[6:]
    sink = sink_ref[...] * LOG2E
    head_kv = lax.broadcasted_iota(jnp.int32, (B_HEADS, KV_WIDTH), 0) >> GROUPS_SHIFT
    lane_kv = lax.broadcasted_iota(jnp.int32, (B_HEADS, KV_WIDTH), 1) >> HEAD_DIM_SHIFT
    for b in range(ATTN_TOKENS):
        if shift_cache:
            k = jnp.where(last, kn_ref[b], pltpu.roll(k_ref[b], WINDOW - 1, 0))
            v = jnp.where(last, vn_ref[b], pltpu.roll(v_ref[b], WINDOW - 1, 0))
            ko_ref[b] = k
            vo_ref[b] = v
        else:
            k, v = k_ref[b], v_ref[b]
        qz = qz_ref[b].astype(BF16)
        logits = lax.dot_general(qz, k.astype(BF16), _TN, preferred_element_type=F32)
        logits = logits + bias_ref[...]
        m = jnp.maximum(jnp.max(logits, axis=-1, keepdims=True), sink)
        p = jnp.exp2(logits - m)
        denom = jnp.sum(p, axis=-1, keepdims=True) + jnp.exp2(sink - m)
        pv = jnp.dot(p.astype(BF16), v.astype(BF16), preferred_element_type=F32) / denom
        own = jnp.where(head_kv == lane_kv, pv, 0.0).astype(BF16)
        o_ref[b] = jnp.dot(own, fold_ref[...], preferred_element_type=F32)


def swa_step(qz, k_cache, v_cache, sinks, bias_row, kv_new=None):
    bsz = qz.shape[0]
    nt = ATTN_TOKENS
    assert bsz % nt == 0
    fold = np.zeros((KV_WIDTH, B_HEAD_DIM), np.float32)
    fold[np.arange(KV_WIDTH), np.arange(KV_WIDTH) % B_HEAD_DIM] = 1.0
    cache_blk = pl.BlockSpec((nt, WINDOW, KV_WIDTH), lambda b: (b, 0, 0))
    in_specs = [pl.BlockSpec((B_HEADS, 1), lambda b: (0, 0)),
                pl.BlockSpec((nt, B_HEADS, KV_WIDTH), lambda b: (b, 0, 0)),
                cache_blk, cache_blk,
                pl.BlockSpec((B_HEADS, WINDOW), lambda b: (0, 0)),
                pl.BlockSpec((KV_WIDTH, B_HEAD_DIM), lambda b: (0, 0))]
    args = [sinks.reshape(B_HEADS, 1), qz, k_cache, v_cache, bias_row, jnp.asarray(fold, BF16)]
    out_specs = [pl.BlockSpec((nt, B_HEADS, B_HEAD_DIM), lambda b: (b, 0, 0))]
    out_shape = [jax.ShapeDtypeStruct((bsz, B_HEADS, B_HEAD_DIM), F32)]
    if kv_new is not None:
        row_blk = pl.BlockSpec((nt, 1, KV_WIDTH), lambda b: (b, 0, 0))
        in_specs += [row_blk, row_blk]
        args += list(kv_new)
        out_specs += [cache_blk, cache_blk]
        out_shape += [jax.ShapeDtypeStruct(k_cache.shape, F32)] * 2
    outs = pl.pallas_call(
        functools.partial(_swa_step_body, shift_cache=kv_new is not None),
        grid=(bsz // nt,),
        in_specs=in_specs,
        out_specs=out_specs,
        out_shape=out_shape,
        compiler_params=_params(("parallel",)),
        name="swa_step",
    )(*args)
    return outs if kv_new is not None else outs[0]


def _trunk(x, mods, kv_mod, per_token, hgrn_state0, k_buf, v_buf, wts, lbs, bias_tab, bias_row):
    (w_in_a, w_o_a, gnorm_a, w_q_b, w_o_b, sinks_b, w_ffn_in, w_ffn_out, final_norm_w,
     w_kv) = wts
    prompt = k_buf is None
    bsz, t, _ = x.shape
    m = bsz * t
    common = dict(per_token=per_token, rows_per_batch=t)
    h = x.reshape(m, D_MODEL)
    states = []
    step_inputs = []
    new_state = None
    kv = k_state = v_state = kv_new = None
    for l in range(DEPTH):
        a1, s1, g1, a2, s2, g2 = mods[l]
        if l < N_A_LAYERS:
            qkvg, lf = norm_proj(h, (a1, s1), w_in_a, l, epilogue="hgrn",
                                 lb=lbs[l].reshape(1, D_MODEL), **common)
            gn = gnorm_a[l].reshape(1, A_DV)
            if prompt:
                mix, s_new = hgrn_scan_prompt(qkvg, lf, gn, bsz, t)
                states.append(s_new)
            else:
                qkvg = qkvg.astype(F32)
                last_a = l == N_A_LAYERS - 1
                mix, new_state = hgrn_step(qkvg, lf, gn, hgrn_state0, l,
                                           earlier=list(step_inputs) if last_a else None)
                step_inputs.append((qkvg, lf))
            w_o, lo = w_o_a, l
        else:
            j = l - N_A_LAYERS
            scale = LOG2E / math.sqrt(B_HEAD_DIM)
            if prompt:
                q = norm_proj(h, (a1, s1), w_q_b, j, out_scale=scale, **common)
                mix = swa_prompt(q, kv, sinks_b[j], bias_tab, bsz, t)
            else:
                q = norm_proj(h, (a1, s1), w_q_b, j, out_scale=scale, out_dtype=F32, **common)
                lane_kv = (np.arange(KV_WIDTH) // B_HEAD_DIM)[None, :]
                head_kv = (np.arange(B_HEADS) // B_GROUPS)[:, None]
                qz = jnp.where((lane_kv == head_kv)[None],
                               jnp.tile(q.reshape(m, B_HEADS, B_HEAD_DIM), (1, 1, B_KV_HEADS)), 0.0)
                if j == 0:
                    mix, k_state, v_state = swa_step(qz, k_buf.reshape(m, WINDOW, KV_WIDTH),
                                                     v_buf.reshape(m, WINDOW, KV_WIDTH),
                                                     sinks_b[j], bias_row, kv_new)
                else:
                    mix = swa_step(qz, k_state, v_state, sinks_b[j], bias_row)
                mix = mix.reshape(m, D_MODEL)
            w_o, lo = w_o_b, j
        h = post_ffn(mix, h, g1, (a2, s2), g2, w_o, lo, w_ffn_in, w_ffn_out, l,
                     final_w=final_norm_w if l == DEPTH - 1 else None, **common)
        if l == N_A_LAYERS - 1:
            if prompt:
                kv = norm_proj(h, kv_mod, w_kv, 0, epilogue="kv_rep", **common)
                tail = h.reshape(bsz, t, D_MODEL)[:, -WINDOW:].reshape(bsz * WINDOW, D_MODEL)
                kv_tail = norm_proj(tail, kv_mod, w_kv, 0, per_token=False,
                                    rows_per_batch=WINDOW, out_dtype=F32)
                kv_tail = kv_tail.reshape(bsz, WINDOW, 2, B_KV_HEADS, B_HEAD_DIM)
                k_state, v_state = kv_tail[:, :, 0], kv_tail[:, :, 1]
            else:
                kv_row = norm_proj(h, kv_mod, w_kv, 0, out_dtype=F32, **common)
                kv_new = (kv_row[:, :KV_WIDTH].reshape(m, 1, KV_WIDTH),
                          kv_row[:, KV_WIDTH:].reshape(m, 1, KV_WIDTH))
    y = h.reshape(bsz, t, D_MODEL)
    if prompt:
        return y, jnp.stack(states, axis=1), k_state, v_state
    cache_shape = (m, WINDOW, B_KV_HEADS, B_HEAD_DIM)
    return y, new_state, k_state.reshape(cache_shape), v_state.reshape(cache_shape)


def kernel(x_prompt, x_sample, state_hgrn, cache_swa_k, cache_swa_v, c_prompt, c_sample,
           w_in_a, w_o_a, gnorm_a, lb_a, w_kv, w_ada_kv, b_ada_kv, kv_norm_w, w_q_b, w_o_b,
           sinks_b, rel_bias, norm_w, w_ada, b_ada, w_ffn_in, w_ffn_out, final_norm_w):
    n_p, n_s = c_prompt.shape[0], c_sample.shape[0]
    rows = -(-(n_p + n_s) // SUBLANES) * SUBLANES
    c_all = jnp.concatenate(
        [c_sample, c_prompt, jnp.zeros((rows - n_p - n_s, D_MODEL), F32)], axis=0)

    zeros, ones = jnp.zeros((DEPTH, D_MODEL), F32), jnp.ones((DEPTH, D_MODEL), F32)
    p_ada = jnp.stack([zeros, norm_w[:, 0], zeros, zeros, norm_w[:, 1], zeros], axis=1)
    q_ada = jnp.stack([ones, norm_w[:, 0], ones, ones, norm_w[:, 1], ones], axis=1)
    ada = ada_project(c_all, w_ada, b_ada.reshape(DEPTH, 1, 6 * D_MODEL),
                      p_ada.reshape(DEPTH, 1, 6 * D_MODEL), q_ada.reshape(DEPTH, 1, 6 * D_MODEL))
    p_kv = jnp.concatenate([zeros[0], kv_norm_w]).reshape(1, 1, 2 * D_MODEL)
    q_kv = jnp.concatenate([ones[0], kv_norm_w]).reshape(1, 1, 2 * D_MODEL)
    ada_kv = ada_project(c_all, w_ada_kv.reshape(1, D_MODEL, 2 * D_MODEL),
                         b_ada_kv.reshape(1, 1, 2 * D_MODEL), p_kv, q_kv)

    def mods_for(row0):
        layers = [tuple((ada, l, c, row0) for c in (1, 0, 2, 4, 3, 5)) for l in range(DEPTH)]
        return layers, ((ada_kv, 0, 1, row0), (ada_kv, 0, 0, row0))

    lb_sm = jax.nn.softmax(lb_a.astype(F32), axis=0)
    lbs = jnp.cumsum(lb_sm, axis=0) - lb_sm[0:1]

    rb = rel_bias.astype(F32)[_T5_BUCKETS]
    t_idx, j_idx = np.arange(WINDOW)[:, None], np.arange(WINDOW)[None, :]
    from_prev = j_idx > t_idx
    dist = np.where(from_prev, t_idx + WINDOW - j_idx, t_idx - j_idx)
    onehot = (jnp.asarray(dist)[:, :, None] == jnp.arange(WINDOW)[None, None, :]).astype(F32)
    tab = jnp.einsum("tjd,dh->htj", onehot, rb, precision=lax.Precision.HIGHEST) * LOG2E
    bias_tab = jnp.stack([jnp.where(from_prev, MASK_VALUE, tab),
                          tab])
    bias_row = rb[::-1].T * LOG2E

    bf = lambda w: w.astype(BF16)
    wts = (bf(w_in_a), bf(w_o_a), gnorm_a, bf(w_q_b), bf(w_o_b), sinks_b, bf(w_ffn_in),
           bf(w_ffn_out), final_norm_w, bf(w_kv)[None])
    mods_p, kv_mod_p = mods_for(n_s)
    mods_s, kv_mod_s = mods_for(0)
    y_p, st_p, k_p, v_p = _trunk(x_prompt, mods_p, kv_mod_p, False, None, None, None, wts, lbs,
                                 bias_tab, bias_row)
    y_s, st_s, k_s, v_s = _trunk(x_sample, mods_s, kv_mod_s, True, state_hgrn, cache_swa_k,
                                 cache_swa_v, wts, lbs, bias_tab, bias_row)
    return (y_p, y_s, st_p, st_s, k_p, v_p, k_s, v_s)
```

```python
import functools
import math

import numpy as np
import jax
import jax.numpy as jnp
from jax import lax
from jax.experimental import pallas as pl
from jax.experimental.pallas import tpu as pltpu

F32 = jnp.float32
BF16 = jnp.bfloat16

D_MODEL = 1024
DEPTH = 4
N_A_LAYERS = 2
A_HEADS = 8
A_DK = 128
A_DV = 128
F_MIN = 1e-30
B_HEAD_DIM = 64
B_HEADS = 16
B_KV_HEADS = 4
B_GROUPS = 4
WINDOW = 128
MASK_VALUE = -1e30
N_BUCKETS = 32
MAX_DISTANCE = 128
D_FF = 2816
EPS = 1e-6
GROUP_LANES = B_GROUPS * B_HEAD_DIM
KV_WIDTH = B_KV_HEADS * B_HEAD_DIM
HEAD_DIM_SHIFT = B_HEAD_DIM.bit_length() - 1
GROUPS_SHIFT = B_GROUPS.bit_length() - 1

LOG2E = math.log2(math.e)

SUBLANES = 8
LANES = 128
BF16_SUBLANES = 16

CHUNK = 128
N_LEVELS = 7
MATMUL_LEVELS = (5, 6)
STEP_CHUNKS = 8
ROW_TILE = 1024
COL_TILE = 1024
FF_TILE = 256
ADA_COLS = 2048
SWA_BLOCKS = 8
STEP_TOKENS = 4
ATTN_TOKENS = 32
STEP_ROWS = BF16_SUBLANES
BATCH_ROWS = SUBLANES
VMEM_LIMIT_BYTES = 58 * 1024 * 1024


def _params(sem):
    return pltpu.CompilerParams(dimension_semantics=sem, vmem_limit_bytes=VMEM_LIMIT_BYTES)


def _row_tile(m, per_token, rows_per_batch):
    return min(m, ROW_TILE) if per_token else min(m, ROW_TILE, rows_per_batch)


def _sigmoid(x):
    return 0.5 * jnp.tanh(0.5 * x) + 0.5


def _silu(x):
    h = 0.5 * x
    return h * jnp.tanh(h) + h


def _norm_mod(x, a, s):
    y = x * lax.rsqrt(jnp.mean(x * x, axis=-1, keepdims=True) + EPS)
    return (y * a + s).astype(BF16)


def _resident(shape, layer):
    return pl.BlockSpec((None,) + shape, lambda i: (layer,) + (0,) * len(shape),
                        pipeline_mode=pl.Buffered(1))


def _mod_operand(mod, tm, per_token):
    arr, layer, chunk, row0 = mod
    if per_token:
        return arr, pl.BlockSpec((None, tm, D_MODEL), lambda i: (layer, i, chunk))
    assert row0 % BATCH_ROWS == 0
    return arr, pl.BlockSpec((None, BATCH_ROWS, D_MODEL),
                             lambda i: (layer, row0 // BATCH_ROWS, chunk))


def _row_value(ref, tiles_per_batch):
    if tiles_per_batch is None:
        return ref[...]
    return ref[pl.ds(lax.div(pl.program_id(0), tiles_per_batch), 1), :]


def _tiles_per_batch(m, tm, per_token, rows_per_batch):
    if per_token:
        return None
    assert rows_per_batch % tm == 0 and m // rows_per_batch <= BATCH_ROWS
    return rows_per_batch // tm


def _store_kv_repeated(acc, o_ref):
    low_half = lax.broadcasted_iota(jnp.int32, (acc.shape[0], LANES), 1) < B_HEAD_DIM
    for c in range(acc.shape[1] // LANES):
        x = acc[:, c * LANES:(c + 1) * LANES]
        swapped = pltpu.roll(x, B_HEAD_DIM, 1)
        for half, rep in enumerate((jnp.where(low_half, x, swapped),
                                    jnp.where(low_half, swapped, x))):
            rep = rep.astype(o_ref.dtype)
            base = (2 * c + half) * GROUP_LANES
            o_ref[:, base:base + LANES] = rep
            o_ref[:, base + LANES:base + GROUP_LANES] = rep


def _norm_qkv_body(x_ref, aq_ref, sq_ref, akv_ref, skv_ref, wq_ref, wkv_ref, q_ref, kv_ref, *,
                   tiles_per_batch, q_scale):
    x = x_ref[...]
    y = x * lax.rsqrt(jnp.mean(x * x, axis=-1, keepdims=True) + EPS)
    row = functools.partial(_row_value, tiles_per_batch=tiles_per_batch)
    xn_kv = (y * row(akv_ref) + row(skv_ref)).astype(BF16)
    _store_kv_repeated(jnp.dot(xn_kv, wkv_ref[...], preferred_element_type=F32), kv_ref)
    xn_q = (y * row(aq_ref) + row(sq_ref)).astype(BF16)
    q_ref[...] = (jnp.dot(xn_q, wq_ref[...], preferred_element_type=F32) * q_scale).astype(q_ref.dtype)


def norm_proj_qkv(x, mod_q, w_q, layer_q, mod_kv, w_kv, *, q_scale, per_token, rows_per_batch):
    m, k = x.shape
    tm = _row_tile(m, per_token, rows_per_batch)
    assert m % tm == 0
    mods = [_mod_operand(v, tm, per_token) for v in (*mod_q, *mod_kv)]
    n_q, n_kv = w_q.shape[-1], w_kv.shape[-1]
    tile = lambda n: pl.BlockSpec((tm, n), lambda i: (i, 0))
    return pl.pallas_call(
        functools.partial(_norm_qkv_body, q_scale=q_scale,
                          tiles_per_batch=_tiles_per_batch(m, tm, per_token, rows_per_batch)),
        grid=(m // tm,),
        in_specs=[tile(k)] + [spec for _, spec in mods] + [
            _resident((k, n_q), layer_q), _resident((k, n_kv), 0)],
        out_specs=[tile(n_q), tile(n_kv * B_GROUPS)],
        out_shape=[jax.ShapeDtypeStruct((m, n_q), BF16),
                   jax.ShapeDtypeStruct((m, n_kv * B_GROUPS), BF16)],
        compiler_params=_params(("parallel",)),
        name=f"norm_proj_qkv_m{m}",
    )(x, *[arr for arr, _ in mods], w_q, w_kv)


def _norm_proj_body(*refs, tiles_per_batch, epilogue, out_scale):
    x_ref, a_ref, s_ref, w_ref = refs[:4]
    xn = _norm_mod(x_ref[...], _row_value(a_ref, tiles_per_batch),
                   _row_value(s_ref, tiles_per_batch))

    tn = min(COL_TILE, w_ref.shape[1])

    def proj(c):
        return jnp.dot(xn, w_ref[:, c:c + tn], preferred_element_type=F32)

    if epilogue == "plain":
        (o_ref,) = refs[4:]
        for c in range(0, o_ref.shape[1], tn):
            acc = proj(c)
            if out_scale is not None:
                acc = acc * out_scale
            o_ref[:, c:c + tn] = acc.astype(o_ref.dtype)
    elif epilogue == "kv_rep":
        (o_ref,) = refs[4:]
        _store_kv_repeated(proj(0), o_ref)
    elif epilogue == "hgrn":
        lb_ref, o_ref, lf_ref = refs[4:]
        d = D_MODEL
        lb = lb_ref[...]
        o_ref[:, 0:d] = _silu(proj(0)).astype(o_ref.dtype)
        f = lb + (1.0 - lb) * _sigmoid(proj(d))
        o_ref[:, d:2 * d] = (1.0 - f).astype(o_ref.dtype)
        lf_ref[...] = jnp.log2(jnp.maximum(f, F_MIN))
        o_ref[:, 2 * d:3 * d] = proj(2 * d).astype(o_ref.dtype)
        o_ref[:, 3 * d:4 * d] = _silu(proj(3 * d)).astype(o_ref.dtype)
    else:
        raise ValueError(epilogue)


def norm_proj(x, mod, w, layer, *, per_token, rows_per_batch, epilogue="plain", lb=None,
              out_dtype=BF16, out_scale=None):
    m, k = x.shape
    n = w.shape[-1]
    tm = _row_tile(m, per_token, rows_per_batch)
    assert m % tm == 0 and (per_token or rows_per_batch % tm == 0)
    (a_arr, a_spec), (s_arr, s_spec) = [_mod_operand(v, tm, per_token) for v in mod]
    in_specs = [pl.BlockSpec((tm, k), lambda i: (i, 0)), a_spec, s_spec, _resident((k, n), layer)]
    args = [x, a_arr, s_arr, w]
    n_out = n * B_GROUPS if epilogue == "kv_rep" else n
    out_specs = [pl.BlockSpec((tm, n_out), lambda i: (i, 0))]
    out_shape = [jax.ShapeDtypeStruct((m, n_out), out_dtype)]
    if epilogue == "hgrn":
        in_specs.append(pl.BlockSpec((1, D_MODEL), lambda i: (0, 0)))
        args.append(lb)
        out_specs.append(pl.BlockSpec((tm, D_MODEL), lambda i: (i, 0)))
        out_shape.append(jax.ShapeDtypeStruct((m, D_MODEL), F32))
    outs = pl.pallas_call(
        functools.partial(_norm_proj_body, epilogue=epilogue, out_scale=out_scale,
                          tiles_per_batch=_tiles_per_batch(m, tm, per_token, rows_per_batch)),
        grid=(m // tm,),
        in_specs=in_specs,
        out_specs=out_specs,
        out_shape=out_shape,
        compiler_params=_params(("parallel",)),
        name=f"norm_proj_{epilogue}_m{m}_n{n}",
    )(*args)
    return outs if epilogue == "hgrn" else outs[0]


def _post_ffn_body(*refs, tiles_per_batch, final_norm):
    (mix_ref, h_ref, g1_ref, a2_ref, s2_ref, g2_ref, wo_ref, win_ref, wout_ref) = refs[:9]
    rest = refs[9:]
    fw_ref = rest[0] if final_norm else None
    o_ref, act_ref = rest[-2:]
    row = functools.partial(_row_value, tiles_per_batch=tiles_per_batch)
    h_mid = h_ref[...] + row(g1_ref) * jnp.dot(mix_ref[...].astype(BF16), wo_ref[...],
                                               preferred_element_type=F32)
    xn = _norm_mod(h_mid, row(a2_ref), row(s2_ref))
    for c in range(0, D_FF, FF_TILE):
        gate = jnp.dot(xn, win_ref[:, c:c + FF_TILE], preferred_element_type=F32)
        up = jnp.dot(xn, win_ref[:, D_FF + c:D_FF + c + FF_TILE], preferred_element_type=F32)
        act_ref[:, c:c + FF_TILE] = (_silu(gate) * up).astype(BF16)
    out = h_mid + row(g2_ref) * jnp.dot(act_ref[...], wout_ref[...], preferred_element_type=F32)
    if final_norm:
        out = out * lax.rsqrt(jnp.mean(out * out, axis=-1, keepdims=True) + EPS) * fw_ref[...]
    o_ref[...] = out


def post_ffn(mix, h, g1, mod2, g2, w_o, lo, w_ffn_in, w_ffn_out, lf, *, per_token,
             rows_per_batch, final_w=None):
    m, d = h.shape
    tm = _row_tile(m, per_token, rows_per_batch)
    assert m % tm == 0 and (per_token or rows_per_batch % tm == 0)
    rows = [_mod_operand(v, tm, per_token) for v in (g1, mod2[0], mod2[1], g2)]
    tile = pl.BlockSpec((tm, d), lambda i: (i, 0))
    in_specs = [tile, tile] + [spec for _, spec in rows] + [
        _resident((d, d), lo), _resident((d, 2 * D_FF), lf), _resident((D_FF, d), lf)]
    args = [mix, h] + [arr for arr, _ in rows] + [w_o, w_ffn_in, w_ffn_out]
    if final_w is not None:
        in_specs.append(pl.BlockSpec((1, d), lambda i: (0, 0)))
        args.append(final_w.reshape(1, d))
    return pl.pallas_call(
        functools.partial(_post_ffn_body, final_norm=final_w is not None,
                          tiles_per_batch=_tiles_per_batch(m, tm, per_token, rows_per_batch)),
        grid=(m // tm,),
        in_specs=in_specs,
        out_specs=tile,
        out_shape=jax.ShapeDtypeStruct((m, d), F32),
        scratch_shapes=[pltpu.VMEM((tm, D_FF), BF16)],
        compiler_params=_params(("parallel",)),
        name=f"post_ffn_m{m}",
    )(*args)


def _ada_body(c_ref, w_ref, b_ref, p_ref, q_ref, o_ref):
    c = _silu(c_ref[...]).astype(BF16)
    acc = jnp.dot(c, w_ref[...].astype(BF16), preferred_element_type=F32)
    o_ref[...] = p_ref[...] + q_ref[...] * (acc + b_ref[...])


def ada_project(c_all, w, b, p, q):
    n_l, _, n_cols = w.shape
    tn = min(ADA_COLS, n_cols)
    assert n_cols % tn == 0
    rows = c_all.shape[0]
    vec_spec = pl.BlockSpec((None, 1, tn), lambda l, j: (l, 0, j))
    return pl.pallas_call(
        _ada_body,
        grid=(n_l, n_cols // tn),
        in_specs=[
            pl.BlockSpec((rows, D_MODEL), lambda l, j: (0, 0)),
            pl.BlockSpec((None, D_MODEL, tn), lambda l, j: (l, 0, j)),
            vec_spec, vec_spec, vec_spec,
        ],
        out_specs=pl.BlockSpec((None, rows, tn), lambda l, j: (l, 0, j)),
        out_shape=jax.ShapeDtypeStruct((n_l, rows, n_cols), F32),
        compiler_params=_params(("parallel", "parallel")),
        name="ada_project",
    )(c_all, w, b, p, q)


def _scan_tables():
    c = CHUNK
    t = np.arange(c)[:, None]
    u = np.arange(c)[None, :]
    blocks = [(u <= t)]
    for level in MATMUL_LEVELS:
        p = N_LEVELS - level
        odd = ((t >> p) & 1) == 1
        start = (t >> p) << p
        end = (((t >> p) + 1) << p) - 1
        blocks.append(np.where(odd, (u >= start) & (u <= t), (u > t) & (u <= end)))
    sums = np.concatenate(blocks, axis=0).astype(np.float32)
    x = np.arange(c)[:, None] ^ np.arange(c)[None, :]
    msb = np.floor(np.log2(np.maximum(x, 1))).astype(np.int32)
    lvl = np.where(x == 0, 0, N_LEVELS - msb)
    lvl = np.where(np.arange(c)[None, :] > np.arange(c)[:, None], -1, lvl).astype(np.int32)
    return sums, lvl


_SCAN_SUMS, _SCAN_LEVELS = _scan_tables()

_TN = (((1,), (1,)), ((), ()))
_TM = (((0,), (0,)), ((), ()))


def _level_exponent(ex, log2_f, odd_rows, cols, level):
    m = CHUNK >> level
    if level in MATMUL_LEVELS:
        i = 1 + MATMUL_LEVELS.index(level)
        return ex[i * CHUNK:(i + 1) * CHUNK, cols]
    if m == 1:
        return jnp.where(odd_rows, log2_f[:, cols], 0.0)
    parts = []
    for i in range(CHUNK // m):
        blk = ex[i * m:(i + 1) * m, cols]
        if i & 1:
            parts.append(blk - ex[i * m - 1:i * m, cols])
        else:
            parts.append(ex[(i + 1) * m - 1:(i + 1) * m, cols] - blk)
    return jnp.concatenate(parts, axis=0)


def _level_mix(q, kk, odd_rows, level):
    m = CHUNK >> level
    if m >= 8:
        parts = [(q if (i & 1) else kk)[i * m:(i + 1) * m] for i in range(CHUNK // m)]
        return jnp.concatenate(parts, axis=0)
    return jnp.where(odd_rows, q, kk)


def _odd_blocks(x, m):
    return jnp.concatenate([x[i * m:(i + 1) * m] for i in range(1, CHUNK // m, 2)], axis=0)


def _level_product(mix, level):
    m = CHUNK >> level
    if m < 8:
        return lax.dot_general(mix, mix, _TN, preferred_element_type=F32)
    lhs = _odd_blocks(mix, m) if m >= 16 else mix
    a = lax.dot_general(lhs, mix, _TN, preferred_element_type=F32)
    return a if m >= 16 else _odd_blocks(a, m)


def _level_select(att, a, lvl, level):
    m = CHUNK >> level
    if m < 8:
        return jnp.where(lvl == level, a, att)
    tiles = []
    for i in range(CHUNK // m):
        rows = slice(i * m, (i + 1) * m)
        if i & 1:
            a_rows = a[(i // 2) * m:(i // 2 + 1) * m]
            tiles.append(jnp.where(lvl[rows] == level, a_rows, att[rows]))
        else:
            tiles.append(att[rows])
    return jnp.concatenate(tiles, axis=0)


def _hgrn_scan_body(q_ref, k_ref, v_ref, g_ref, lf_ref, gn_ref, sums_ref, lvl_ref,
                    o_ref, st_ref, state_ref, ex_ref):
    n = pl.program_id(1)

    @pl.when(n == 0)
    def _():
        state_ref[...] = jnp.zeros_like(state_ref)

    lvl = lvl_ref[...]
    row = lax.broadcasted_iota(jnp.int32, (CHUNK, A_DK), 0)
    gn = gn_ref[...]
    heads = [slice(h * A_DK, (h + 1) * A_DK) for h in range(A_HEADS)]
    for c in range(STEP_CHUNKS):
        rows = slice(c * CHUNK, (c + 1) * CHUNK)
        ex = ex_ref.at[c]
        log2_f = lf_ref[rows, :]
        hi = log2_f.astype(BF16)
        rem = log2_f - hi.astype(F32)
        mid = rem.astype(BF16)
        lo = (rem - mid.astype(F32)).astype(BF16)
        ex[...] = jnp.dot(sums_ref[...], jnp.concatenate([hi, mid, lo], axis=0),
                          preferred_element_type=F32)
        b_end = ex[CHUNK - 1:CHUNK, :]
        d_end = jnp.exp2(b_end)
        group = range(A_HEADS)
        q_bf = {h: q_ref[rows, heads[h]] for h in group}
        k_bf = {h: k_ref[rows, heads[h]] for h in group}
        q = {h: q_bf[h].astype(F32) for h in group}
        kk = {h: k_bf[h].astype(F32) for h in group}
        att = {h: jnp.where(lvl == 0, lax.dot_general(q_bf[h], k_bf[h], _TN,
                                                      preferred_element_type=F32), 0.0)
               for h in group}
        for level in range(1, N_LEVELS + 1):
            odd_rows = ((row >> (N_LEVELS - level)) & 1) == 1
            for h in group:
                e = jnp.exp2(_level_exponent(ex, log2_f, odd_rows, heads[h], level))
                mix = (_level_mix(q[h], kk[h], odd_rows, level) * e).astype(BF16)
                att[h] = _level_select(att[h], _level_product(mix, level), lvl, level)
        for h in group:
            sl = heads[h]
            b_incl = ex[0:CHUNK, sl]
            q_dec = (q[h] * jnp.exp2(b_incl)).astype(BF16)
            k_dec = (kk[h] * jnp.exp2(b_end[:, sl] - b_incl)).astype(BF16)
            v = v_ref[rows, sl]
            s_t = state_ref[h]
            o = (lax.dot_general(q_dec, s_t.astype(BF16), _TN, preferred_element_type=F32)
                 + jnp.dot(att[h].astype(BF16), v, preferred_element_type=F32))
            state_ref[h] = d_end[:, sl] * s_t + lax.dot_general(
                v, k_dec, _TM, preferred_element_type=F32)
            y = o * lax.rsqrt(jnp.mean(o * o, axis=-1, keepdims=True) + EPS)
            o_ref[rows, sl] = (y * gn * g_ref[rows, sl].astype(F32)).astype(o_ref.dtype)

    @pl.when(n == pl.num_programs(1) - 1)
    def _():
        for h in range(A_HEADS):
            st_ref[0, h] = state_ref[h].T


def hgrn_scan_prompt(qkvg, lf, gn, bsz, t):
    tr = STEP_CHUNKS * CHUNK
    ns = t // tr
    assert t % tr == 0
    sec = lambda k: pl.BlockSpec((tr, D_MODEL), lambda b, n, k=k: (b * ns + n, k))
    n_sum = _SCAN_SUMS.shape[0]
    return pl.pallas_call(
        _hgrn_scan_body,
        grid=(bsz, ns),
        in_specs=[sec(0), sec(1), sec(2), sec(3), sec(0),
                  pl.BlockSpec((1, A_DV), lambda b, n: (0, 0)),
                  pl.BlockSpec((n_sum, 3 * CHUNK), lambda b, n: (0, 0)),
                  pl.BlockSpec((CHUNK, CHUNK), lambda b, n: (0, 0))],
        out_specs=[pl.BlockSpec((tr, D_MODEL), lambda b, n: (b * ns + n, 0)),
                   pl.BlockSpec((1, A_HEADS, A_DK, A_DV), lambda b, n: (b, 0, 0, 0))],
        out_shape=[jax.ShapeDtypeStruct((bsz * t, D_MODEL), BF16),
                   jax.ShapeDtypeStruct((bsz, A_HEADS, A_DK, A_DV), F32)],
        scratch_shapes=[pltpu.VMEM((A_HEADS, A_DV, A_DK), F32),
                        pltpu.VMEM((STEP_CHUNKS, n_sum, D_MODEL), F32)],
        compiler_params=_params(("parallel", "arbitrary")),
        name="hgrn_scan",
    )(qkvg, qkvg, qkvg, qkvg, lf, gn, jnp.asarray(np.tile(_SCAN_SUMS, (1, 3)), BF16),
      jnp.asarray(_SCAN_LEVELS))


def _step_columns(decay, kk, q, v_row):
    r = lax.broadcasted_iota(jnp.int32, (STEP_ROWS, A_DK), 0)
    hi = decay.astype(BF16).astype(F32)
    rem = decay - hi
    mid = rem.astype(BF16).astype(F32)
    lo = rem - mid
    lhs = jnp.where(r == 0, hi, jnp.where(r == 1, mid, jnp.where(r == 2, lo,
                                                                 jnp.where(r == 3, kk, 0.0))))
    parts = [jnp.where(r < 3, 1.0, 0.0), jnp.where(r == 3, v_row, 0.0)]
    if q is not None:
        lhs = jnp.where(r == 4, q, lhs)
        parts.append(jnp.where(r == 4, 1.0, 0.0))
    rhs = jnp.concatenate(parts, axis=1).astype(BF16)
    out = lax.dot_general(lhs.astype(BF16), rhs, _TM, preferred_element_type=F32)
    return [out[:, i * A_DV:(i + 1) * A_DV] for i in range(len(parts))]


def _hgrn_step_body(*refs, n_replay, write_state):
    q_ref, k_ref, v_ref, g_ref, lf_ref, gn_ref, s_ref = refs[:7]
    replay = [refs[7 + 4 * i:11 + 4 * i] for i in range(n_replay)]
    outs = refs[7 + 4 * n_replay:]
    o_ref = outs[0]
    gn = gn_ref[...]
    for b in range(STEP_TOKENS):
        decay = jnp.exp2(lf_ref[b])
        q, kk, v, gate = q_ref[b], k_ref[b], v_ref[b], g_ref[b]
        for h in range(A_HEADS):
            sl = slice(h * A_DK, (h + 1) * A_DK)
            if write_state:
                dec_m, kv_m, q_m = _step_columns(decay[:, sl], kk[:, sl], q[:, sl], v[:, sl])
                s_new = dec_m * s_ref[b, h] + kv_m
                outs[1][b, n_replay, h] = s_new
                o = jnp.sum(q_m * s_new, axis=0, keepdims=True)
            else:
                r = lax.broadcasted_iota(jnp.int32, (STEP_ROWS, A_DK), 0)
                lhs = jnp.where(r == 0, q[:, sl] * decay[:, sl], 0.0).astype(BF16)
                ones = jnp.where(r == 0, 1.0, 0.0).astype(BF16)
                qd_m = lax.dot_general(lhs, ones, _TM, preferred_element_type=F32)
                qk = jnp.sum(q[:, sl] * kk[:, sl], axis=-1, keepdims=True)
                o = jnp.sum(qd_m * s_ref[b, h], axis=0, keepdims=True) + qk * v[:, sl]
            y = o * lax.rsqrt(jnp.mean(o * o, axis=-1, keepdims=True) + EPS)
            o_ref[b, :, sl] = y * gn * gate[:, sl]
        for i, (kp_ref, vp_ref, lfp_ref, sp_ref) in enumerate(replay):
            decay_p = jnp.exp2(lfp_ref[b])
            for h in range(A_HEADS):
                sl = slice(h * A_DK, (h + 1) * A_DK)
                dec_m, kv_m = _step_columns(decay_p[:, sl], kp_ref[b][:, sl], None,
                                            vp_ref[b][:, sl])
                outs[1][b, i, h] = dec_m * sp_ref[b, h] + kv_m


def hgrn_step(qkvg, lf, gn, state, layer, earlier=None):
    bsz = qkvg.shape[0]
    nb = STEP_TOKENS
    assert bsz % nb == 0
    sec = lambda k: pl.BlockSpec((nb, 1, D_MODEL), lambda b, k=k: (b, 0, k))
    state_blk = lambda l: pl.BlockSpec((nb, None, A_HEADS, A_DK, A_DV),
                                       lambda b, l=l: (b, l, 0, 0, 0))
    as3 = lambda a: a.reshape(bsz, 1, a.shape[-1])
    in_specs = [sec(0), sec(1), sec(2), sec(3), sec(0),
                pl.BlockSpec((1, A_DV), lambda b: (0, 0)), state_blk(layer)]
    args = [as3(qkvg)] * 4 + [as3(lf), gn, state]
    write_state = earlier is not None
    for i, (qkvg_p, lf_p) in enumerate(earlier or ()):
        in_specs += [sec(1), sec(2), sec(0), state_blk(i)]
        args += [as3(qkvg_p), as3(qkvg_p), as3(lf_p), state]
    out_specs = [pl.BlockSpec((nb, 1, D_MODEL), lambda b: (b, 0, 0))]
    out_shape = [jax.ShapeDtypeStruct((bsz, 1, D_MODEL), F32)]
    if write_state:
        assert len(earlier) == layer
        n_l = layer + 1
        out_specs.append(pl.BlockSpec((nb, n_l, A_HEADS, A_DK, A_DV), lambda b: (b, 0, 0, 0, 0)))
        out_shape.append(jax.ShapeDtypeStruct((bsz, n_l, A_HEADS, A_DK, A_DV), F32))
    outs = pl.pallas_call(
        functools.partial(_hgrn_step_body, n_replay=len(earlier or ()), write_state=write_state),
        grid=(bsz // nb,),
        in_specs=in_specs,
        out_specs=out_specs,
        out_shape=out_shape,
        compiler_params=_params(("parallel",)),
        name="hgrn_step",
    )(*args)
    o = outs[0].reshape(bsz, D_MODEL)
    return (o, outs[1]) if write_state else (o, None)


def _t5_buckets():
    max_exact = N_BUCKETS // 2
    d = np.arange(WINDOW)
    large = max_exact + (np.log(np.maximum(d, 1).astype(np.float32) / max_exact)
                         / math.log(MAX_DISTANCE / max_exact)
                         * (N_BUCKETS - max_exact)).astype(np.int32)
    large = np.clip(large, 0, N_BUCKETS - 1)
    return np.where(d < max_exact, d, large).astype(np.int32)


_T5_BUCKETS = _t5_buckets()


def _swa_prompt_body(sink_ref, q_ref, kp_ref, kc_ref, vp_ref, vc_ref, bias0_ref, bias_ref, o_ref):
    lane_head = lax.broadcasted_iota(jnp.int32, (WINDOW, GROUP_LANES), 1) >> HEAD_DIM_SHIFT
    from_prev = (lax.broadcasted_iota(jnp.int32, (WINDOW, WINDOW), 1)
                 > lax.broadcasted_iota(jnp.int32, (WINDOW, WINDOW), 0))
    groups = [slice(g * GROUP_LANES, (g + 1) * GROUP_LANES) for g in range(B_KV_HEADS)]
    for blk in range(SWA_BLOCKS):
        rows = slice(blk * WINDOW, (blk + 1) * WINDOW)
        prev_rows = slice((blk - 1) * WINDOW, blk * WINDOW)
        table = bias0_ref if blk == 0 else bias_ref

        def keys(prev_ref, cur_ref, cs):
            prev = prev_ref[:, cs] if blk == 0 else cur_ref[prev_rows, cs]
            return jnp.concatenate([prev, cur_ref[rows, cs]], axis=0)

        all_logits = []
        for cs in groups:
            qg = q_ref[rows, cs].astype(F32)
            qstack = jnp.concatenate(
                [jnp.where(lane_head == j, qg, 0.0).astype(BF16) for j in range(B_GROUPS)], axis=0)
            all_logits.append(lax.dot_general(qstack, keys(kp_ref, kc_ref, cs), _TN,
                                              preferred_element_type=F32))
        for g, cs in enumerate(groups):
            logits = all_logits[g]
            vcat = keys(vp_ref, vc_ref, cs)
            acc = jnp.zeros((WINDOW, GROUP_LANES), F32)
            for j in range(B_GROUPS):
                head = g * B_GROUPS + j
                hr = slice(j * WINDOW, (j + 1) * WINDOW)
                lg = jnp.where(from_prev, logits[hr, :WINDOW], logits[hr, WINDOW:]) + table[head]
                sink = sink_ref[head] * LOG2E
                m = jnp.maximum(jnp.max(lg, axis=-1, keepdims=True), sink)
                p = jnp.exp2(lg - m)
                denom = jnp.sum(p, axis=-1, keepdims=True) + jnp.exp2(sink - m)
                p_prev = jnp.where(from_prev, p, 0.0)
                p_both = jnp.concatenate([p_prev, p - p_prev], axis=1).astype(BF16)
                pv = jnp.dot(p_both, vcat, preferred_element_type=F32)
                acc = jnp.where(lane_head == j, pv / denom, acc)
            o_ref[rows, cs] = acc.astype(o_ref.dtype)


def swa_prompt(q, kv_rep, sinks, bias_tabs, bsz, t):
    tr = SWA_BLOCKS * WINDOW
    ns = t // tr
    assert t % tr == 0
    cur = lambda c: pl.BlockSpec((tr, D_MODEL), lambda b, n, c=c: (b * ns + n, c))
    prev = lambda c: pl.BlockSpec(
        (WINDOW, D_MODEL),
        lambda b, n, c=c: (jnp.maximum((b * ns + n) * SWA_BLOCKS - 1, 0), c))
    table = lambda first: pl.BlockSpec(
        (None, B_HEADS, WINDOW, WINDOW),
        (lambda b, n: (jnp.minimum(n, 1), 0, 0, 0)) if first else (lambda b, n: (1, 0, 0, 0)))
    return pl.pallas_call(
        _swa_prompt_body,
        grid=(bsz, ns),
        in_specs=[pl.BlockSpec(memory_space=pltpu.SMEM),
                  cur(0), prev(0), cur(0), prev(1), cur(1), table(True), table(False)],
        out_specs=pl.BlockSpec((tr, D_MODEL), lambda b, n: (b * ns + n, 0)),
        out_shape=jax.ShapeDtypeStruct((bsz * t, D_MODEL), BF16),
        compiler_params=_params(("parallel", "arbitrary")),
        name="swa_prompt",
    )(sinks, q, kv_rep, kv_rep, kv_rep, kv_rep, bias_tabs, bias_tabs)


def _swa_step_body(*refs, shift_cache):
    sink_ref, qz_ref, k_ref, v_ref, bias_ref, fold_ref = refs[:6]
    if shift_cache:
        kn_ref, vn_ref, o_ref, ko_ref, vo_ref = refs[6:]
        last = lax.broadcasted_iota(jnp.int32, (WINDOW, KV_WIDTH), 0) == WINDOW - 1
    else:
        (o_ref,) = refs[6:]
    sink = sink_ref[...] * LOG2E
    head_kv = lax.broadcasted_iota(jnp.int32, (B_HEADS, KV_WIDTH), 0) >> GROUPS_SHIFT
    lane_kv = lax.broadcasted_iota(jnp.int32, (B_HEADS, KV_WIDTH), 1) >> HEAD_DIM_SHIFT
    for b in range(ATTN_TOKENS):
        if shift_cache:
            k = jnp.where(last, kn_ref[b], pltpu.roll(k_ref[b], WINDOW - 1, 0))
            v = jnp.where(last, vn_ref[b], pltpu.roll(v_ref[b], WINDOW - 1, 0))
            ko_ref[b] = k
            vo_ref[b] = v
        else:
            k, v = k_ref[b], v_ref[b]
        qz = qz_ref[b].astype(BF16)
        logits = lax.dot_general(qz, k.astype(BF16), _TN, preferred_element_type=F32)
        logits = logits + bias_ref[...]
        m = jnp.maximum(jnp.max(logits, axis=-1, keepdims=True), sink)
        p = jnp.exp2(logits - m)
        denom = jnp.sum(p, axis=-1, keepdims=True) + jnp.exp2(sink - m)
        pv = jnp.dot(p.astype(BF16), v.astype(BF16), preferred_element_type=F32) / denom
        own = jnp.where(head_kv == lane_kv, pv, 0.0).astype(BF16)
        o_ref[b] = jnp.dot(own, fold_ref[...], preferred_element_type=F32)


def swa_step(qz, k_cache, v_cache, sinks, bias_row, kv_new=None):
    bsz = qz.shape[0]
    nt = ATTN_TOKENS
    assert bsz % nt == 0
    fold = np.zeros((KV_WIDTH, B_HEAD_DIM), np.float32)
    fold[np.arange(KV_WIDTH), np.arange(KV_WIDTH) % B_HEAD_DIM] = 1.0
    cache_blk = pl.BlockSpec((nt, WINDOW, KV_WIDTH), lambda b: (b, 0, 0))
    in_specs = [pl.BlockSpec((B_HEADS, 1), lambda b: (0, 0)),
                pl.BlockSpec((nt, B_HEADS, KV_WIDTH), lambda b: (b, 0, 0)),
                cache_blk, cache_blk,
                pl.BlockSpec((B_HEADS, WINDOW), lambda b: (0, 0)),
                pl.BlockSpec((KV_WIDTH, B_HEAD_DIM), lambda b: (0, 0))]
    args = [sinks.reshape(B_HEADS, 1), qz, k_cache, v_cache, bias_row, jnp.asarray(fold, BF16)]
    out_specs = [pl.BlockSpec((nt, B_HEADS, B_HEAD_DIM), lambda b: (b, 0, 0))]
    out_shape = [jax.ShapeDtypeStruct((bsz, B_HEADS, B_HEAD_DIM), F32)]
    if kv_new is not None:
        row_blk = pl.BlockSpec((nt, 1, KV_WIDTH), lambda b: (b, 0, 0))
        in_specs += [row_blk, row_blk]
        args += list(kv_new)
        out_specs += [cache_blk, cache_blk]
        out_shape += [jax.ShapeDtypeStruct(k_cache.shape, F32)] * 2
    outs = pl.pallas_call(
        functools.partial(_swa_step_body, shift_cache=kv_new is not None),
        grid=(bsz // nt,),
        in_specs=in_specs,
        out_specs=out_specs,
        out_shape=out_shape,
        compiler_params=_params(("parallel",)),
        name="swa_step",
    )(*args)
    return outs if kv_new is not None else outs[0]


def _trunk(x, mods, kv_mod, per_token, hgrn_state0, k_buf, v_buf, wts, lbs, bias_tab, bias_row):
    (w_in_a, w_o_a, gnorm_a, w_q_b, w_o_b, sinks_b, w_ffn_in, w_ffn_out, final_norm_w,
     w_kv) = wts
    prompt = k_buf is None
    bsz, t, _ = x.shape
    m = bsz * t
    common = dict(per_token=per_token, rows_per_batch=t)
    h = x.reshape(m, D_MODEL)
    states = []
    step_inputs = []
    new_state = None
    kv = q_first = k_state = v_state = kv_new = None
    for l in range(DEPTH):
        a1, s1, g1, a2, s2, g2 = mods[l]
        if l < N_A_LAYERS:
            qkvg, lf = norm_proj(h, (a1, s1), w_in_a, l, epilogue="hgrn",
                                 lb=lbs[l].reshape(1, D_MODEL), **common)
            gn = gnorm_a[l].reshape(1, A_DV)
            if prompt:
                mix, s_new = hgrn_scan_prompt(qkvg, lf, gn, bsz, t)
                states.append(s_new)
            else:
                qkvg = qkvg.astype(F32)
                last_a = l == N_A_LAYERS - 1
                mix, new_state = hgrn_step(qkvg, lf, gn, hgrn_state0, l,
                                           earlier=list(step_inputs) if last_a else None)
                step_inputs.append((qkvg, lf))
            w_o, lo = w_o_a, l
        else:
            j = l - N_A_LAYERS
            scale = LOG2E / math.sqrt(B_HEAD_DIM)
            if prompt:
                if j == 0:
                    q = q_first
                else:
                    q = norm_proj(h, (a1, s1), w_q_b, j, out_scale=scale, **common)
                mix = swa_prompt(q, kv, sinks_b[j], bias_tab, bsz, t)
            else:
                q = norm_proj(h, (a1, s1), w_q_b, j, out_scale=scale, out_dtype=F32, **common)
                lane_kv = (np.arange(KV_WIDTH) // B_HEAD_DIM)[None, :]
                head_kv = (np.arange(B_HEADS) // B_GROUPS)[:, None]
                qz = jnp.where((lane_kv == head_kv)[None],
                               jnp.tile(q.reshape(m, B_HEADS, B_HEAD_DIM), (1, 1, B_KV_HEADS)), 0.0)
                if j == 0:
                    mix, k_state, v_state = swa_step(qz, k_buf.reshape(m, WINDOW, KV_WIDTH),
                                                     v_buf.reshape(m, WINDOW, KV_WIDTH),
                                                     sinks_b[j], bias_row, kv_new)
                else:
                    mix = swa_step(qz, k_state, v_state, sinks_b[j], bias_row)
                mix = mix.reshape(m, D_MODEL)
            w_o, lo = w_o_b, j
        h = post_ffn(mix, h, g1, (a2, s2), g2, w_o, lo, w_ffn_in, w_ffn_out, l,
                     final_w=final_norm_w if l == DEPTH - 1 else None, **common)
        if l == N_A_LAYERS - 1:
            if prompt:
                q_first, kv = norm_proj_qkv(h, mods[l + 1][:2], w_q_b, 0, kv_mod, w_kv,
                                            q_scale=LOG2E / math.sqrt(B_HEAD_DIM), **common)
                tail =h.reshape(bsz, t, D_MODEL)[:, -WINDOW:].reshape(bsz * WINDOW, D_MODEL)
                kv_tail = norm_proj(tail, kv_mod, w_kv, 0, per_token=False,
                                    rows_per_batch=WINDOW, out_dtype=F32)
                kv_tail = kv_tail.reshape(bsz, WINDOW, 2, B_KV_HEADS, B_HEAD_DIM)
                k_state, v_state = kv_tail[:, :, 0], kv_tail[:, :, 1]
            else:
                kv_row = norm_proj(h, kv_mod, w_kv, 0, out_dtype=F32, **common)
                kv_new = (kv_row[:, :KV_WIDTH].reshape(m, 1, KV_WIDTH),
                          kv_row[:, KV_WIDTH:].reshape(m, 1, KV_WIDTH))
    y = h.reshape(bsz, t, D_MODEL)
    if prompt:
        return y, jnp.stack(states, axis=1), k_state, v_state
    cache_shape = (m, WINDOW, B_KV_HEADS, B_HEAD_DIM)
    return y, new_state, k_state.reshape(cache_shape), v_state.reshape(cache_shape)


def kernel(x_prompt, x_sample, state_hgrn, cache_swa_k, cache_swa_v, c_prompt, c_sample,
           w_in_a, w_o_a, gnorm_a, lb_a, w_kv, w_ada_kv, b_ada_kv, kv_norm_w, w_q_b, w_o_b,
           sinks_b, rel_bias, norm_w, w_ada, b_ada, w_ffn_in, w_ffn_out, final_norm_w):
    n_p, n_s = c_prompt.shape[0], c_sample.shape[0]
    rows = -(-(n_p + n_s) // SUBLANES) * SUBLANES
    c_all = jnp.concatenate(
        [c_sample, c_prompt, jnp.zeros((rows - n_p - n_s, D_MODEL), F32)], axis=0)

    zeros, ones = jnp.zeros((DEPTH, D_MODEL), F32), jnp.ones((DEPTH, D_MODEL), F32)
    p_ada = jnp.stack([zeros, norm_w[:, 0], zeros, zeros, norm_w[:, 1], zeros], axis=1)
    q_ada = jnp.stack([ones, norm_w[:, 0], ones, ones, norm_w[:, 1], ones], axis=1)
    ada = ada_project(c_all, w_ada, b_ada.reshape(DEPTH, 1, 6 * D_MODEL),
                      p_ada.reshape(DEPTH, 1, 6 * D_MODEL), q_ada.reshape(DEPTH, 1, 6 * D_MODEL))
    p_kv = jnp.concatenate([zeros[0], kv_norm_w]).reshape(1, 1, 2 * D_MODEL)
    q_kv = jnp.concatenate([ones[0], kv_norm_w]).reshape(1, 1, 2 * D_MODEL)
    ada_kv = ada_project(c_all, w_ada_kv.reshape(1, D_MODEL, 2 * D_MODEL),
                         b_ada_kv.reshape(1, 1, 2 * D_MODEL), p_kv, q_kv)

    def mods_for(row0):
        layers = [tuple((ada, l, c, row0) for c in (1, 0, 2, 4, 3, 5)) for l in range(DEPTH)]
        return layers, ((ada_kv, 0, 1, row0), (ada_kv, 0, 0, row0))

    lb_sm = jax.nn.softmax(lb_a.astype(F32), axis=0)
    lbs = jnp.cumsum(lb_sm, axis=0) - lb_sm[0:1]

    rb = rel_bias.astype(F32)[_T5_BUCKETS]
    t_idx, j_idx = np.arange(WINDOW)[:, None], np.arange(WINDOW)[None, :]
    from_prev = j_idx > t_idx
    dist = np.where(from_prev, t_idx + WINDOW - j_idx, t_idx - j_idx)
    onehot = (jnp.asarray(dist)[:, :, None] == jnp.arange(WINDOW)[None, None, :]).astype(F32)
    tab = jnp.einsum("tjd,dh->htj", onehot, rb, precision=lax.Precision.HIGHEST) * LOG2E
    bias_tab = jnp.stack([jnp.where(from_prev, MASK_VALUE, tab),
                          tab])
    bias_row = rb[::-1].T * LOG2E

    bf = lambda w: w.astype(BF16)
    wts = (bf(w_in_a), bf(w_o_a), gnorm_a, bf(w_q_b), bf(w_o_b), sinks_b, bf(w_ffn_in),
           bf(w_ffn_out), final_norm_w, bf(w_kv)[None])
    mods_p, kv_mod_p = mods_for(n_s)
    mods_s, kv_mod_s = mods_for(0)
    y_p, st_p, k_p, v_p = _trunk(x_prompt, mods_p, kv_mod_p, False, None, None, None, wts, lbs,
                                 bias_tab, bias_row)
    y_s, st_s, k_s, v_s = _trunk(x_sample, mods_s, kv_mod_s, True, state_hgrn, cache_swa_k,
                                 cache_swa_v, wts, lbs, bias_tab, bias_row)
    return (y_p, y_s, st_p, st_s, k_p, v_p, k_s, v_s)
```

```python
import functools
import math

import numpy as np
import jax
import jax.numpy as jnp
from jax import lax
from jax.experimental import pallas as pl
from jax.experimental.pallas import tpu as pltpu

F32 = jnp.float32
BF16 = jnp.bfloat16

D_MODEL = 1024
DEPTH = 4
N_A_LAYERS = 2
A_HEADS = 8
A_DK = 128
A_DV = 128
F_MIN = 1e-30
B_HEAD_DIM = 64
B_HEADS = 16
B_KV_HEADS = 4
B_GROUPS = 4
WINDOW = 128
MASK_VALUE = -1e30
N_BUCKETS = 32
MAX_DISTANCE = 128
D_FF = 2816
EPS = 1e-6
GROUP_LANES = B_GROUPS * B_HEAD_DIM
KV_WIDTH = B_KV_HEADS * B_HEAD_DIM
HEAD_DIM_SHIFT = B_HEAD_DIM.bit_length() - 1
GROUPS_SHIFT = B_GROUPS.bit_length() - 1

LOG2E = math.log2(math.e)

SUBLANES = 8
LANES = 128
BF16_SUBLANES = 16

CHUNK = 128
N_LEVELS = 7
MATMUL_LEVELS = (5, 6)
STEP_CHUNKS = 8
ROW_TILE = 1024
COL_TILE = 1024
FF_TILE = 256
ADA_COLS = 2048
SWA_BLOCKS = 8
STEP_TOKENS = 4
ATTN_TOKENS = 32
STEP_ROWS = BF16_SUBLANES
BATCH_ROWS = SUBLANES
VMEM_LIMIT_BYTES = 58 * 1024 * 1024


def _params(sem):
    return pltpu.CompilerParams(dimension_semantics=sem, vmem_limit_bytes=VMEM_LIMIT_BYTES)


def _row_tile(m, per_token, rows_per_batch):
    return min(m, ROW_TILE) if per_token else min(m, ROW_TILE, rows_per_batch)


def _sigmoid(x):
    return 0.5 * jnp.tanh(0.5 * x) + 0.5


def _silu(x):
    h = 0.5 * x
    return h * jnp.tanh(h) + h


def _norm_mod(x, a, s):
    y = x * lax.rsqrt(jnp.mean(x * x, axis=-1, keepdims=True) + EPS)
    return (y * a + s).astype(BF16)


def _resident(shape, layer):
    return pl.BlockSpec((None,) + shape, lambda i: (layer,) + (0,) * len(shape),
                        pipeline_mode=pl.Buffered(1))


def _mod_operand(mod, tm, per_token):
    arr, layer, chunk, row0 = mod
    if per_token:
        return arr, pl.BlockSpec((None, tm, D_MODEL), lambda i: (layer, i, chunk))
    assert row0 % BATCH_ROWS == 0
    return arr, pl.BlockSpec((None, BATCH_ROWS, D_MODEL),
                             lambda i: (layer, row0 // BATCH_ROWS, chunk))


def _row_value(ref, tiles_per_batch):
    if tiles_per_batch is None:
        return ref[...]
    return ref[pl.ds(lax.div(pl.program_id(0), tiles_per_batch), 1), :]


def _tiles_per_batch(m, tm, per_token, rows_per_batch):
    if per_token:
        return None
    assert rows_per_batch % tm == 0 and m // rows_per_batch <= BATCH_ROWS
    return rows_per_batch // tm


def _store_kv_repeated(acc, o_ref):
    low_half = lax.broadcasted_iota(jnp.int32, (acc.shape[0], LANES), 1) < B_HEAD_DIM
    for c in range(acc.shape[1] // LANES):
        x = acc[:, c * LANES:(c + 1) * LANES]
        swapped = pltpu.roll(x, B_HEAD_DIM, 1)
        for half, rep in enumerate((jnp.where(low_half, x, swapped),
                                    jnp.where(low_half, swapped, x))):
            rep = rep.astype(o_ref.dtype)
            base = (2 * c + half) * GROUP_LANES
            o_ref[:, base:base + LANES] = rep
            o_ref[:, base + LANES:base + GROUP_LANES] = rep


def _norm_qkv_body(x_ref, aq_ref, sq_ref, akv_ref, skv_ref, wq_ref, wkv_ref, q_ref, kv_ref, *,
                   tiles_per_batch, q_scale):
    x = x_ref[...]
    y = x * lax.rsqrt(jnp.mean(x * x, axis=-1, keepdims=True) + EPS)
    row = functools.partial(_row_value, tiles_per_batch=tiles_per_batch)
    xn_kv = (y * row(akv_ref) + row(skv_ref)).astype(BF16)
    _store_kv_repeated(jnp.dot(xn_kv, wkv_ref[...], preferred_element_type=F32), kv_ref)
    xn_q = (y * row(aq_ref) + row(sq_ref)).astype(BF16)
    q_ref[...] = (jnp.dot(xn_q, wq_ref[...], preferred_element_type=F32) * q_scale).astype(q_ref.dtype)


def norm_proj_qkv(x, mod_q, w_q, layer_q, mod_kv, w_kv, *, q_scale, per_token, rows_per_batch):
    m, k = x.shape
    tm = _row_tile(m, per_token, rows_per_batch)
    assert m % tm == 0
    mods = [_mod_operand(v, tm, per_token) for v in (*mod_q, *mod_kv)]
    n_q, n_kv = w_q.shape[-1], w_kv.shape[-1]
    tile = lambda n: pl.BlockSpec((tm, n), lambda i: (i, 0))
    return pl.pallas_call(
        functools.partial(_norm_qkv_body, q_scale=q_scale,
                          tiles_per_batch=_tiles_per_batch(m, tm, per_token, rows_per_batch)),
        grid=(m // tm,),
        in_specs=[tile(k)] + [spec for _, spec in mods] + [
            _resident((k, n_q), layer_q), _resident((k, n_kv), 0)],
        out_specs=[tile(n_q), tile(n_kv * B_GROUPS)],
        out_shape=[jax.ShapeDtypeStruct((m, n_q), BF16),
                   jax.ShapeDtypeStruct((m, n_kv * B_GROUPS), BF16)],
        compiler_params=_params(("parallel",)),
        name=f"norm_proj_qkv_m{m}",
    )(x, *[arr for arr, _ in mods], w_q, w_kv)


def _norm_proj_body(*refs, tiles_per_batch, epilogue, out_scale):
    x_ref, a_ref, s_ref, w_ref = refs[:4]
    xn = _norm_mod(x_ref[...], _row_value(a_ref, tiles_per_batch),
                   _row_value(s_ref, tiles_per_batch))

    tn = min(COL_TILE, w_ref.shape[1])

    def proj(c):
        return jnp.dot(xn, w_ref[:, c:c + tn], preferred_element_type=F32)

    if epilogue == "plain":
        (o_ref,) = refs[4:]
        for c in range(0, o_ref.shape[1], tn):
            acc = proj(c)
            if out_scale is not None:
                acc = acc * out_scale
            o_ref[:, c:c + tn] = acc.astype(o_ref.dtype)
    elif epilogue == "kv_rep":
        (o_ref,) = refs[4:]
        _store_kv_repeated(proj(0), o_ref)
    elif epilogue == "hgrn":
        lb_ref, o_ref, lf_ref = refs[4:]
        d = D_MODEL
        lb = lb_ref[...]
        o_ref[:, 0:d] = _silu(proj(0)).astype(o_ref.dtype)
        f = lb + (1.0 - lb) * _sigmoid(proj(d))
        o_ref[:, d:2 * d] = (1.0 - f).astype(o_ref.dtype)
        lf_ref[...] = jnp.log2(jnp.maximum(f, F_MIN))
        o_ref[:, 2 * d:3 * d] = proj(2 * d).astype(o_ref.dtype)
        o_ref[:, 3 * d:4 * d] = _silu(proj(3 * d)).astype(o_ref.dtype)
    else:
        raise ValueError(epilogue)


def norm_proj(x, mod, w, layer, *, per_token, rows_per_batch, epilogue="plain", lb=None,
              out_dtype=BF16, out_scale=None):
    m, k = x.shape
    n = w.shape[-1]
    tm = _row_tile(m, per_token, rows_per_batch)
    assert m % tm == 0 and (per_token or rows_per_batch % tm == 0)
    (a_arr, a_spec), (s_arr, s_spec) = [_mod_operand(v, tm, per_token) for v in mod]
    in_specs = [pl.BlockSpec((tm, k), lambda i: (i, 0)), a_spec, s_spec, _resident((k, n), layer)]
    args = [x, a_arr, s_arr, w]
    n_out = n * B_GROUPS if epilogue == "kv_rep" else n
    out_specs = [pl.BlockSpec((tm, n_out), lambda i: (i, 0))]
    out_shape = [jax.ShapeDtypeStruct((m, n_out), out_dtype)]
    if epilogue == "hgrn":
        in_specs.append(pl.BlockSpec((1, D_MODEL), lambda i: (0, 0)))
        args.append(lb)
        out_specs.append(pl.BlockSpec((tm, D_MODEL), lambda i: (i, 0)))
        out_shape.append(jax.ShapeDtypeStruct((m, D_MODEL), F32))
    outs = pl.pallas_call(
        functools.partial(_norm_proj_body, epilogue=epilogue, out_scale=out_scale,
                          tiles_per_batch=_tiles_per_batch(m, tm, per_token, rows_per_batch)),
        grid=(m // tm,),
        in_specs=in_specs,
        out_specs=out_specs,
        out_shape=out_shape,
        compiler_params=_params(("parallel",)),
        name=f"norm_proj_{epilogue}_m{m}_n{n}",
    )(*args)
    return outs if epilogue == "hgrn" else outs[0]


def _post_ffn_body(*refs, tiles_per_batch, final_norm, q_scale):
    (mix_ref, h_ref, g1_ref, a2_ref, s2_ref, g2_ref, wo_ref, win_ref, wout_ref) = refs[:9]
    rest = refs[9:]
    fw_ref = rest[0] if final_norm else None
    act_ref = rest[-1]
    if q_scale is None:
        o_ref = rest[-2]
    else:
        aq_ref, sq_ref, wq_ref, o_ref, q_ref = rest[-6:-1]
    row = functools.partial(_row_value, tiles_per_batch=tiles_per_batch)
    h_mid = h_ref[...] + row(g1_ref) * jnp.dot(mix_ref[...].astype(BF16), wo_ref[...],
                                               preferred_element_type=F32)
    xn = _norm_mod(h_mid, row(a2_ref), row(s2_ref))
    for c in range(0, D_FF, FF_TILE):
        gate = jnp.dot(xn, win_ref[:, c:c + FF_TILE], preferred_element_type=F32)
        up = jnp.dot(xn, win_ref[:, D_FF + c:D_FF + c + FF_TILE], preferred_element_type=F32)
        act_ref[:, c:c + FF_TILE] = (_silu(gate) * up).astype(BF16)
    out = h_mid + row(g2_ref) * jnp.dot(act_ref[...], wout_ref[...], preferred_element_type=F32)
    if final_norm:
        out = out * lax.rsqrt(jnp.mean(out * out, axis=-1, keepdims=True) + EPS) * fw_ref[...]
    o_ref[...] = out
    if q_scale is not None:
        xq = _norm_mod(out, row(aq_ref), row(sq_ref))
        q_ref[...] = (jnp.dot(xq, wq_ref[...], preferred_element_type=F32) * q_scale).astype(
            q_ref.dtype)


def post_ffn(mix, h, g1, mod2, g2, w_o, lo, w_ffn_in, w_ffn_out, lf, *, per_token,
             rows_per_batch, final_w=None, next_q=None):
    m, d = h.shape
    tm = _row_tile(m, per_token, rows_per_batch)
    assert m % tm == 0 and (per_token or rows_per_batch % tm == 0)
    rows = [_mod_operand(v, tm, per_token) for v in (g1, mod2[0], mod2[1], g2)]
    tile = pl.BlockSpec((tm, d), lambda i: (i, 0))
    in_specs = [tile, tile] + [spec for _, spec in rows] + [
        _resident((d, d), lo), _resident((d, 2 * D_FF), lf), _resident((D_FF, d), lf)]
    args = [mix, h] + [arr for arr, _ in rows] + [w_o, w_ffn_in, w_ffn_out]
    if final_w is not None:
        in_specs.append(pl.BlockSpec((1, d), lambda i: (0, 0)))
        args.append(final_w.reshape(1, d))
    out_specs, out_shape = [tile], [jax.ShapeDtypeStruct((m, d), F32)]
    if next_q is not None:
        assert final_w is None
        mod_q, w_q, layer_q, _ = next_q
        q_rows = [_mod_operand(v, tm, per_token) for v in mod_q]
        in_specs += [spec for _, spec in q_rows] + [_resident((d, d), layer_q)]
        args += [arr for arr, _ in q_rows] + [w_q]
        out_specs.append(tile)
        out_shape.append(jax.ShapeDtypeStruct((m, d), BF16))
    outs = pl.pallas_call(
        functools.partial(_post_ffn_body, final_norm=final_w is not None,
                          q_scale=None if next_q is None else next_q[3],
                          tiles_per_batch=_tiles_per_batch(m, tm, per_token, rows_per_batch)),
        grid=(m // tm,),
        in_specs=in_specs,
        out_specs=out_specs,
        out_shape=out_shape,
        scratch_shapes=[pltpu.VMEM((tm, D_FF), BF16)],
        compiler_params=_params(("parallel",)),
        name=f"post_ffn_m{m}",
    )(*args)
    return outs if next_q is not None else outs[0]


def _ada_body(c_ref, w_ref, b_ref, p_ref, q_ref, o_ref):
    c = _silu(c_ref[...]).astype(BF16)
    acc = jnp.dot(c, w_ref[...].astype(BF16), preferred_element_type=F32)
    o_ref[...] = p_ref[...] + q_ref[...] * (acc + b_ref[...])


def ada_project(c_all, w, b, p, q):
    n_l, _, n_cols = w.shape
    tn = min(ADA_COLS, n_cols)
    assert n_cols % tn == 0
    rows = c_all.shape[0]
    vec_spec = pl.BlockSpec((None, 1, tn), lambda l, j: (l, 0, j))
    return pl.pallas_call(
        _ada_body,
        grid=(n_l, n_cols // tn),
        in_specs=[
            pl.BlockSpec((rows, D_MODEL), lambda l, j: (0, 0)),
            pl.BlockSpec((None, D_MODEL, tn), lambda l, j: (l, 0, j)),
            vec_spec, vec_spec, vec_spec,
        ],
        out_specs=pl.BlockSpec((None, rows, tn), lambda l, j: (l, 0, j)),
        out_shape=jax.ShapeDtypeStruct((n_l, rows, n_cols), F32),
        compiler_params=_params(("parallel", "parallel")),
        name="ada_project",
    )(c_all, w, b, p, q)


def _scan_tables():
    c = CHUNK
    t = np.arange(c)[:, None]
    u = np.arange(c)[None, :]
    blocks = [(u <= t)]
    for level in MATMUL_LEVELS:
        p = N_LEVELS - level
        odd = ((t >> p) & 1) == 1
        start = (t >> p) << p
        end = (((t >> p) + 1) << p) - 1
        blocks.append(np.where(odd, (u >= start) & (u <= t), (u > t) & (u <= end)))
    sums = np.concatenate(blocks, axis=0).astype(np.float32)
    x = np.arange(c)[:, None] ^ np.arange(c)[None, :]
    msb = np.floor(np.log2(np.maximum(x, 1))).astype(np.int32)
    lvl = np.where(x == 0, 0, N_LEVELS - msb)
    lvl = np.where(np.arange(c)[None, :] > np.arange(c)[:, None], -1, lvl).astype(np.int32)
    return sums, lvl


_SCAN_SUMS, _SCAN_LEVELS = _scan_tables()

_TN = (((1,), (1,)), ((), ()))
_TM = (((0,), (0,)), ((), ()))


def _level_exponent(ex, log2_f, odd_rows, cols, level):
    m = CHUNK >> level
    if level in MATMUL_LEVELS:
        i = 1 + MATMUL_LEVELS.index(level)
        return ex[i * CHUNK:(i + 1) * CHUNK, cols]
    if m == 1:
        return jnp.where(odd_rows, log2_f[:, cols], 0.0)
    parts = []
    for i in range(CHUNK // m):
        blk = ex[i * m:(i + 1) * m, cols]
        if i & 1:
            parts.append(blk - ex[i * m - 1:i * m, cols])
        else:
            parts.append(ex[(i + 1) * m - 1:(i + 1) * m, cols] - blk)
    return jnp.concatenate(parts, axis=0)


def _level_mix(q, kk, odd_rows, level):
    m = CHUNK >> level
    if m >= 8:
        parts = [(q if (i & 1) else kk)[i * m:(i + 1) * m] for i in range(CHUNK // m)]
        return jnp.concatenate(parts, axis=0)
    return jnp.where(odd_rows, q, kk)


def _odd_blocks(x, m):
    return jnp.concatenate([x[i * m:(i + 1) * m] for i in range(1, CHUNK // m, 2)], axis=0)


def _level_product(mix, level):
    m = CHUNK >> level
    if m < 8:
        return lax.dot_general(mix, mix, _TN, preferred_element_type=F32)
    lhs = _odd_blocks(mix, m) if m >= 16 else mix
    a = lax.dot_general(lhs, mix, _TN, preferred_element_type=F32)
    return a if m >= 16 else _odd_blocks(a, m)


def _level_select(att, a, lvl, level):
    m = CHUNK >> level
    if m < 8:
        return jnp.where(lvl == level, a, att)
    tiles = []
    for i in range(CHUNK // m):
        rows = slice(i * m, (i + 1) * m)
        if i & 1:
            a_rows = a[(i // 2) * m:(i // 2 + 1) * m]
            tiles.append(jnp.where(lvl[rows] == level, a_rows, att[rows]))
        else:
            tiles.append(att[rows])
    return jnp.concatenate(tiles, axis=0)


def _hgrn_scan_body(q_ref, k_ref, v_ref, g_ref, lf_ref, gn_ref, sums_ref, lvl_ref,
                    o_ref, st_ref, state_ref, ex_ref):
    n = pl.program_id(1)

    @pl.when(n == 0)
    def _():
        state_ref[...] = jnp.zeros_like(state_ref)

    lvl = lvl_ref[...]
    row = lax.broadcasted_iota(jnp.int32, (CHUNK, A_DK), 0)
    gn = gn_ref[...]
    heads = [slice(h * A_DK, (h + 1) * A_DK) for h in range(A_HEADS)]
    for c in range(STEP_CHUNKS):
        rows = slice(c * CHUNK, (c + 1) * CHUNK)
        ex = ex_ref.at[c]
        log2_f = lf_ref[rows, :]
        hi = log2_f.astype(BF16)
        rem = log2_f - hi.astype(F32)
        mid = rem.astype(BF16)
        lo = (rem - mid.astype(F32)).astype(BF16)
        ex[...] = jnp.dot(sums_ref[...], jnp.concatenate([hi, mid, lo], axis=0),
                          preferred_element_type=F32)
        b_end = ex[CHUNK - 1:CHUNK, :]
        d_end = jnp.exp2(b_end)
        group = range(A_HEADS)
        q_bf = {h: q_ref[rows, heads[h]] for h in group}
        k_bf = {h: k_ref[rows, heads[h]] for h in group}
        q = {h: q_bf[h].astype(F32) for h in group}
        kk = {h: k_bf[h].astype(F32) for h in group}
        att = {h: jnp.where(lvl == 0, lax.dot_general(q_bf[h], k_bf[h], _TN,
                                                      preferred_element_type=F32), 0.0)
               for h in group}
        for level in range(1, N_LEVELS + 1):
            odd_rows = ((row >> (N_LEVELS - level)) & 1) == 1
            for h in group:
                e = jnp.exp2(_level_exponent(ex, log2_f, odd_rows, heads[h], level))
                mix = (_level_mix(q[h], kk[h], odd_rows, level) * e).astype(BF16)
                att[h] = _level_select(att[h], _level_product(mix, level), lvl, level)
        for h in group:
            sl = heads[h]
            b_incl = ex[0:CHUNK, sl]
            q_dec = (q[h] * jnp.exp2(b_incl)).astype(BF16)
            k_dec = (kk[h] * jnp.exp2(b_end[:, sl] - b_incl)).astype(BF16)
            v = v_ref[rows, sl]
            s_t = state_ref[h]
            o = (lax.dot_general(q_dec, s_t.astype(BF16), _TN, preferred_element_type=F32)
                 + jnp.dot(att[h].astype(BF16), v, preferred_element_type=F32))
            state_ref[h] = d_end[:, sl] * s_t + lax.dot_general(
                v, k_dec, _TM, preferred_element_type=F32)
            y = o * lax.rsqrt(jnp.mean(o * o, axis=-1, keepdims=True) + EPS)
            o_ref[rows, sl] = (y * gn * g_ref[rows, sl].astype(F32)).astype(o_ref.dtype)

    @pl.when(n == pl.num_programs(1) - 1)
    def _():
        for h in range(A_HEADS):
            st_ref[0, h] = state_ref[h].T


def hgrn_scan_prompt(qkvg, lf, gn, bsz, t):
    tr = STEP_CHUNKS * CHUNK
    ns = t // tr
    assert t % tr == 0
    sec = lambda k: pl.BlockSpec((tr, D_MODEL), lambda b, n, k=k: (b * ns + n, k))
    n_sum = _SCAN_SUMS.shape[0]
    return pl.pallas_call(
        _hgrn_scan_body,
        grid=(bsz, ns),
        in_specs=[sec(0), sec(1), sec(2), sec(3), sec(0),
                  pl.BlockSpec((1, A_DV), lambda b, n: (0, 0)),
                  pl.BlockSpec((n_sum, 3 * CHUNK), lambda b, n: (0, 0)),
                  pl.BlockSpec((CHUNK, CHUNK), lambda b, n: (0, 0))],
        out_specs=[pl.BlockSpec((tr, D_MODEL), lambda b, n: (b * ns + n, 0)),
                   pl.BlockSpec((1, A_HEADS, A_DK, A_DV), lambda b, n: (b, 0, 0, 0))],
        out_shape=[jax.ShapeDtypeStruct((bsz * t, D_MODEL), BF16),
                   jax.ShapeDtypeStruct((bsz, A_HEADS, A_DK, A_DV), F32)],
        scratch_shapes=[pltpu.VMEM((A_HEADS, A_DV, A_DK), F32),
                        pltpu.VMEM((STEP_CHUNKS, n_sum, D_MODEL), F32)],
        compiler_params=_params(("parallel", "arbitrary")),
        name="hgrn_scan",
    )(qkvg, qkvg, qkvg, qkvg, lf, gn, jnp.asarray(np.tile(_SCAN_SUMS, (1, 3)), BF16),
      jnp.asarray(_SCAN_LEVELS))


def _step_columns(decay, kk, q, v_row):
    r = lax.broadcasted_iota(jnp.int32, (STEP_ROWS, A_DK), 0)
    hi = decay.astype(BF16).astype(F32)
    rem = decay - hi
    mid = rem.astype(BF16).astype(F32)
    lo = rem - mid
    lhs = jnp.where(r == 0, hi, jnp.where(r == 1, mid, jnp.where(r == 2, lo,
                                                                 jnp.where(r == 3, kk, 0.0))))
    parts = [jnp.where(r < 3, 1.0, 0.0), jnp.where(r == 3, v_row, 0.0)]
    if q is not None:
        lhs = jnp.where(r == 4, q, lhs)
        parts.append(jnp.where(r == 4, 1.0, 0.0))
    rhs = jnp.concatenate(parts, axis=1).astype(BF16)
    out = lax.dot_general(lhs.astype(BF16), rhs, _TM, preferred_element_type=F32)
    return [out[:, i * A_DV:(i + 1) * A_DV] for i in range(len(parts))]


def _hgrn_step_body(*refs, n_replay, write_state):
    q_ref, k_ref, v_ref, g_ref, lf_ref, gn_ref, s_ref = refs[:7]
    replay = [refs[7 + 4 * i:11 + 4 * i] for i in range(n_replay)]
    outs = refs[7 + 4 * n_replay:]
    o_ref = outs[0]
    gn = gn_ref[...]
    for b in range(STEP_TOKENS):
        decay = jnp.exp2(lf_ref[b])
        q, kk, v, gate = q_ref[b], k_ref[b], v_ref[b], g_ref[b]
        for h in range(A_HEADS):
            sl = slice(h * A_DK, (h + 1) * A_DK)
            if write_state:
                dec_m, kv_m, q_m = _step_columns(decay[:, sl], kk[:, sl], q[:, sl], v[:, sl])
                s_new = dec_m * s_ref[b, h] + kv_m
                outs[1][b, n_replay, h] = s_new
                o = jnp.sum(q_m * s_new, axis=0, keepdims=True)
            else:
                r = lax.broadcasted_iota(jnp.int32, (STEP_ROWS, A_DK), 0)
                lhs = jnp.where(r == 0, q[:, sl] * decay[:, sl], 0.0).astype(BF16)
                ones = jnp.where(r == 0, 1.0, 0.0).astype(BF16)
                qd_m = lax.dot_general(lhs, ones, _TM, preferred_element_type=F32)
                qk = jnp.sum(q[:, sl] * kk[:, sl], axis=-1, keepdims=True)
                o = jnp.sum(qd_m * s_ref[b, h], axis=0, keepdims=True) + qk * v[:, sl]
            y = o * lax.rsqrt(jnp.mean(o * o, axis=-1, keepdims=True) + EPS)
            o_ref[b, :, sl] = y * gn * gate[:, sl]
        for i, (kp_ref, vp_ref, lfp_ref, sp_ref) in enumerate(replay):
            decay_p = jnp.exp2(lfp_ref[b])
            for h in range(A_HEADS):
                sl = slice(h * A_DK, (h + 1) * A_DK)
                dec_m, kv_m = _step_columns(decay_p[:, sl], kp_ref[b][:, sl], None,
                                            vp_ref[b][:, sl])
                outs[1][b, i, h] = dec_m * sp_ref[b, h] + kv_m


def hgrn_step(qkvg, lf, gn, state, layer, earlier=None):
    bsz = qkvg.shape[0]
    nb = STEP_TOKENS
    assert bsz % nb == 0
    sec = lambda k: pl.BlockSpec((nb, 1, D_MODEL), lambda b, k=k: (b, 0, k))
    state_blk = lambda l: pl.BlockSpec((nb, None, A_HEADS, A_DK, A_DV),
                                       lambda b, l=l: (b, l, 0, 0, 0))
    as3 = lambda a: a.reshape(bsz, 1, a.shape[-1])
    in_specs = [sec(0), sec(1), sec(2), sec(3), sec(0),
                pl.BlockSpec((1, A_DV), lambda b: (0, 0)), state_blk(layer)]
    args = [as3(qkvg)] * 4 + [as3(lf), gn, state]
    write_state = earlier is not None
    for i, (qkvg_p, lf_p) in enumerate(earlier or ()):
        in_specs += [sec(1), sec(2), sec(0), state_blk(i)]
        args += [as3(qkvg_p), as3(qkvg_p), as3(lf_p), state]
    out_specs = [pl.BlockSpec((nb, 1, D_MODEL), lambda b: (b, 0, 0))]
    out_shape = [jax.ShapeDtypeStruct((bsz, 1, D_MODEL), F32)]
    if write_state:
        assert len(earlier) == layer
        n_l = layer + 1
        out_specs.append(pl.BlockSpec((nb, n_l, A_HEADS, A_DK, A_DV), lambda b: (b, 0, 0, 0, 0)))
        out_shape.append(jax.ShapeDtypeStruct((bsz, n_l, A_HEADS, A_DK, A_DV), F32))
    outs = pl.pallas_call(
        functools.partial(_hgrn_step_body, n_replay=len(earlier or ()), write_state=write_state),
        grid=(bsz // nb,),
        in_specs=in_specs,
        out_specs=out_specs,
        out_shape=out_shape,
        compiler_params=_params(("parallel",)),
        name="hgrn_step",
    )(*args)
    o = outs[0].reshape(bsz, D_MODEL)
    return (o, outs[1]) if write_state else (o, None)


def _t5_buckets():
    max_exact = N_BUCKETS // 2
    d = np.arange(WINDOW)
    large = max_exact + (np.log(np.maximum(d, 1).astype(np.float32) / max_exact)
                         / math.log(MAX_DISTANCE / max_exact)
                         * (N_BUCKETS - max_exact)).astype(np.int32)
    large = np.clip(large, 0, N_BUCKETS - 1)
    return np.where(d < max_exact, d, large).astype(np.int32)


_T5_BUCKETS = _t5_buckets()


def _swa_prompt_body(sink_ref, q_ref, kp_ref, kc_ref, vp_ref, vc_ref, bias0_ref, bias_ref, o_ref):
    lane_head = lax.broadcasted_iota(jnp.int32, (WINDOW, GROUP_LANES), 1) >> HEAD_DIM_SHIFT
    from_prev = (lax.broadcasted_iota(jnp.int32, (WINDOW, WINDOW), 1)
                 > lax.broadcasted_iota(jnp.int32, (WINDOW, WINDOW), 0))
    groups = [slice(g * GROUP_LANES, (g + 1) * GROUP_LANES) for g in range(B_KV_HEADS)]
    for blk in range(SWA_BLOCKS):
        rows = slice(blk * WINDOW, (blk + 1) * WINDOW)
        prev_rows = slice((blk - 1) * WINDOW, blk * WINDOW)
        table = bias0_ref if blk == 0 else bias_ref

        def keys(prev_ref, cur_ref, cs):
            prev = prev_ref[:, cs] if blk == 0 else cur_ref[prev_rows, cs]
            return jnp.concatenate([prev, cur_ref[rows, cs]], axis=0)

        all_logits = []
        for cs in groups:
            qg = q_ref[rows, cs].astype(F32)
            qstack = jnp.concatenate(
                [jnp.where(lane_head == j, qg, 0.0).astype(BF16) for j in range(B_GROUPS)], axis=0)
            all_logits.append(lax.dot_general(qstack, keys(kp_ref, kc_ref, cs), _TN,
                                              preferred_element_type=F32))
        for g, cs in enumerate(groups):
            logits = all_logits[g]
            vcat = keys(vp_ref, vc_ref, cs)
            acc = jnp.zeros((WINDOW, GROUP_LANES), F32)
            for j in range(B_GROUPS):
                head = g * B_GROUPS + j
                hr = slice(j * WINDOW, (j + 1) * WINDOW)
                lg = jnp.where(from_prev, logits[hr, :WINDOW], logits[hr, WINDOW:]) + table[head]
                sink = sink_ref[head] * LOG2E
                m = jnp.maximum(jnp.max(lg, axis=-1, keepdims=True), sink)
                p = jnp.exp2(lg - m)
                denom = jnp.sum(p, axis=-1, keepdims=True) + jnp.exp2(sink - m)
                p_prev = jnp.where(from_prev, p, 0.0)
                p_both = jnp.concatenate([p_prev, p - p_prev], axis=1).astype(BF16)
                pv = jnp.dot(p_both, vcat, preferred_element_type=F32)
                acc = jnp.where(lane_head == j, pv / denom, acc)
            o_ref[rows, cs] = acc.astype(o_ref.dtype)


def swa_prompt(q, kv_rep, sinks, bias_tabs, bsz, t):
    tr = SWA_BLOCKS * WINDOW
    ns = t // tr
    assert t % tr == 0
    cur = lambda c: pl.BlockSpec((tr, D_MODEL), lambda b, n, c=c: (b * ns + n, c))
    prev = lambda c: pl.BlockSpec(
        (WINDOW, D_MODEL),
        lambda b, n, c=c: (jnp.maximum((b * ns + n) * SWA_BLOCKS - 1, 0), c))
    table = lambda first: pl.BlockSpec(
        (None, B_HEADS, WINDOW, WINDOW),
        (lambda b, n: (jnp.minimum(n, 1), 0, 0, 0)) if first else (lambda b, n: (1, 0, 0, 0)))
    return pl.pallas_call(
        _swa_prompt_body,
        grid=(bsz, ns),
        in_specs=[pl.BlockSpec(memory_space=pltpu.SMEM),
                  cur(0), prev(0), cur(0), prev(1), cur(1), table(True), table(False)],
        out_specs=pl.BlockSpec((tr, D_MODEL), lambda b, n: (b * ns + n, 0)),
        out_shape=jax.ShapeDtypeStruct((bsz * t, D_MODEL), BF16),
        compiler_params=_params(("parallel", "arbitrary")),
        name="swa_prompt",
    )(sinks, q, kv_rep, kv_rep, kv_rep, kv_rep, bias_tabs, bias_tabs)


def _swa_step_body(*refs, shift_cache):
    sink_ref, qz_ref, k_ref, v_ref, bias_ref, fold_ref = refs[:6]
    if shift_cache:
        kn_ref, vn_ref, o_ref, ko_ref, vo_ref = refs[6:]
        last = lax.broadcasted_iota(jnp.int32, (WINDOW, KV_WIDTH), 0) == WINDOW - 1
    else:
        (o_ref,) = refs[6:]
    sink = sink_ref[...] * LOG2E
    head_kv = lax.broadcasted_iota(jnp.int32, (B_HEADS, KV_WIDTH), 0) >> GROUPS_SHIFT
    lane_kv = lax.broadcasted_iota(jnp.int32, (B_HEADS, KV_WIDTH), 1) >> HEAD_DIM_SHIFT
    for b in range(ATTN_TOKENS):
        if shift_cache:
            k = jnp.where(last, kn_ref[b], pltpu.roll(k_ref[b], WINDOW - 1, 0))
            v = jnp.where(last, vn_ref[b], pltpu.roll(v_ref[b], WINDOW - 1, 0))
            ko_ref[b] = k
            vo_ref[b] = v
        else:
            k, v = k_ref[b], v_ref[b]
        qz = qz_ref[b].astype(BF16)
        logits = lax.dot_general(qz, k.astype(BF16), _TN, preferred_element_type=F32)
        logits = logits + bias_ref[...]
        m = jnp.maximum(jnp.max(logits, axis=-1, keepdims=True), sink)
        p = jnp.exp2(logits - m)
        denom = jnp.sum(p, axis=-1, keepdims=True) + jnp.exp2(sink - m)
        pv = jnp.dot(p.astype(BF16), v.astype(BF16), preferred_element_type=F32) / denom
        own = jnp.where(head_kv == lane_kv, pv, 0.0).astype(BF16)
        o_ref[b] = jnp.dot(own, fold_ref[...], preferred_element_type=F32)


def swa_step(qz, k_cache, v_cache, sinks, bias_row, kv_new=None):
    bsz = qz.shape[0]
    nt = ATTN_TOKENS
    assert bsz % nt == 0
    fold = np.zeros((KV_WIDTH, B_HEAD_DIM), np.float32)
    fold[np.arange(KV_WIDTH), np.arange(KV_WIDTH) % B_HEAD_DIM] = 1.0
    cache_blk = pl.BlockSpec((nt, WINDOW, KV_WIDTH), lambda b: (b, 0, 0))
    in_specs = [pl.BlockSpec((B_HEADS, 1), lambda b: (0, 0)),
                pl.BlockSpec((nt, B_HEADS, KV_WIDTH), lambda b: (b, 0, 0)),
                cache_blk, cache_blk,
                pl.BlockSpec((B_HEADS, WINDOW), lambda b: (0, 0)),
                pl.BlockSpec((KV_WIDTH, B_HEAD_DIM), lambda b: (0, 0))]
    args = [sinks.reshape(B_HEADS, 1), qz, k_cache, v_cache, bias_row, jnp.asarray(fold, BF16)]
    out_specs = [pl.BlockSpec((nt, B_HEADS, B_HEAD_DIM), lambda b: (b, 0, 0))]
    out_shape = [jax.ShapeDtypeStruct((bsz, B_HEADS, B_HEAD_DIM), F32)]
    if kv_new is not None:
        row_blk = pl.BlockSpec((nt, 1, KV_WIDTH), lambda b: (b, 0, 0))
        in_specs += [row_blk, row_blk]
        args += list(kv_new)
        out_specs += [cache_blk, cache_blk]
        out_shape += [jax.ShapeDtypeStruct(k_cache.shape, F32)] * 2
    outs = pl.pallas_call(
        functools.partial(_swa_step_body, shift_cache=kv_new is not None),
        grid=(bsz // nt,),
        in_specs=in_specs,
        out_specs=out_specs,
        out_shape=out_shape,
        compiler_params=_params(("parallel",)),
        name="swa_step",
    )(*args)
    return outs if kv_new is not None else outs[0]


def _trunk(x, mods, kv_mod, per_token, hgrn_state0, k_buf, v_buf, wts, lbs, bias_tab, bias_row):
    (w_in_a, w_o_a, gnorm_a, w_q_b, w_o_b, sinks_b, w_ffn_in, w_ffn_out, final_norm_w,
     w_kv) = wts
    prompt = k_buf is None
    bsz, t, _ = x.shape
    m = bsz * t
    common = dict(per_token=per_token, rows_per_batch=t)
    h = x.reshape(m, D_MODEL)
    states = []
    step_inputs = []
    new_state = None
    kv = q_first = k_state = v_state = kv_new = None
    for l in range(DEPTH):
        a1, s1, g1, a2, s2, g2 = mods[l]
        if l < N_A_LAYERS:
            qkvg, lf = norm_proj(h, (a1, s1), w_in_a, l, epilogue="hgrn",
                                 lb=lbs[l].reshape(1, D_MODEL), **common)
            gn = gnorm_a[l].reshape(1, A_DV)
            if prompt:
                mix, s_new = hgrn_scan_prompt(qkvg, lf, gn, bsz, t)
                states.append(s_new)
            else:
                qkvg = qkvg.astype(F32)
                last_a = l == N_A_LAYERS - 1
                mix, new_state = hgrn_step(qkvg, lf, gn, hgrn_state0, l,
                                           earlier=list(step_inputs) if last_a else None)
                step_inputs.append((qkvg, lf))
            w_o, lo = w_o_a, l
        else:
            j = l - N_A_LAYERS
            scale = LOG2E / math.sqrt(B_HEAD_DIM)
            if prompt:
                mix = swa_prompt(q_first, kv, sinks_b[j], bias_tab, bsz, t)
            else:
                q = norm_proj(h, (a1, s1), w_q_b, j, out_scale=scale, out_dtype=F32, **common)
                lane_kv = (np.arange(KV_WIDTH) // B_HEAD_DIM)[None, :]
                head_kv = (np.arange(B_HEADS) // B_GROUPS)[:, None]
                qz = jnp.where((lane_kv == head_kv)[None],
                               jnp.tile(q.reshape(m, B_HEADS, B_HEAD_DIM), (1, 1, B_KV_HEADS)), 0.0)
                if j == 0:
                    mix, k_state, v_state = swa_step(qz, k_buf.reshape(m, WINDOW, KV_WIDTH),
                                                     v_buf.reshape(m, WINDOW, KV_WIDTH),
                                                     sinks_b[j], bias_row, kv_new)
                else:
                    mix = swa_step(qz, k_state, v_state, sinks_b[j], bias_row)
                mix = mix.reshape(m, D_MODEL)
            w_o, lo = w_o_b, j
        if prompt and N_A_LAYERS <= l < DEPTH - 1:
            next_q = (mods[l + 1][:2], w_q_b, l + 1 - N_A_LAYERS, LOG2E / math.sqrt(B_HEAD_DIM))
            h, q_first = post_ffn(mix, h, g1, (a2, s2), g2, w_o, lo, w_ffn_in, w_ffn_out, l,
                                  next_q=next_q, **common)
        else:
            h = post_ffn(mix, h, g1, (a2, s2), g2, w_o, lo, w_ffn_in, w_ffn_out, l,
                         final_w=final_norm_w if l == DEPTH - 1 else None, **common)
        if l == N_A_LAYERS - 1:
            if prompt:
                q_first, kv = norm_proj_qkv(h, mods[l + 1][:2], w_q_b, 0, kv_mod, w_kv,
                                            q_scale=LOG2E / math.sqrt(B_HEAD_DIM), **common)
                tail =h.reshape(bsz, t, D_MODEL)[:, -WINDOW:].reshape(bsz * WINDOW, D_MODEL)
                kv_tail = norm_proj(tail, kv_mod, w_kv, 0, per_token=False,
                                    rows_per_batch=WINDOW, out_dtype=F32)
                kv_tail = kv_tail.reshape(bsz, WINDOW, 2, B_KV_HEADS, B_HEAD_DIM)
                k_state, v_state = kv_tail[:, :, 0], kv_tail[:, :, 1]
            else:
                kv_row = norm_proj(h, kv_mod, w_kv, 0, out_dtype=F32, **common)
                kv_new = (kv_row[:, :KV_WIDTH].reshape(m, 1, KV_WIDTH),
                          kv_row[:, KV_WIDTH:].reshape(m, 1, KV_WIDTH))
    y = h.reshape(bsz, t, D_MODEL)
    if prompt:
        return y, jnp.stack(states, axis=1), k_state, v_state
    cache_shape = (m, WINDOW, B_KV_HEADS, B_HEAD_DIM)
    return y, new_state, k_state.reshape(cache_shape), v_state.reshape(cache_shape)


def kernel(x_prompt, x_sample, state_hgrn, cache_swa_k, cache_swa_v, c_prompt, c_sample,
           w_in_a, w_o_a, gnorm_a, lb_a, w_kv, w_ada_kv, b_ada_kv, kv_norm_w, w_q_b, w_o_b,
           sinks_b, rel_bias, norm_w, w_ada, b_ada, w_ffn_in, w_ffn_out, final_norm_w):
    n_p, n_s = c_prompt.shape[0], c_sample.shape[0]
    rows = -(-(n_p + n_s) // SUBLANES) * SUBLANES
    c_all = jnp.concatenate(
        [c_sample, c_prompt, jnp.zeros((rows - n_p - n_s, D_MODEL), F32)], axis=0)

    zeros, ones = jnp.zeros((DEPTH, D_MODEL), F32), jnp.ones((DEPTH, D_MODEL), F32)
    p_ada = jnp.stack([zeros, norm_w[:, 0], zeros, zeros, norm_w[:, 1], zeros], axis=1)
    q_ada = jnp.stack([ones, norm_w[:, 0], ones, ones, norm_w[:, 1], ones], axis=1)
    ada = ada_project(c_all, w_ada, b_ada.reshape(DEPTH, 1, 6 * D_MODEL),
                      p_ada.reshape(DEPTH, 1, 6 * D_MODEL), q_ada.reshape(DEPTH, 1, 6 * D_MODEL))
    p_kv = jnp.concatenate([zeros[0], kv_norm_w]).reshape(1, 1, 2 * D_MODEL)
    q_kv = jnp.concatenate([ones[0], kv_norm_w]).reshape(1, 1, 2 * D_MODEL)
    ada_kv = ada_project(c_all, w_ada_kv.reshape(1, D_MODEL, 2 * D_MODEL),
                         b_ada_kv.reshape(1, 1, 2 * D_MODEL), p_kv, q_kv)

    def mods_for(row0):
        layers = [tuple((ada, l, c, row0) for c in (1, 0, 2, 4, 3, 5)) for l in range(DEPTH)]
        return layers, ((ada_kv, 0, 1, row0), (ada_kv, 0, 0, row0))

    lb_sm = jax.nn.softmax(lb_a.astype(F32), axis=0)
    lbs = jnp.cumsum(lb_sm, axis=0) - lb_sm[0:1]

    rb = rel_bias.astype(F32)[_T5_BUCKETS]
    t_idx, j_idx = np.arange(WINDOW)[:, None], np.arange(WINDOW)[None, :]
    from_prev = j_idx > t_idx
    dist = np.where(from_prev, t_idx + WINDOW - j_idx, t_idx - j_idx)
    onehot = (jnp.asarray(dist)[:, :, None] == jnp.arange(WINDOW)[None, None, :]).astype(F32)
    tab = jnp.einsum("tjd,dh->htj", onehot, rb, precision=lax.Precision.HIGHEST) * LOG2E
    bias_tab = jnp.stack([jnp.where(from_prev, MASK_VALUE, tab),
                          tab])
    bias_row = rb[::-1].T * LOG2E

    bf = lambda w: w.astype(BF16)
    wts = (bf(w_in_a), bf(w_o_a), gnorm_a, bf(w_q_b), bf(w_o_b), sinks_b, bf(w_ffn_in),
           bf(w_ffn_out), final_norm_w, bf(w_kv)[None])
    mods_p, kv_mod_p = mods_for(n_s)
    mods_s, kv_mod_s = mods_for(0)
    y_p, st_p, k_p, v_p = _trunk(x_prompt, mods_p, kv_mod_p, False, None, None, None, wts, lbs,
                                 bias_tab, bias_row)
    y_s, st_s, k_s, v_s = _trunk(x_sample, mods_s, kv_mod_s, True, state_hgrn, cache_swa_k,
                                 cache_swa_v, wts, lbs, bias_tab, bias_row)
    return (y_p, y_s, st_p, st_s, k_p, v_p, k_s, v_s)
```
